```python
import math
import jax
import jax.numpy as jnp
from jax import lax
import numpy as np

D_MODEL = 2048
BATCH = 32
SEQ = 256
DEPTH = 4
DEC_BATCH = 2
DEC_SEQ = 2048
PAST_LEN = 256

GRID_W = 64
N_MIXERS = 3
N_ATTN = (DEPTH + 2) // 3
N_GMLP = (DEPTH + 1) // 3
N_HYENA = DEPTH // 3
N_DENSE = (DEPTH + 1) // 2
N_MOE = DEPTH // 2
N_HEADS = 16
HEAD_DIM = 64
V_DIM = 2 * HEAD_DIM
Q_BLOCK = 128
ROPE_THETA = 10000.0
CHUNK = 128
GMLP_GROUPS = 16
GMLP_GROUP_DIM = D_MODEL // GMLP_GROUPS
HYENA_ORDER = 2
HYENA_SHORT = 3
HYENA_BANDS = 16
HYENA_EMB = 1 + 2 * HYENA_BANDS
HYENA_FH = 64
HYENA_FAST_DECAY = 0.3
HYENA_SLOW_DECAY = 1.5
HYENA_TARGET = 0.01
D_FF = 5632
N_EXPERTS = 8
TOP_K = 2
MOE_D_FF = 2816
EPS = 1e-6

kernel_name = "hybrid_diffattn_gmlp_hyena_dit_step"


def rmsnorm(x, g):
    xf = x.astype(jnp.float32)
    y = xf * lax.rsqrt(jnp.mean(xf * xf, axis=-1, keepdims=True) + EPS)
    return (y * g.astype(jnp.float32)).astype(x.dtype)


def layernorm(x, g, b):
    xf = x.astype(jnp.float32)
    mu = jnp.mean(xf, axis=-1, keepdims=True)
    var = jnp.mean(jnp.square(xf - mu), axis=-1, keepdims=True)
    y = (xf - mu) * lax.rsqrt(var + EPS)
    return (y * g.astype(jnp.float32) + b.astype(jnp.float32)).astype(x.dtype)


def modulation(cond, w, b):
    m = jnp.einsum("bd,de->be", jax.nn.silu(cond), w) + b
    return jnp.split(m[:, None, :], 6, axis=-1)


def rope_axis(x, pos):
    half = x.shape[-1] // 2
    inv = ROPE_THETA ** (-jnp.arange(half, dtype=jnp.float32) / half)
    ang = pos.astype(jnp.float32)[:, None] * inv[None, :]
    cos = jnp.cos(ang)[:, None, :]
    sin = jnp.sin(ang)[:, None, :]
    xf = x.astype(jnp.float32)
    x1, x2 = xf[..., :half], xf[..., half:]
    return jnp.concatenate([x1 * cos - x2 * sin, x2 * cos + x1 * sin], axis=-1).astype(x.dtype)


def rope_2d(x):
    rows = x.shape[1] // GRID_W
    row = jnp.repeat(jnp.arange(rows), GRID_W)
    col = jnp.tile(jnp.arange(GRID_W), rows)
    h = x.shape[-1] // 2
    return jnp.concatenate([rope_axis(x[..., :h], row), rope_axis(x[..., h:], col)], axis=-1)


def diff_lambda(lam_p, layer):
    lam_init = 0.8 - 0.6 * math.exp(-0.3 * layer)
    p = lam_p.astype(jnp.float32)
    lam = jnp.exp(jnp.sum(p[0] * p[1])) - jnp.exp(jnp.sum(p[2] * p[3])) + lam_init
    return lam, lam_init


def diff_attention(q, k, v, lam):
    b, s = q.shape[:2]
    nk = k.shape[1]
    nb = s // Q_BLOCK
    qb = jnp.swapaxes(q.reshape(b, nb, Q_BLOCK, 2 * N_HEADS, HEAD_DIM), 0, 1)
    scale = HEAD_DIM ** -0.5

    def block(qi):
        sc = jnp.einsum("bqhd,bkhd->bhqk", qi, k, preferred_element_type=jnp.float32) * scale
        p = jax.nn.softmax(sc, axis=-1).reshape(b, N_HEADS, 2, Q_BLOCK, nk)
        a = p[:, :, 0] - lam * p[:, :, 1]
        return jnp.einsum("bhqk,bkhe->bqhe", a.astype(v.dtype), v)

    o = lax.map(block, qb)
    return jnp.swapaxes(o, 0, 1).reshape(b, s, N_HEADS, V_DIM)


def attn_project(h, w_in):
    b, s, _ = h.shape
    q, k, v = jnp.split(jnp.einsum("bsd,de->bse", h, w_in), 3, axis=-1)
    return (q.reshape(b, s, 2 * N_HEADS, HEAD_DIM),
            k.reshape(b, s, 2 * N_HEADS, HEAD_DIM),
            v.reshape(b, s, N_HEADS, V_DIM))


def attn_output(o, subln, lam_init, w_out):
    b, s = o.shape[:2]
    o = rmsnorm(o, subln) * (1.0 - lam_init)
    return jnp.einsum("bsd,de->bse", o.reshape(b, s, D_MODEL), w_out)


def gmlp_mixer(h, w_in, ln_g, ln_b, w_s, b_s, w_out):
    b, s, _ = h.shape
    u, v = jnp.split(jax.nn.gelu(jnp.einsum("bsd,de->bse", h, w_in), approximate=False), 2, axis=-1)
    v = layernorm(v, ln_g, ln_b)
    vc = v.reshape(b, s // CHUNK, CHUNK, GMLP_GROUPS, GMLP_GROUP_DIM)
    vm = jnp.einsum("gpq,bnqgc->bnpgc", w_s, vc) + jnp.transpose(b_s)[:, :, None]
    return jnp.einsum("bsd,de->bse", u * vm.reshape(b, s, D_MODEL), w_out)


def hyena_filters(L, w1, b1, w2, b2, w3, freq):
    f32 = jnp.float32
    pos = jnp.arange(L, dtype=f32)
    t = jnp.linspace(0.0, 1.0, L, dtype=f32)[:, None]
    bands = jnp.linspace(1e-4, HYENA_BANDS - 1, HYENA_BANDS, dtype=f32)
    ang = (2.0 * math.pi / L) * pos[:, None] * bands[None, :]
    z = jnp.concatenate([t, jnp.cos(ang), -jnp.sin(ang)], axis=-1)
    fr = freq.astype(f32)
    hdn = jnp.sin(fr[0] * (z @ w1.astype(f32) + b1.astype(f32)))
    hdn = jnp.sin(fr[1] * (hdn @ w2.astype(f32) + b2.astype(f32)))
    filt = (hdn @ w3.astype(f32)).reshape(L, HYENA_ORDER, 2, D_MODEL)
    max_decay = math.log(HYENA_TARGET) / HYENA_FAST_DECAY
    min_decay = math.log(HYENA_TARGET) / HYENA_SLOW_DECAY
    deltas = jnp.abs(jnp.linspace(min_decay, max_decay, D_MODEL, dtype=f32))
    filt = filt * jnp.exp(-t * deltas[None, :])[:, None, None, :]
    fwd = filt[:, :, 0]
    bwd = filt[1:, :, 1][::-1]
    circ = jnp.concatenate([fwd, jnp.zeros((1, HYENA_ORDER, D_MODEL), f32), bwd], axis=0)
    circ = circ / (jnp.sum(jnp.abs(circ), axis=0, keepdims=True) + EPS)
    return jnp.fft.rfft(circ, axis=0)


def long_conv(z, khat, bias):
    L = z.shape[1]
    zf = z.astype(jnp.float32)
    y = jnp.fft.irfft(jnp.fft.rfft(zf, n=2 * L, axis=1) * khat[None], n=2 * L, axis=1)[:, :L]
    return (y + zf * bias.astype(jnp.float32)).astype(z.dtype)


def hyena_mixer(h, w_in, b_in, conv_w, conv_b, w1, b1, w2, b2, w3, freq, bias, w_out):
    L = h.shape[1]
    p = jnp.einsum("bsd,de->bse", h, w_in) + b_in
    pad = HYENA_SHORT // 2
    pp = jnp.pad(p, ((0, 0), (pad, pad), (0, 0)))
    p = conv_b + sum(pp[:, j:j + L] * conv_w[j] for j in range(HYENA_SHORT))
    v, x1, x2 = jnp.split(p, 3, axis=-1)
    khat = hyena_filters(L, w1, b1, w2, b2, w3, freq)
    z = x1 * long_conv(v, khat[:, 0], bias[0])
    z = x2 * long_conv(z, khat[:, 1], bias[1])
    return jnp.einsum("bsd,de->bse", z, w_out)


def swiglu(h, wg, wu, wd):
    a = jax.nn.silu(jnp.einsum("bsd,df->bsf", h, wg)) * jnp.einsum("bsd,df->bsf", h, wu)
    return jnp.einsum("bsf,fd->bsd", a, wd)


def moe_swiglu(h, router, wg, wu, wd):
    logits = jnp.einsum("bsd,de->bse", h, router).astype(jnp.float32)
    top_v, top_i = lax.top_k(logits, TOP_K)
    w = jax.nn.softmax(top_v, axis=-1)
    gate = jnp.sum(jax.nn.one_hot(top_i, N_EXPERTS, dtype=jnp.float32) * w[..., None], axis=-2)
    y = jnp.zeros(h.shape, jnp.float32)
    for e in range(N_EXPERTS):
        y = y + gate[..., e:e + 1] * swiglu(h, wg[e], wu[e], wd[e]).astype(jnp.float32)
    return y.astype(h.dtype)


def setup_inputs(seed: int = 0) -> dict:
    key = jax.random.key(seed)
    keys = iter(jax.random.split(key, 64))

    def nrm(shape, scale):
        return jax.random.normal(next(keys), shape, jnp.float32) * scale

    D = D_MODEL
    inv = D ** -0.5
    return {
        "x_prompt": nrm((BATCH, SEQ, D), 1.0),
        "x_sample": nrm((DEC_BATCH, DEC_SEQ, D), 1.0),
        "cache_k": nrm((DEC_BATCH, N_ATTN, PAST_LEN, 2 * N_HEADS, HEAD_DIM), 1.0),
        "cache_v": nrm((DEC_BATCH, N_ATTN, PAST_LEN, N_HEADS, V_DIM), 1.0),
        "c": nrm((DEC_BATCH, D), 1.0),
        "c_ctx": nrm((D,), 1.0),
        "ada_w": nrm((DEPTH, D, 6 * D), 0.5 * inv),
        "ada_b": nrm((DEPTH, 6 * D), 0.02),
        "norm_g": 1.0 + nrm((DEPTH, 4, D), 0.1),
        "attn_w_in": nrm((N_ATTN, D, 3 * D), inv),
        "attn_lambda": nrm((N_ATTN, 4, HEAD_DIM), 0.1),
        "attn_subln": 1.0 + nrm((N_ATTN, V_DIM), 0.1),
        "attn_w_out": nrm((N_ATTN, D, D), inv),
        "gmlp_w_in": nrm((N_GMLP, D, 2 * D), inv),
        "gmlp_ln_g": 1.0 + nrm((N_GMLP, D), 0.1),
        "gmlp_ln_b": nrm((N_GMLP, D), 0.02),
        "gmlp_w_s": nrm((N_GMLP, GMLP_GROUPS, CHUNK, CHUNK), CHUNK ** -0.5),
        "gmlp_b_s": 1.0 + nrm((N_GMLP, GMLP_GROUPS, CHUNK), 0.1),
        "gmlp_w_out": nrm((N_GMLP, D, D), inv),
        "hyena_w_in": nrm((N_HYENA, D, 3 * D), inv),
        "hyena_b_in": nrm((N_HYENA, 3 * D), 0.02),
        "hyena_conv_w": nrm((N_HYENA, HYENA_SHORT, 3 * D), HYENA_SHORT ** -0.5),
        "hyena_conv_b": nrm((N_HYENA, 3 * D), 0.02),
        "hyena_ffn_w1": nrm((N_HYENA, HYENA_EMB, HYENA_FH), HYENA_EMB ** -0.5),
        "hyena_ffn_b1": nrm((N_HYENA, HYENA_FH), 0.1),
        "hyena_ffn_w2": nrm((N_HYENA, HYENA_FH, HYENA_FH), HYENA_FH ** -0.5),
        "hyena_ffn_b2": nrm((N_HYENA, HYENA_FH), 0.1),
        "hyena_ffn_w3": nrm((N_HYENA, HYENA_FH, HYENA_ORDER * 2 * D), HYENA_FH ** -0.5),
        "hyena_sin_freq": 1.0 + nrm((N_HYENA, 2, HYENA_FH), 0.1),
        "hyena_bias": nrm((N_HYENA, HYENA_ORDER, D), 0.1),
        "hyena_w_out": nrm((N_HYENA, D, D), inv),
        "ffn_w_gate": nrm((N_DENSE, D, D_FF), inv),
        "ffn_w_up": nrm((N_DENSE, D, D_FF), inv),
        "ffn_w_down": nrm((N_DENSE, D_FF, D), D_FF ** -0.5),
        "moe_router": nrm((N_MOE, D, N_EXPERTS), inv),
        "moe_w_gate": nrm((N_MOE, N_EXPERTS, D, MOE_D_FF), inv),
        "moe_w_up": nrm((N_MOE, N_EXPERTS, D, MOE_D_FF), inv),
        "moe_w_down": nrm((N_MOE, N_EXPERTS, MOE_D_FF, D), MOE_D_FF ** -0.5),
    }


def reference(x_prompt, x_sample, cache_k, cache_v, c, c_ctx, ada_w, ada_b, norm_g,
              attn_w_in, attn_lambda, attn_subln, attn_w_out,
              gmlp_w_in, gmlp_ln_g, gmlp_ln_b, gmlp_w_s, gmlp_b_s, gmlp_w_out,
              hyena_w_in, hyena_b_in, hyena_conv_w, hyena_conv_b, hyena_ffn_w1, hyena_ffn_b1,
              hyena_ffn_w2, hyena_ffn_b2, hyena_ffn_w3, hyena_sin_freq, hyena_bias, hyena_w_out,
              ffn_w_gate, ffn_w_up, ffn_w_down, moe_router, moe_w_gate, moe_w_up, moe_w_down):

    def token_mixer(h, i, is_ctx, ctx_kv):
        kind, j = i % N_MIXERS, i // N_MIXERS
        if kind == 0:
            q, k, v = attn_project(h, attn_w_in[j])
            lam, lam_init = diff_lambda(attn_lambda[j], i)
            if is_ctx:
                o = diff_attention(q, k, v, lam)
                new_kv = (k, v)
            else:
                keys = jnp.concatenate([ctx_kv[0], rope_2d(k)], axis=1)
                vals = jnp.concatenate([ctx_kv[1], v], axis=1)
                o = diff_attention(rope_2d(q), keys, vals, lam)
                new_kv = None
            return attn_output(o, attn_subln[j], lam_init, attn_w_out[j]), new_kv
        if kind == 1:
            return gmlp_mixer(h, gmlp_w_in[j], gmlp_ln_g[j], gmlp_ln_b[j], gmlp_w_s[j],
                              gmlp_b_s[j], gmlp_w_out[j]), None
        return hyena_mixer(h, hyena_w_in[j], hyena_b_in[j], hyena_conv_w[j], hyena_conv_b[j],
                           hyena_ffn_w1[j], hyena_ffn_b1[j], hyena_ffn_w2[j], hyena_ffn_b2[j],
                           hyena_ffn_w3[j], hyena_sin_freq[j], hyena_bias[j], hyena_w_out[j]), None

    def channel_mixer(h, i):
        j = i // 2
        if i % 2 == 0:
            return swiglu(h, ffn_w_gate[j], ffn_w_up[j], ffn_w_down[j])
        return moe_swiglu(h, moe_router[j], moe_w_gate[j], moe_w_up[j], moe_w_down[j])

    def layer(x, i, cond, is_ctx, ctx_kv):
        sh1, sc1, g1, sh2, sc2, g2 = modulation(cond, ada_w[i], ada_b[i])
        h = rmsnorm(x, norm_g[i, 0]) * (1.0 + sc1) + sh1
        out, kv = token_mixer(h, i, is_ctx, ctx_kv)
        x = x + g1 * rmsnorm(out, norm_g[i, 1])
        h = rmsnorm(x, norm_g[i, 2]) * (1.0 + sc2) + sh2
        x = x + g2 * rmsnorm(channel_mixer(h, i), norm_g[i, 3])
        return x, kv

    xp = x_prompt
    cond_ctx = c_ctx[None, :]
    ks, vs = [], []
    for i in range(DEPTH):
        xp, kv = layer(xp, i, cond_ctx, True, None)
        if kv is not None:
            ks.append(kv[0])
            vs.append(kv[1])
    new_cache_k = jnp.stack(ks, axis=1)
    new_cache_v = jnp.stack(vs, axis=1)

    xs = x_sample
    for i in range(DEPTH):
        ctx_kv = (cache_k[:, i // N_MIXERS], cache_v[:, i // N_MIXERS]) if i % N_MIXERS == 0 else None
        xs, _ = layer(xs, i, c, False, ctx_kv)

    return (xp, xs, new_cache_k, new_cache_v)
```

```python
import functools
import math

import jax
import jax.numpy as jnp
from jax import lax
from jax.experimental import pallas as pl
from jax.experimental.pallas import tpu as pltpu

F32 = jnp.float32
BF16 = jnp.bfloat16

EPS = 1e-6
GRID_W = 64
ROPE_THETA = 10000.0
CHUNK = 128
HYENA_BANDS = 16
HYENA_FAST_DECAY = 0.3
HYENA_SLOW_DECAY = 1.5
HYENA_TARGET = 0.01
HEAD_DIM = 64
TOP_K = 2

V7X_VMEM_BYTES = 64 * 1024 * 1024
VMEM_LIMIT = V7X_VMEM_BYTES - 8 * 1024 * 1024
LANES = 128
TAIL_ROWS = 128
FREQ_ROWS = 512

SH1, SC1, G1, SH2, SC2, G2 = range(6)


def _pick(n, *cands):
    for c in cands:
        if n % c == 0:
            return c
    return n


def _params(*sem):
    return pltpu.CompilerParams(dimension_semantics=sem, vmem_limit_bytes=VMEM_LIMIT)


def _const_spec(shape):
    nd = len(shape)
    return pl.BlockSpec(shape, lambda *_: (0,) * nd, pipeline_mode=pl.Buffered(1))


def _rms(x):
    return x * lax.rsqrt(jnp.mean(x * x, axis=-1, keepdims=True) + EPS)


class _Rows:
    def __init__(self, n_p, n_s, dec_seq):
        self.n_p, self.n_s, self.dec_seq = n_p, n_s, dec_seq
        self.m = n_p + n_s
        self.nseg = 1 + n_s // dec_seq

    def seg(self, i, bm):
        r = i * bm
        return jnp.where(r < self.n_p, 0, 1 + (r - self.n_p) // self.dec_seq)

    def tile(self, *cands):
        return _pick(math.gcd(self.n_p, self.dec_seq), *cands)


def _mod_kernel(cond_ref, w_ref, b_ref, o_ref):
    s = cond_ref[...]
    s = s * jax.nn.sigmoid(s)
    o_ref[...] = jnp.dot(s.astype(BF16), w_ref[...].astype(BF16), preferred_element_type=F32) + b_ref[...]


def _modulation(cond8, ada_w, ada_b):
    depth, d, n6 = ada_w.shape
    bn = _pick(n6, 1024, 512, 256, 128)
    return pl.pallas_call(
        _mod_kernel,
        grid=(depth, n6 // bn),
        in_specs=[
            pl.BlockSpec((8, d), lambda l, j: (0, 0)),
            pl.BlockSpec((None, d, bn), lambda l, j: (l, 0, j)),
            pl.BlockSpec((None, 1, bn), lambda l, j: (l, 0, j)),
        ],
        out_specs=pl.BlockSpec((None, 8, bn), lambda l, j: (l, 0, j)),
        out_shape=jax.ShapeDtypeStruct((depth, 8, n6), F32),
        compiler_params=_params("arbitrary", "arbitrary"),
        name="modulation",
    )(cond8, ada_w, ada_b.reshape(depth, 1, n6))


def _pre_kernel(xp_ref, xs_ref, g_ref, mod_ref, x_ref, h_ref, *, tp):
    i = pl.program_id(0)

    def emit(src_ref):
        x = src_ref[...]
        x_ref[...] = x
        h = _rms(x) * g_ref[...] * (1.0 + mod_ref[SC1:SC1 + 1, :]) + mod_ref[SH1:SH1 + 1, :]
        h_ref[...] = h.astype(h_ref.dtype)

    @pl.when(i < tp)
    def _():
        emit(xp_ref)

    @pl.when(i >= tp)
    def _():
        emit(xs_ref)


def _prenorm_join(xp, xs, ng3, mod3, rows, layer):
    d = xp.shape[-1]
    bm = rows.tile(512, 256, 128)
    tp = rows.n_p // bm
    ts = rows.n_s // bm
    return pl.pallas_call(
        functools.partial(_pre_kernel, tp=tp),
        grid=(tp + ts,),
        in_specs=[
            pl.BlockSpec((bm, d), lambda i: (jnp.minimum(i, tp - 1), 0)),
            pl.BlockSpec((bm, d), lambda i: (jnp.maximum(i - tp, 0), 0)),
            pl.BlockSpec((None, 1, d), lambda i: (layer * 4, 0, 0)),
            pl.BlockSpec((None, 6, d), lambda i: (layer * rows.nseg + rows.seg(i, bm), 0, 0)),
        ],
        out_specs=[pl.BlockSpec((bm, d), lambda i: (i, 0)), pl.BlockSpec((bm, d), lambda i: (i, 0))],
        out_shape=[jax.ShapeDtypeStruct((rows.m, d), F32), jax.ShapeDtypeStruct((rows.m, d), BF16)],
        compiler_params=_params("arbitrary"),
        name="prenorm_join",
    )(xp, xs, ng3, mod3)


def _proj_kernel(*refs, epilogue, half, cache_rows, cache_slot=None):
    if epilogue == "bias":
        x_ref, w_ref, b_ref, o_ref = refs
    elif epilogue == "rope":
        x_ref, w_ref, cos_ref, sin_ref, o_ref = refs
    else:
        x_ref, w_ref, o_ref = refs
    acc = jnp.dot(x_ref[...], w_ref[...].astype(BF16), preferred_element_type=F32)
    if epilogue == "bias":
        acc = acc + b_ref[...]
    elif epilogue == "gelu":
        acc = 0.5 * acc * (1.0 + lax.erf(acc * (2.0 ** -0.5)))
    if epilogue == "rope":
        cos_t = cos_ref[...]
        sin_t = sin_ref[...]
        lane = lax.broadcasted_iota(jnp.int32, cos_t.shape, 1)
        first = (lane % (2 * half)) < half
        for cblk in range(acc.shape[1] // LANES):
            blk = acc[:, cblk * LANES:(cblk + 1) * LANES]
            partner = jnp.where(first, pltpu.roll(blk, LANES - half, 1), pltpu.roll(blk, half, 1))
            o_ref[:, cblk * LANES:(cblk + 1) * LANES] = (blk * cos_t + partner * sin_t).astype(o_ref.dtype)
    elif cache_slot is not None:
        o_ref[...] = jnp.zeros(o_ref.shape, o_ref.dtype)
        o_ref[:, cache_slot] = acc.reshape(o_ref.shape[0], *o_ref.shape[2:]).astype(o_ref.dtype)
    elif cache_rows:
        o_ref[...] = acc.reshape(o_ref.shape).astype(o_ref.dtype)
    else:
        o_ref[...] = acc.astype(o_ref.dtype)


def _proj(h, w, layer, *, row0, nrows, col0, ncols, out_dtype=BF16, epilogue="none", bias=None,
          rope=None, seq=None, cache=None):
    k = h.shape[1]
    bm = _pick(math.gcd(math.gcd(row0, nrows), seq or 0), 1024, 512, 256, 128)
    bn = _pick(math.gcd(col0, ncols) if col0 else ncols, 1024, 512, 256, 128)
    r0, c0 = row0 // bm, col0 // bn
    in_specs = [
        pl.BlockSpec((bm, k), lambda j, i: (r0 + i, 0)),
        pl.BlockSpec((None, k, bn), lambda j, i: (layer, 0, c0 + j)),
    ]
    args = [h, w]
    half = 0
    if epilogue == "bias":
        in_specs.append(pl.BlockSpec((None, 1, bn), lambda j, i: (layer, 0, c0 + j)))
        args.append(bias.reshape(bias.shape[0], 1, bias.shape[1]))
    elif epilogue == "rope":
        cos_t, sin_t, half = rope
        per_seq = seq // bm
        in_specs += [pl.BlockSpec((bm, LANES), lambda j, i: (i % per_seq, 0))] * 2
        args += [cos_t, sin_t]
    aliases = {}
    first_slot = None
    if cache is None:
        out_spec = pl.BlockSpec((bm, bn), lambda j, i: (i, j))
        out_shape = jax.ShapeDtypeStruct((nrows, ncols), out_dtype)
    else:
        arr, n_slots, slot, batch, cseq = cache
        bb = bm // cseq
        out_shape = jax.ShapeDtypeStruct((batch, n_slots, cseq, ncols), out_dtype)
        if arr is None:
            out_spec = pl.BlockSpec((bb, n_slots, cseq, bn), lambda j, i: (i, 0, 0, j))
            first_slot = slot
        else:
            out_spec = pl.BlockSpec((bb, None, cseq, bn), lambda j, i: (i, slot, 0, j))
            in_specs.append(pl.BlockSpec(memory_space=pl.ANY))
            args.append(arr)
            aliases = {len(args) - 1: 0}

    def body(*refs):
        if aliases:
            refs = refs[:len(args) - 1] + refs[len(args):]
        _proj_kernel(*refs, epilogue=epilogue, half=half, cache_rows=cache is not None, cache_slot=first_slot)

    return pl.pallas_call(
        body,
        grid=(ncols // bn, nrows // bm),
        in_specs=in_specs,
        out_specs=out_spec,
        out_shape=out_shape,
        input_output_aliases=aliases,
        compiler_params=_params("arbitrary", "arbitrary"),
        name="proj_" + epilogue,
    )(*args)


def _rope_tables(dec_seq):
    rows = dec_seq // GRID_W
    row = jnp.repeat(jnp.arange(rows), GRID_W).astype(F32)
    col = jnp.tile(jnp.arange(GRID_W), rows).astype(F32)
    half = HEAD_DIM // 4
    inv = ROPE_THETA ** (-jnp.arange(half, dtype=F32) / half)
    ar = row[:, None] * inv[None, :]
    ac = col[:, None] * inv[None, :]
    cos64 = jnp.concatenate([jnp.cos(ar), jnp.cos(ar), jnp.cos(ac), jnp.cos(ac)], axis=-1)
    sin64 = jnp.concatenate([-jnp.sin(ar), jnp.sin(ar), -jnp.sin(ac), jnp.sin(ac)], axis=-1)
    reps = LANES // HEAD_DIM
    return jnp.tile(cos64, (1, reps)), jnp.tile(sin64, (1, reps)), half


def _attn_kernel(*refs, heads, has_cache, lam_init):
    if has_cache:
        lam_ref, sub_ref, q_ref, k_ref, v_ref, kc_ref, vc_ref, o_ref = refs
    else:
        lam_ref, sub_ref, q_ref, k_ref, v_ref, o_ref = refs
    p = lam_ref[...]
    lam = (jnp.exp(jnp.sum(p[0:1] * p[1:2], axis=-1, keepdims=True))
           - jnp.exp(jnp.sum(p[2:3] * p[3:4], axis=-1, keepdims=True)) + lam_init)
    scale = HEAD_DIM ** -0.5
    nt = (((1,), (1,)), ((), ()))
    for h in range(heads):
        sl = slice(h * 2 * HEAD_DIM, (h + 1) * 2 * HEAD_DIM)
        q = q_ref[:, sl]
        k = k_ref[:, sl].astype(BF16)
        v = v_ref[:, sl].astype(BF16)
        if has_cache:
            kc = kc_ref[:, sl].astype(BF16)
            vc = vc_ref[:, sl].astype(BF16)
        a_own = None
        a_cache = None
        for m in range(2):
            ms = slice(m * HEAD_DIM, (m + 1) * HEAD_DIM)
            qm = q[:, ms]
            s = lax.dot_general(qm, k[:, ms], nt, preferred_element_type=F32) * scale
            mx = jnp.max(s, axis=-1, keepdims=True)
            if has_cache:
                sc = lax.dot_general(qm, kc[:, ms], nt, preferred_element_type=F32) * scale
                mx = jnp.maximum(mx, jnp.max(sc, axis=-1, keepdims=True))
                ec = jnp.exp(sc - mx)
            e = jnp.exp(s - mx)
            den = jnp.sum(e, axis=-1, keepdims=True)
            if has_cache:
                den = den + jnp.sum(ec, axis=-1, keepdims=True)
            r = 1.0 / den
            if m == 0:
                a_own = e * r
                if has_cache:
                    a_cache = ec * r
            else:
                r = r * lam
                a_own = a_own - e * r
                if has_cache:
                    a_cache = a_cache - ec * r
        o = jnp.dot(a_own.astype(BF16), v, preferred_element_type=F32)
        if has_cache:
            o = o + jnp.dot(a_cache.astype(BF16), vc, preferred_element_type=F32)
        o = _rms(o) * sub_ref[...] * (1.0 - lam_init)
        o_ref[:, sl] = o.astype(o_ref.dtype)


def _attention(q, q_col0, k4, k_slot, k_col0, v4, v_slot, cache4, lam, subln, layer_j, lam_init, *, batch, seq,
               heads_per_step, bq):
    d = v4.shape[-1]
    hw = heads_per_step * 2 * HEAD_DIM
    nk = k4.shape[2]
    per_seq = seq // bq
    qc0, kc0 = q_col0 // hw, k_col0 // hw
    in_specs = [
        pl.BlockSpec((None, 4, HEAD_DIM), lambda b, h, i: (layer_j, 0, 0)),
        pl.BlockSpec((None, 1, 2 * HEAD_DIM), lambda b, h, i: (layer_j, 0, 0)),
        pl.BlockSpec((bq, hw), lambda b, h, i: (b * per_seq + i, qc0 + h)),
        pl.BlockSpec((None, None, nk, hw), lambda b, h, i: (b, k_slot, 0, kc0 + h)),
        pl.BlockSpec((None, None, nk, hw), lambda b, h, i: (b, v_slot, 0, h)),
    ]
    args = [lam, subln.reshape(subln.shape[0], 1, subln.shape[1]), q, k4, v4]
    if cache4 is not None:
        ck, cv = cache4
        nc = ck.shape[2]
        in_specs += [pl.BlockSpec((None, None, nc, hw), lambda b, h, i: (b, layer_j, 0, h))] * 2
        args += [ck, cv]
    return pl.pallas_call(
        functools.partial(_attn_kernel, heads=heads_per_step, has_cache=cache4 is not None, lam_init=lam_init),
        grid=(batch, d // hw, per_seq),
        in_specs=in_specs,
        out_specs=pl.BlockSpec((bq, hw), lambda b, h, i: (b * per_seq + i, h)),
        out_shape=jax.ShapeDtypeStruct((batch * seq, d), BF16),
        compiler_params=_params("arbitrary", "arbitrary", "arbitrary"),
        name="diff_attention",
    )(*args)


def _gmlp_kernel(u_ref, v_ref, g_ref, b_ref, ws_ref, bs_ref, o_ref, *, chunks, groups, gd):
    v = v_ref[...].astype(F32)
    mu = jnp.mean(v, axis=-1, keepdims=True)
    vc = v - mu
    var = jnp.mean(vc * vc, axis=-1, keepdims=True)
    vn = (vc * lax.rsqrt(var + EPS) * g_ref[...] + b_ref[...]).astype(BF16)
    for g in range(groups):
        w = ws_ref[g].astype(BF16)
        cs = slice(g * gd, (g + 1) * gd)
        for c in range(chunks):
            rs = slice(c * CHUNK, (c + 1) * CHUNK)
            vm = jnp.dot(w, vn[rs, cs], preferred_element_type=F32) + bs_ref[:, cs]
            o_ref[rs, cs] = (u_ref[rs, cs].astype(F32) * vm).astype(o_ref.dtype)


def _gmlp_gate(uv, ln_g, ln_b, w_s, b_s, layer_j):
    m, d2 = uv.shape
    d = d2 // 2
    groups = w_s.shape[1]
    gd = d // groups
    tm = _pick(m, 512, 256, 128)
    bs_full = jnp.repeat(jnp.transpose(b_s[layer_j]), gd, axis=1)
    return pl.pallas_call(
        functools.partial(_gmlp_kernel, chunks=tm // CHUNK, groups=groups, gd=gd),
        grid=(m // tm,),
        in_specs=[
            pl.BlockSpec((tm, d), lambda i: (i, 0)),
            pl.BlockSpec((tm, d), lambda i: (i, 1)),
            pl.BlockSpec((None, 1, d), lambda i: (layer_j, 0, 0)),
            pl.BlockSpec((None, 1, d), lambda i: (layer_j, 0, 0)),
            pl.BlockSpec((None, groups, CHUNK, CHUNK), lambda i: (layer_j, 0, 0, 0)),
            pl.BlockSpec((CHUNK, d), lambda i: (0, 0)),
        ],
        out_specs=pl.BlockSpec((tm, d), lambda i: (i, 0)),
        out_shape=jax.ShapeDtypeStruct((m, d), BF16),
        compiler_params=_params("arbitrary"),
        name="gmlp_gate",
    )(uv, uv, ln_g.reshape(ln_g.shape[0], 1, d), ln_b.reshape(ln_b.shape[0], 1, d), w_s, bs_full)


def _dft_matrix(length):
    n = 2 * length
    f = jnp.arange(length, dtype=jnp.int32)
    m = (f[:, None] * f[None, :]) % n
    ang = m.astype(F32) * (2.0 * math.pi / n)
    return jnp.concatenate([jnp.cos(ang), -jnp.sin(ang)], axis=0).astype(BF16)


def _filter_features(length):
    pos = jnp.arange(length, dtype=F32)
    t = jnp.linspace(0.0, 1.0, length, dtype=F32)[:, None]
    bands = jnp.linspace(1e-4, HYENA_BANDS - 1, HYENA_BANDS, dtype=F32)
    ang = (2.0 * math.pi / length) * pos[:, None] * bands[None, :]
    z = jnp.concatenate([t, jnp.cos(ang), -jnp.sin(ang)], axis=-1)
    return jnp.pad(z, ((0, 0), (0, LANES - z.shape[1])))


def _filt_kernel(z_ref, w1_ref, b1_ref, w2_ref, b2_ref, fr_ref, w00, w01, w10, w11, dl_ref, a_ref, b_ref):
    hi = lax.Precision.HIGHEST
    z = z_ref[...]
    h = jnp.sin(fr_ref[0:1, :] * (jnp.dot(z, w1_ref[...], precision=hi, preferred_element_type=F32) + b1_ref[...]))
    h = jnp.sin(fr_ref[1:2, :] * (jnp.dot(h, w2_ref[...], precision=hi, preferred_element_type=F32) + b2_ref[...]))
    hb = h.astype(BF16)
    decay = jnp.exp(-z[:, 0:1] * dl_ref[...])
    row = lax.broadcasted_iota(jnp.int32, (z.shape[0], 1), 0)
    for o, (wf, wb) in enumerate(((w00, w01), (w10, w11))):
        fwd = jnp.dot(hb, wf[...].astype(BF16), preferred_element_type=F32) * decay
        bwd = jnp.dot(hb, wb[...].astype(BF16), preferred_element_type=F32) * decay
        bwd = jnp.where(row == 0, 0.0, bwd)
        norm = (jnp.sum(jnp.abs(fwd), axis=0, keepdims=True) + jnp.sum(jnp.abs(bwd), axis=0, keepdims=True) + EPS)
        inv = 1.0 / norm
        a_ref[o] = ((fwd + bwd) * inv).astype(a_ref.dtype)
        b_ref[o] = ((fwd - bwd) * inv).astype(b_ref.dtype)


def _hyena_filters(length, d, w1, b1, w2, b2, w3, freq, layer_j):
    fh = w2.shape[-1]
    emb = w1.shape[1]
    bd = _pick(d, 512, 256, 128)
    nb = d // bd
    zfeat = _filter_features(length)
    w1p = jnp.pad(w1[layer_j], ((0, LANES - emb), (0, 0)))
    max_decay = math.log(HYENA_TARGET) / HYENA_FAST_DECAY
    min_decay = math.log(HYENA_TARGET) / HYENA_SLOW_DECAY
    deltas = jnp.abs(jnp.linspace(min_decay, max_decay, d, dtype=F32))[None, :]
    w3_specs = [pl.BlockSpec((None, fh, bd), functools.partial(lambda c, g: (layer_j, 0, g * nb + c), g=g))
                for g in range(4)]
    out_spec = pl.BlockSpec((2, length, bd), lambda c: (0, 0, c))
    return pl.pallas_call(
        _filt_kernel,
        grid=(nb,),
        in_specs=[
            pl.BlockSpec((length, LANES), lambda c: (0, 0)),
            pl.BlockSpec((LANES, fh), lambda c: (0, 0)),
            pl.BlockSpec((None, 1, fh), lambda c: (layer_j, 0, 0)),
            pl.BlockSpec((None, fh, fh), lambda c: (layer_j, 0, 0)),
            pl.BlockSpec((None, 1, fh), lambda c: (layer_j, 0, 0)),
            pl.BlockSpec((None, 2, fh), lambda c: (layer_j, 0, 0)),
            *w3_specs,
            pl.BlockSpec((1, bd), lambda c: (0, c)),
        ],
        out_specs=[out_spec, out_spec],
        out_shape=[jax.ShapeDtypeStruct((2, length, d), BF16)] * 2,
        compiler_params=_params("arbitrary"),
        name="hyena_filters",
    )(zfeat, w1p, b1.reshape(b1.shape[0], 1, fh), w2, b2.reshape(b2.shape[0], 1, fh), freq, w3, w3, w3, w3, deltas)


def _spectrum_kernel(f_ref, a_ref, b_ref, kh_ref, kn_ref, *, length):
    n = 2 * length
    row = lax.broadcasted_iota(jnp.int32, (length, 1), 0)
    wn = jnp.where(row == 0, 1.0 / n, 2.0 / n)
    sgn = jnp.where(row % 2 == 0, 1.0, -1.0)
    a = a_ref[...]
    kh_ref[0:length, :] = jnp.dot(f_ref[0:length, :], a, preferred_element_type=F32) * wn
    kh_ref[length:n, :] = jnp.dot(f_ref[length:n, :], b_ref[...], preferred_element_type=F32) * wn
    nyq = jnp.sum(a.astype(F32) * sgn, axis=0, keepdims=True) * (1.0 / n)
    kn_ref[...] = jnp.broadcast_to(nyq, kn_ref.shape)


def _hyena_spectrum(fmat, a_tab, b_tab):
    _, length, d = a_tab.shape
    bd = _pick(d, 256, 128)
    tab_spec = pl.BlockSpec((None, length, bd), lambda o, c: (o, 0, c))
    return pl.pallas_call(
        functools.partial(_spectrum_kernel, length=length),
        grid=(2, d // bd),
        in_specs=[_const_spec((2 * length, length)), tab_spec, tab_spec],
        out_specs=[pl.BlockSpec((None, 2 * length, bd), lambda o, c: (o, 0, c)),
                   pl.BlockSpec((None, 8, bd), lambda o, c: (o, 0, c))],
        out_shape=[jax.ShapeDtypeStruct((2, 2 * length, d), F32), jax.ShapeDtypeStruct((2, 8, d), F32)],
        compiler_params=_params("arbitrary", "arbitrary"),
        name="hyena_spectrum",
    )(fmat, a_tab, b_tab)


def _hconv_kernel(f_ref, kh_ref, kn_ref, pv_ref, p1_ref, p2_ref, cwv, cw1, cw2, cbv, cb1, cb2, hb_ref, o_ref, y_scr,
                  *, length):
    n = 2 * length
    row = lax.broadcasted_iota(jnp.int32, (length, 1), 0)
    sgn = jnp.where(row % 2 == 0, 1.0, -1.0)

    def short_conv(p_ref, cw, cb):
        p = p_ref[...].astype(F32)
        prev = jnp.where(row == 0, 0.0, pltpu.roll(p, 1, 0))
        nxt = jnp.where(row == length - 1, 0.0, pltpu.roll(p, length - 1, 0))
        return cb[...] + (prev * cw[0:1, :] + p * cw[1:2, :] + nxt * cw[2:3, :])

    fb = min(length, FREQ_ROWS)

    def long_conv(z, o):
        zb = z.astype(BF16)

        def freq_block(i, carry):
            re = pl.ds(pl.multiple_of(i * fb, fb), fb)
            im = pl.ds(pl.multiple_of(length + i * fb, fb), fb)
            zr = jnp.dot(f_ref[re, :], zb, preferred_element_type=F32)
            zi = jnp.dot(f_ref[im, :], zb, preferred_element_type=F32)
            kr, ki = kh_ref[o, re, :], kh_ref[o, im, :]
            y_scr[re, :] = (zr * kr - zi * ki).astype(BF16)
            y_scr[im, :] = (zr * ki + zi * kr).astype(BF16)
            return carry

        lax.fori_loop(0, length // fb, freq_block, 0)
        nyq = jnp.sum(zb.astype(F32) * sgn, axis=0, keepdims=True) * kn_ref[o, 0:1, :]
        y = (jnp.dot(f_ref[0:length, :], y_scr[0:length, :], preferred_element_type=F32)
             + jnp.dot(f_ref[length:n, :], y_scr[length:n, :], preferred_element_type=F32) + sgn * nyq)
        return y + z * hb_ref[o:o + 1, :]

    z = short_conv(p1_ref, cw1, cb1) * long_conv(short_conv(pv_ref, cwv, cbv), 0)
    z = short_conv(p2_ref, cw2, cb2) * long_conv(z, 1)
    o_ref[...] = z.astype(o_ref.dtype)


def _hyena_conv(p, row0, batch, length, fmat, kh, kn, conv_w, conv_b, hbias, layer_j):
    d = p.shape[1] // 3
    bd = _pick(d, 256, 128) if length > 512 else _pick(d, 512, 256, 128)
    nb = d // bd
    r0 = row0 // length

    def pspec(g):
        return pl.BlockSpec((length, bd), lambda c, b: (r0 + b, g * nb + c))

    def wspec(g, rows_):
        return pl.BlockSpec((None, rows_, bd), lambda c, b: (layer_j, 0, g * nb + c))

    return pl.pallas_call(
        functools.partial(_hconv_kernel, length=length),
        grid=(nb, batch),
        in_specs=[
            _const_spec((2 * length, length)),
            pl.BlockSpec((2, 2 * length, bd), lambda c, b: (0, 0, c), pipeline_mode=pl.Buffered(1)),
            pl.BlockSpec((2, 8, bd), lambda c, b: (0, 0, c)),
            pspec(0), pspec(1), pspec(2),
            wspec(0, 3), wspec(1, 3), wspec(2, 3),
            wspec(0, 1), wspec(1, 1), wspec(2, 1),
            pl.BlockSpec((None, 2, bd), lambda c, b: (layer_j, 0, c)),
        ],
        out_specs=pl.BlockSpec((length, bd), lambda c, b: (b, c)),
        out_shape=jax.ShapeDtypeStruct((batch * length, d), BF16),
        scratch_shapes=[pltpu.VMEM((2 * length, bd), BF16)],
        compiler_params=_params("arbitrary", "arbitrary"),
        name="hyena_conv",
    )(fmat, kh, kn, p, p, p, conv_w, conv_w, conv_w, conv_b.reshape(conv_b.shape[0], 1, -1),
      conv_b.reshape(conv_b.shape[0], 1, -1), conv_b.reshape(conv_b.shape[0], 1, -1), hbias)


def _up_kernel(te_ref, x_ref, wg_ref, wu_ref, o_ref):
    del te_ref
    x = x_ref[...]
    g = jnp.dot(x, wg_ref[...].astype(BF16), preferred_element_type=F32)
    u = jnp.dot(x, wu_ref[...].astype(BF16), preferred_element_type=F32)
    o_ref[...] = (g * jax.nn.sigmoid(g) * u).astype(o_ref.dtype)


def _swiglu_up(x, wg4, wu4, layer, tile_expert, tm):
    rows_, d = x.shape
    f = wg4.shape[-1]
    bn = _pick(f, 512, 256, 128)
    wspec = pl.BlockSpec((None, None, d, bn), lambda j, i, te: (layer, te[i], 0, j))
    return pl.pallas_call(
        _up_kernel,
        grid_spec=pltpu.PrefetchScalarGridSpec(
            num_scalar_prefetch=1,
            grid=(f // bn, rows_ // tm),
            in_specs=[pl.BlockSpec((tm, d), lambda j, i, te: (i, 0)), wspec, wspec],
            out_specs=pl.BlockSpec((tm, bn), lambda j, i, te: (i, j)),
        ),
        out_shape=jax.ShapeDtypeStruct((rows_, f), BF16),
        compiler_params=_params("arbitrary", "arbitrary"),
        name="swiglu_up",
    )(tile_expert, x, wg4, wu4)


def _down_kernel(te_ref, a_ref, w_ref, s_ref, o_ref):
    del te_ref
    acc = jnp.dot(a_ref[...], w_ref[...].astype(BF16), preferred_element_type=F32)
    o_ref[...] = (acc * s_ref[...]).astype(o_ref.dtype)


def _expert_down(a, wd4, layer, tile_expert, row_scale, tm):
    rows_, f = a.shape
    d = wd4.shape[-1]
    bn = _pick(d, 512, 256, 128)
    return pl.pallas_call(
        _down_kernel,
        grid_spec=pltpu.PrefetchScalarGridSpec(
            num_scalar_prefetch=1,
            grid=(d // bn, rows_ // tm),
            in_specs=[
                pl.BlockSpec((tm, f), lambda j, i, te: (i, 0)),
                pl.BlockSpec((None, None, f, bn), lambda j, i, te: (layer, te[i], 0, j)),
                pl.BlockSpec((tm, 1), lambda j, i, te: (i, 0)),
            ],
            out_specs=pl.BlockSpec((tm, bn), lambda j, i, te: (i, j)),
        ),
        out_shape=jax.ShapeDtypeStruct((rows_, d), F32),
        compiler_params=_params("arbitrary", "arbitrary"),
        name="expert_down",
    )(tile_expert, a, wd4, row_scale)


def _router_kernel(h_ref, r_ref, idx_ref, w_ref):
    logits = lax.dot_general(r_ref[...], h_ref[...], (((1,), (1,)), ((), ())), precision=lax.Precision.HIGHEST,
                             preferred_element_type=F32)
    ne = logits.shape[0]
    eid = lax.broadcasted_iota(jnp.int32, logits.shape, 0)
    m1 = jnp.max(logits, axis=0, keepdims=True)
    i1 = jnp.min(jnp.where(logits == m1, eid, ne), axis=0, keepdims=True)
    rest = jnp.where(eid == i1, -jnp.inf, logits)
    m2 = jnp.max(rest, axis=0, keepdims=True)
    i2 = jnp.min(jnp.where(rest == m2, eid, ne), axis=0, keepdims=True)
    e2 = jnp.exp(m2 - m1)
    den = 1.0 + e2
    rid = lax.broadcasted_iota(jnp.int32, idx_ref.shape, 0)
    idx_ref[...] = jnp.where(rid == 0, i1, i2)
    w_ref[...] = jnp.where(rid == 0, 1.0 / den, e2 / den)


def _router(h, router_t, layer_j):
    m, d = h.shape
    ne = router_t.shape[1]
    tm = _pick(m, 512, 256, 128)
    return pl.pallas_call(
        _router_kernel,
        grid=(m // tm,),
        in_specs=[pl.BlockSpec((tm, d), lambda i: (i, 0)), pl.BlockSpec((None, ne, d), lambda i: (layer_j, 0, 0))],
        out_specs=[pl.BlockSpec((8, tm), lambda i: (0, i)), pl.BlockSpec((8, tm), lambda i: (0, i))],
        out_shape=[jax.ShapeDtypeStruct((8, m), jnp.int32), jax.ShapeDtypeStruct((8, m), F32)],
        compiler_params=_params("arbitrary"),
        name="router",
    )(h, router_t)


def _gather_kernel(src_ref, h_hbm, o_ref, buf, sem, *, tm):
    def row_copy(r, t):
        return pltpu.make_async_copy(h_hbm.at[pl.ds(t, 1), :], buf.at[pl.ds(r, 1), :], sem)

    def start(r, carry):
        row_copy(r, src_ref[0, 0, r]).start()
        return carry

    def wait(r, carry):
        row_copy(r, 0).wait()
        return carry

    lax.fori_loop(0, tm, start, 0)
    lax.fori_loop(0, tm, wait, 0)
    o_ref[...] = buf[...].astype(o_ref.dtype)


def _gather_rows(h, src, tm):
    _, d = h.shape
    p = src.shape[0]
    return pl.pallas_call(
        functools.partial(_gather_kernel, tm=tm),
        grid=(p // tm,),
        in_specs=[
            pl.BlockSpec((1, 1, tm), lambda i: (i, 0, 0), memory_space=pltpu.SMEM),
            pl.BlockSpec(memory_space=pl.ANY),
        ],
        out_specs=pl.BlockSpec((tm, d), lambda i: (i, 0)),
        out_shape=jax.ShapeDtypeStruct((p, d), BF16),
        scratch_shapes=[pltpu.VMEM((tm, d), F32), pltpu.SemaphoreType.DMA(())],
        compiler_params=_params("arbitrary"),
        name="gather_rows",
    )(src.reshape(p // tm, 1, tm), h)


def _tail(get_out, nrows, x_ref, nw_ref, mod_ref, gate_row, nxt, xo_ref, h_ref):
    nw = nw_ref[...]
    gate = mod_ref[gate_row:gate_row + 1, :]
    if nxt is not None:
        ng_ref, nmod_ref, sc_row, sh_row = nxt
        ng = ng_ref[...]
        sc1 = 1.0 + nmod_ref[sc_row:sc_row + 1, :]
        sh = nmod_ref[sh_row:sh_row + 1, :]

    def chunk(r, carry):
        rs = pl.ds(pl.multiple_of(r * TAIL_ROWS, TAIL_ROWS), TAIL_ROWS)
        xn = x_ref[rs, :] + gate * (_rms(get_out(rs)) * nw)
        xo_ref[rs, :] = xn
        if nxt is not None:
            h_ref[rs, :] = (_rms(xn) * ng * sc1 + sh).astype(h_ref.dtype)
        return carry

    lax.fori_loop(0, nrows // TAIL_ROWS, chunk, 0)


def _mm_tail_kernel(*refs, tp, nk, two_lhs, gate_row, rows_next):
    refs = list(refs)
    ap_ref = refs.pop(0)
    as_ref = refs.pop(0) if two_lhs else None
    w_ref, x_ref, nw_ref, mod_ref = refs[:4]
    refs = refs[4:]
    nxt = None
    if rows_next is not None:
        nxt = (refs[0], refs[1], rows_next[0], rows_next[1])
        refs = refs[2:]
    xo_ref = refs.pop(0)
    h_ref = refs.pop(0) if rows_next is not None else None
    acc_ref = refs.pop(0)
    i, kk = pl.program_id(0), pl.program_id(1)
    w = w_ref[...].astype(BF16)

    def accumulate(a_ref):
        part = jnp.dot(a_ref[...], w, preferred_element_type=F32)

        @pl.when(kk == 0)
        def _():
            acc_ref[...] = part

        @pl.when(kk > 0)
        def _():
            acc_ref[...] += part

    if two_lhs:
        @pl.when(i < tp)
        def _():
            accumulate(ap_ref)

        @pl.when(i >= tp)
        def _():
            accumulate(as_ref)
    else:
        accumulate(ap_ref)

    @pl.when(kk == nk - 1)
    def _():
        _tail(lambda rs: acc_ref[rs, :], acc_ref.shape[0], x_ref, nw_ref, mod_ref, gate_row, nxt, xo_ref, h_ref)


def _mm_tail(a, w, w_layer, x, rows, mod3, layer, gate_row, ng3, nw_row, nxt, h_dtype=BF16):
    two = isinstance(a, tuple)
    k = (a[0] if two else a).shape[1]
    d = w.shape[-1]
    bm = rows.tile(512, 256, 128)
    bk = _pick(k, 512, 256, 128)
    tp, nk = rows.n_p // bm, k // bk
    mt = rows.m // bm
    if two:
        lhs_specs = [pl.BlockSpec((bm, bk), lambda i, kk: (jnp.minimum(i, tp - 1), kk)),
                     pl.BlockSpec((bm, bk), lambda i, kk: (jnp.maximum(i - tp, 0), kk))]
        lhs = list(a)
    else:
        lhs_specs = [pl.BlockSpec((bm, bk), lambda i, kk: (i, kk))]
        lhs = [a]

    def modspec(lyr):
        return pl.BlockSpec((None, 6, d), lambda i, kk: (lyr * rows.nseg + rows.seg(i, bm), 0, 0))

    in_specs = lhs_specs + [
        pl.BlockSpec((None, bk, d), lambda i, kk: (w_layer, kk, 0)),
        pl.BlockSpec((bm, d), lambda i, kk: (i, 0)),
        pl.BlockSpec((None, 1, d), lambda i, kk: (nw_row, 0, 0)),
        modspec(layer),
    ]
    args = lhs + [w, x, ng3, mod3]
    out_specs = [pl.BlockSpec((bm, d), lambda i, kk: (i, 0))]
    out_shape = [jax.ShapeDtypeStruct((rows.m, d), F32)]
    rows_next = None
    if nxt is not None:
        n_row, n_layer, sc_row, sh_row = nxt
        in_specs += [pl.BlockSpec((None, 1, d), lambda i, kk: (n_row, 0, 0)), modspec(n_layer)]
        args += [ng3, mod3]
        out_specs.append(pl.BlockSpec((bm, d), lambda i, kk: (i, 0)))
        out_shape.append(jax.ShapeDtypeStruct((rows.m, d), h_dtype))
        rows_next = (sc_row, sh_row)
    res = pl.pallas_call(
        functools.partial(_mm_tail_kernel, tp=tp, nk=nk, two_lhs=two, gate_row=gate_row, rows_next=rows_next),
        grid=(mt, nk),
        in_specs=in_specs,
        out_specs=out_specs,
        out_shape=out_shape,
        scratch_shapes=[pltpu.VMEM((bm, d), F32)],
        compiler_params=_params("arbitrary", "arbitrary"),
        name="matmul_tail",
    )(*args)
    return (res[0], res[1]) if nxt is not None else (res[0], None)


def _combine_kernel(*refs, tm, gate_row, rows_next):
    refs = list(refs)
    pos_ref, y_hbm, x_ref, nw_ref, mod_ref = refs[:5]
    refs = refs[5:]
    nxt = None
    if rows_next is not None:
        nxt = (refs[0], refs[1], rows_next[0], rows_next[1])
        refs = refs[2:]
    xo_ref = refs.pop(0)
    h_ref = refs.pop(0) if rows_next is not None else None
    buf, sem = refs

    def row_copy(r, slot, t):
        return pltpu.make_async_copy(y_hbm.at[pl.ds(t, 1), :], buf.at[slot, pl.ds(r, 1), :], sem)

    def start(r, carry):
        for slot in range(TOP_K):
            row_copy(r, slot, pos_ref[0, slot, r]).start()
        return carry

    def wait(r, carry):
        for slot in range(TOP_K):
            row_copy(r, slot, 0).wait()
        return carry

    lax.fori_loop(0, tm, start, 0)
    lax.fori_loop(0, tm, wait, 0)
    def summed(rs):
        out = buf[0, rs, :]
        for slot in range(1, TOP_K):
            out = out + buf[slot, rs, :]
        return out

    _tail(summed, tm, x_ref, nw_ref, mod_ref, gate_row, nxt, xo_ref, h_ref)


def _combine_tail(ys, pos, x, rows, mod3, layer, gate_row, ng3, nw_row, nxt, *, row0, nrows):
    d = ys.shape[1]
    tm = _pick(math.gcd(row0, nrows) if row0 else nrows, 256, 128)
    t0 = row0 // tm
    nt = nrows // tm

    def modspec(lyr):
        return pl.BlockSpec((None, 6, d), lambda i: (lyr * rows.nseg + rows.seg(t0 + i, tm), 0, 0))

    in_specs = [
        pl.BlockSpec((1, TOP_K, tm), lambda i: (t0 + i, 0, 0), memory_space=pltpu.SMEM),
        pl.BlockSpec(memory_space=pl.ANY),
        pl.BlockSpec((tm, d), lambda i: (t0 + i, 0)),
        pl.BlockSpec((None, 1, d), lambda i: (nw_row, 0, 0)),
        modspec(layer),
    ]
    pos3 = jnp.transpose(pos.reshape(TOP_K, rows.m // tm, tm), (1, 0, 2))
    args = [pos3, ys, x, ng3, mod3]
    rows_next = None
    out_specs = [pl.BlockSpec((tm, d), lambda i: (i, 0))]
    out_shape = [jax.ShapeDtypeStruct((nrows, d), F32)]
    if nxt is not None:
        n_row, n_layer, sc_row, sh_row = nxt
        in_specs += [pl.BlockSpec((None, 1, d), lambda i: (n_row, 0, 0)), modspec(n_layer)]
        args += [ng3, mod3]
        out_specs.append(pl.BlockSpec((tm, d), lambda i: (i, 0)))
        out_shape.append(jax.ShapeDtypeStruct((nrows, d), BF16))
        rows_next = (sc_row, sh_row)
    res = pl.pallas_call(
        functools.partial(_combine_kernel, tm=tm, gate_row=gate_row, rows_next=rows_next),
        grid=(nt,),
        in_specs=in_specs,
        out_specs=out_specs,
        out_shape=out_shape,
        scratch_shapes=[pltpu.VMEM((TOP_K, tm, d), F32), pltpu.SemaphoreType.DMA(())],
        compiler_params=_params("arbitrary"),
        name="combine_tail",
    )(*args)
    return (res[0], res[1]) if nxt is not None else (res[0], None)


def _routing_tables(idx, wts, n_experts, tm):
    m = idx.shape[1]
    p = TOP_K * m + n_experts * tm
    e = idx[:TOP_K].reshape(-1)
    onehot = (e[:, None] == jnp.arange(n_experts, dtype=jnp.int32)[None, :]).astype(jnp.int32)
    rank = jnp.sum((jnp.cumsum(onehot, axis=0) - 1) * onehot, axis=1)
    counts = jnp.sum(onehot, axis=0)
    padded = ((counts + tm - 1) // tm) * tm
    ends = jnp.cumsum(padded)
    starts = ends - padded
    pos = starts[e] + rank
    tok = jnp.tile(jnp.arange(m, dtype=jnp.int32), TOP_K)
    src = jnp.zeros((p,), jnp.int32).at[pos].set(tok, unique_indices=True)
    scale = jnp.zeros((p,), F32).at[pos].set(wts[:TOP_K].reshape(-1), unique_indices=True)
    tile_start = jnp.arange(p // tm, dtype=jnp.int32) * tm
    tile_expert = jnp.minimum(jnp.searchsorted(ends, tile_start, side="right"), n_experts - 1).astype(jnp.int32)
    return src, pos.reshape(TOP_K, m), scale[:, None], tile_expert


def kernel(x_prompt, x_sample, cache_k, cache_v, c, c_ctx, ada_w, ada_b, norm_g, attn_w_in, attn_lambda, attn_subln, attn_w_out, gmlp_w_in, gmlp_ln_g, gmlp_ln_b, gmlp_w_s, gmlp_b_s, gmlp_w_out, hyena_w_in, hyena_b_in, hyena_conv_w, hyena_conv_b, hyena_ffn_w1, hyena_ffn_b1, hyena_ffn_w2, hyena_ffn_b2, hyena_ffn_w3, hyena_sin_freq, hyena_bias, hyena_w_out, ffn_w_gate, ffn_w_up, ffn_w_down, moe_router, moe_w_gate, moe_w_up, moe_w_down):
    batch, seq, d = x_prompt.shape
    dec_batch, dec_seq, _ = x_sample.shape
    depth = ada_w.shape[0]
    n_attn = attn_w_in.shape[0]
    past = cache_k.shape[2]
    n_experts = moe_router.shape[-1]
    rows = _Rows(batch * seq, dec_batch * dec_seq, dec_seq)
    n_p, n_s, m = rows.n_p, rows.n_s, rows.m
    heads = d // (2 * HEAD_DIM)

    cond8 = jnp.concatenate([c_ctx[None, :], c, jnp.zeros((8 - rows.nseg, d), F32)], axis=0)
    mod = _modulation(cond8, ada_w, ada_b)[:, :rows.nseg]
    mod3 = mod.reshape(depth * rows.nseg, 6, d)
    ng3 = norm_g.reshape(depth * 4, 1, d)

    x, h = _prenorm_join(x_prompt.reshape(n_p, d), x_sample.reshape(n_s, d), ng3, mod3, rows, 0)

    rope = _rope_tables(dec_seq)
    ck4 = cache_k.reshape(dec_batch, n_attn, past, d)
    cv4 = cache_v.reshape(dec_batch, n_attn, past, d)
    new_k = new_v = None
    y_prompt = y_sample = None

    for i in range(depth):
        kind, j = i % 3, i // 3
        if kind == 0:
            lam_init = 0.8 - 0.6 * math.exp(-0.3 * i)
            q_p = _proj(h, attn_w_in, j, row0=0, nrows=n_p, col0=0, ncols=d)
            new_k = _proj(h, attn_w_in, j, row0=0, nrows=n_p, col0=d, ncols=d, out_dtype=F32,
                          cache=(new_k, n_attn, j, batch, seq))
            new_v = _proj(h, attn_w_in, j, row0=0, nrows=n_p, col0=2 * d, ncols=d, out_dtype=F32,
                          cache=(new_v, n_attn, j, batch, seq))
            qk_s = _proj(h, attn_w_in, j, row0=n_p, nrows=n_s, col0=0, ncols=2 * d, epilogue="rope", rope=rope,
                         seq=dec_seq)
            v_s = _proj(h, attn_w_in, j, row0=n_p, nrows=n_s, col0=2 * d, ncols=d)
            o_p = _attention(q_p, 0, new_k, j, 0, new_v, j, None, attn_lambda, attn_subln, j, lam_init,
                             batch=batch, seq=seq, heads_per_step=heads, bq=seq)
            o_s = _attention(qk_s, 0, qk_s.reshape(dec_batch, 1, dec_seq, 2 * d), 0, d,
                             v_s.reshape(dec_batch, 1, dec_seq, d), 0, (ck4, cv4), attn_lambda, attn_subln, j,
                             lam_init, batch=dec_batch, seq=dec_seq, heads_per_step=1,
                             bq=_pick(dec_seq, 256, 128))
            mix, w_out = (o_p, o_s), attn_w_out
        elif kind == 1:
            uv = _proj(h, gmlp_w_in, j, row0=0, nrows=m, col0=0, ncols=2 * d, epilogue="gelu")
            mix, w_out = _gmlp_gate(uv, gmlp_ln_g, gmlp_ln_b, gmlp_w_s, gmlp_b_s, j), gmlp_w_out
        else:
            p = _proj(h, hyena_w_in, j, row0=0, nrows=m, col0=0, ncols=3 * d, epilogue="bias", bias=hyena_b_in)
            outs = []
            for row0, nb, length in ((0, batch, seq), (n_p, dec_batch, dec_seq)):
                fmat = _dft_matrix(length)
                a_tab, b_tab = _hyena_filters(length, d, hyena_ffn_w1, hyena_ffn_b1, hyena_ffn_w2, hyena_ffn_b2,
                                              hyena_ffn_w3, hyena_sin_freq, j)
                kh, kn = _hyena_spectrum(fmat, a_tab, b_tab)
                outs.append(_hyena_conv(p, row0, nb, length, fmat, kh, kn, hyena_conv_w, hyena_conv_b, hyena_bias, j))
            mix, w_out = tuple(outs), hyena_w_out

        moe_layer = i % 2 == 1
        x, h2 = _mm_tail(mix, w_out, j, x, rows, mod3, i, G1, ng3, i * 4 + 1, (i * 4 + 2, i, SC2, SH2),
                         h_dtype=F32 if moe_layer else BF16)

        last = i == depth - 1
        nxt = None if last else ((i + 1) * 4, i + 1, SC1, SH1)
        jj = i // 2
        if not moe_layer:
            tm = rows.tile(1024, 512, 256, 128)
            a = _swiglu_up(h2, ffn_w_gate[:, None], ffn_w_up[:, None], jj, jnp.zeros((m // tm,), jnp.int32), tm)
            x, h = _mm_tail(a, ffn_w_down, jj, x, rows, mod3, i, G2, ng3, i * 4 + 3, nxt)
        else:
            tm = rows.tile(512, 256, 128)
            idx, wts = _router(h2, jnp.swapaxes(moe_router, 1, 2), jj)
            src, pos, scale, tile_expert = _routing_tables(idx, wts, n_experts, tm)
            xs = _gather_rows(h2, src, tm)
            a = _swiglu_up(xs, moe_w_gate, moe_w_up, jj, tile_expert, tm)
            ys = _expert_down(a, moe_w_down, jj, tile_expert, scale, tm)
            if last:
                y_prompt, _ = _combine_tail(ys, pos, x, rows, mod3, i, G2, ng3, i * 4 + 3, None, row0=0, nrows=n_p)
                y_sample, _ = _combine_tail(ys, pos, x, rows, mod3, i, G2, ng3, i * 4 + 3, None, row0=n_p, nrows=n_s)
            else:
                x, h = _combine_tail(ys, pos, x, rows, mod3, i, G2, ng3, i * 4 + 3, nxt, row0=0, nrows=m)

    if y_prompt is None:
        y_prompt, y_sample = x[:n_p], x[n_p:]
    y_prompt = y_prompt.reshape(batch, seq, d)
    y_sample = y_sample.reshape(dec_batch, dec_seq, d)
    hk = cache_k.shape[3]
    hv = cache_v.shape[3]
    return (y_prompt, y_sample, new_k.reshape(batch, n_attn, seq, hk, d // hk),
            new_v.reshape(batch, n_attn, seq, hv, d // hv))
```

```python
import functools
import math

import jax
import jax.numpy as jnp
from jax import lax
from jax.experimental import pallas as pl
from jax.experimental.pallas import tpu as pltpu

F32 = jnp.float32
BF16 = jnp.bfloat16

EPS = 1e-6
GRID_W = 64
ROPE_THETA = 10000.0
CHUNK = 128
HYENA_BANDS = 16
HYENA_FAST_DECAY = 0.3
HYENA_SLOW_DECAY = 1.5
HYENA_TARGET = 0.01
HEAD_DIM = 64
TOP_K = 2

V7X_VMEM_BYTES = 64 * 1024 * 1024
VMEM_LIMIT = V7X_VMEM_BYTES - 8 * 1024 * 1024
LANES = 128
TAIL_ROWS = 128
FREQ_ROWS = 512

SH1, SC1, G1, SH2, SC2, G2 = range(6)


def _pick(n, *cands):
    for c in cands:
        if n % c == 0:
            return c
    return n


def _params(*sem):
    return pltpu.CompilerParams(dimension_semantics=sem, vmem_limit_bytes=VMEM_LIMIT)


def _const_spec(shape):
    nd = len(shape)
    return pl.BlockSpec(shape, lambda *_: (0,) * nd, pipeline_mode=pl.Buffered(1))


def _rms(x):
    return x * lax.rsqrt(jnp.mean(x * x, axis=-1, keepdims=True) + EPS)


class _Rows:
    def __init__(self, n_p, n_s, dec_seq):
        self.n_p, self.n_s, self.dec_seq = n_p, n_s, dec_seq
        self.m = n_p + n_s
        self.nseg = 1 + n_s // dec_seq

    def seg(self, i, bm):
        r = i * bm
        return jnp.where(r < self.n_p, 0, 1 + (r - self.n_p) // self.dec_seq)

    def tile(self, *cands):
        return _pick(math.gcd(self.n_p, self.dec_seq), *cands)


def _mod_kernel(cond_ref, w_ref, b_ref, o_ref):
    s = cond_ref[...]
    s = s * jax.nn.sigmoid(s)
    o_ref[...] = jnp.dot(s.astype(BF16), w_ref[...].astype(BF16), preferred_element_type=F32) + b_ref[...]


def _modulation(cond8, ada_w, ada_b):
    depth, d, n6 = ada_w.shape
    bn = _pick(n6, 1024, 512, 256, 128)
    return pl.pallas_call(
        _mod_kernel,
        grid=(depth, n6 // bn),
        in_specs=[
            pl.BlockSpec((8, d), lambda l, j: (0, 0)),
            pl.BlockSpec((None, d, bn), lambda l, j: (l, 0, j)),
            pl.BlockSpec((None, 1, bn), lambda l, j: (l, 0, j)),
        ],
        out_specs=pl.BlockSpec((None, 8, bn), lambda l, j: (l, 0, j)),
        out_shape=jax.ShapeDtypeStruct((depth, 8, n6), F32),
        compiler_params=_params("arbitrary", "arbitrary"),
        name="modulation",
    )(cond8, ada_w, ada_b.reshape(depth, 1, n6))


def _pre_kernel(xp_ref, xs_ref, g_ref, mod_ref, x_ref, h_ref, *, tp):
    i = pl.program_id(0)

    def emit(src_ref):
        x = src_ref[...]
        x_ref[...] = x
        h = _rms(x) * g_ref[...] * (1.0 + mod_ref[SC1:SC1 + 1, :]) + mod_ref[SH1:SH1 + 1, :]
        h_ref[...] = h.astype(h_ref.dtype)

    @pl.when(i < tp)
    def _():
        emit(xp_ref)

    @pl.when(i >= tp)
    def _():
        emit(xs_ref)


def _prenorm_join(xp, xs, ng3, mod3, rows, layer):
    d = xp.shape[-1]
    bm = rows.tile(512, 256, 128)
    tp = rows.n_p // bm
    ts = rows.n_s // bm
    return pl.pallas_call(
        functools.partial(_pre_kernel, tp=tp),
        grid=(tp + ts,),
        in_specs=[
            pl.BlockSpec((bm, d), lambda i: (jnp.minimum(i, tp - 1), 0)),
            pl.BlockSpec((bm, d), lambda i: (jnp.maximum(i - tp, 0), 0)),
            pl.BlockSpec((None, 1, d), lambda i: (layer * 4, 0, 0)),
            pl.BlockSpec((None, 6, d), lambda i: (layer * rows.nseg + rows.seg(i, bm), 0, 0)),
        ],
        out_specs=[pl.BlockSpec((bm, d), lambda i: (i, 0)), pl.BlockSpec((bm, d), lambda i: (i, 0))],
        out_shape=[jax.ShapeDtypeStruct((rows.m, d), F32), jax.ShapeDtypeStruct((rows.m, d), BF16)],
        compiler_params=_params("arbitrary"),
        name="prenorm_join",
    )(xp, xs, ng3, mod3)


def _proj_kernel(*refs, epilogue, half, cache_rows, cache_slot=None):
    if epilogue == "bias":
        x_ref, w_ref, b_ref, o_ref = refs
    elif epilogue == "rope":
        x_ref, w_ref, cos_ref, sin_ref, o_ref = refs
    else:
        x_ref, w_ref, o_ref = refs
    acc = jnp.dot(x_ref[...], w_ref[...].astype(BF16), preferred_element_type=F32)
    if epilogue == "bias":
        acc = acc + b_ref[...]
    elif epilogue == "gelu":
        acc = 0.5 * acc * (1.0 + lax.erf(acc * (2.0 ** -0.5)))
    if epilogue == "rope":
        cos_t = cos_ref[...]
        sin_t = sin_ref[...]
        lane = lax.broadcasted_iota(jnp.int32, cos_t.shape, 1)
        first = (lane % (2 * half)) < half
        for cblk in range(acc.shape[1] // LANES):
            blk = acc[:, cblk * LANES:(cblk + 1) * LANES]
            partner = jnp.where(first, pltpu.roll(blk, LANES - half, 1), pltpu.roll(blk, half, 1))
            o_ref[:, cblk * LANES:(cblk + 1) * LANES] = (blk * cos_t + partner * sin_t).astype(o_ref.dtype)
    elif cache_slot is not None:
        o_ref[...] = jnp.zeros(o_ref.shape, o_ref.dtype)
        o_ref[:, cache_slot] = acc.reshape(o_ref.shape[0], *o_ref.shape[2:]).astype(o_ref.dtype)
    elif cache_rows:
        o_ref[...] = acc.reshape(o_ref.shape).astype(o_ref.dtype)
    else:
        o_ref[...] = acc.astype(o_ref.dtype)


def _proj(h, w, layer, *, row0, nrows, col0, ncols, out_dtype=BF16, epilogue="none", bias=None,
          rope=None, seq=None, cache=None):
    k = h.shape[1]
    bm = _pick(math.gcd(math.gcd(row0, nrows), seq or 0), 1024, 512, 256, 128)
    bn = _pick(math.gcd(col0, ncols) if col0 else ncols, 1024, 512, 256, 128)
    r0, c0 = row0 // bm, col0 // bn
    in_specs = [
        pl.BlockSpec((bm, k), lambda j, i: (r0 + i, 0)),
        pl.BlockSpec((None, k, bn), lambda j, i: (layer, 0, c0 + j)),
    ]
    args = [h, w]
    half = 0
    if epilogue == "bias":
        in_specs.append(pl.BlockSpec((None, 1, bn), lambda j, i: (layer, 0, c0 + j)))
        args.append(bias.reshape(bias.shape[0], 1, bias.shape[1]))
    elif epilogue == "rope":
        cos_t, sin_t, half = rope
        per_seq = seq // bm
        in_specs += [pl.BlockSpec((bm, LANES), lambda j, i: (i % per_seq, 0))] * 2
        args += [cos_t, sin_t]
    aliases = {}
    first_slot = None
    if cache is None:
        out_spec = pl.BlockSpec((bm, bn), lambda j, i: (i, j))
        out_shape = jax.ShapeDtypeStruct((nrows, ncols), out_dtype)
    else:
        arr, n_slots, slot, batch, cseq = cache
        bb = bm // cseq
        out_shape = jax.ShapeDtypeStruct((batch, n_slots, cseq, ncols), out_dtype)
        if arr is None:
            out_spec = pl.BlockSpec((bb, n_slots, cseq, bn), lambda j, i: (i, 0, 0, j))
            first_slot = slot
        else:
            out_spec = pl.BlockSpec((bb, None, cseq, bn), lambda j, i: (i, slot, 0, j))
            in_specs.append(pl.BlockSpec(memory_space=pl.ANY))
            args.append(arr)
            aliases = {len(args) - 1: 0}

    def body(*refs):
        if aliases:
            refs = refs[:len(args) - 1] + refs[len(args):]
        _proj_kernel(*refs, epilogue=epilogue, half=half, cache_rows=cache is not None, cache_slot=first_slot)

    return pl.pallas_call(
        body,
        grid=(ncols // bn, nrows // bm),
        in_specs=in_specs,
        out_specs=out_spec,
        out_shape=out_shape,
        input_output_aliases=aliases,
        compiler_params=_params("arbitrary", "arbitrary"),
        name="proj_" + epilogue,
    )(*args)


def _rope_tables(dec_seq):
    rows = dec_seq // GRID_W
    row = jnp.repeat(jnp.arange(rows), GRID_W).astype(F32)
    col = jnp.tile(jnp.arange(GRID_W), rows).astype(F32)
    half = HEAD_DIM // 4
    inv = ROPE_THETA ** (-jnp.arange(half, dtype=F32) / half)
    ar = row[:, None] * inv[None, :]
    ac = col[:, None] * inv[None, :]
    cos64 = jnp.concatenate([jnp.cos(ar), jnp.cos(ar), jnp.cos(ac), jnp.cos(ac)], axis=-1)
    sin64 = jnp.concatenate([-jnp.sin(ar), jnp.sin(ar), -jnp.sin(ac), jnp.sin(ac)], axis=-1)
    reps = LANES // HEAD_DIM
    return jnp.tile(cos64, (1, reps)), jnp.tile(sin64, (1, reps)), half


def _attn_kernel(*refs, heads, has_cache, lam_init):
    if has_cache:
        lam_ref, sub_ref, q_ref, k_ref, v_ref, kc_ref, vc_ref, o_ref = refs
    else:
        lam_ref, sub_ref, q_ref, k_ref, v_ref, o_ref = refs
    p = lam_ref[...]
    lam = (jnp.exp(jnp.sum(p[0:1] * p[1:2], axis=-1, keepdims=True))
           - jnp.exp(jnp.sum(p[2:3] * p[3:4], axis=-1, keepdims=True)) + lam_init)
    scale = HEAD_DIM ** -0.5
    assert math.log2(HEAD_DIM) % 2 == 0, "folding the score scale into bf16 queries needs a power-of-two scale"
    nt = (((1,), (1,)), ((), ()))
    for h in range(heads):
        sl = slice(h * 2 * HEAD_DIM, (h + 1) * 2 * HEAD_DIM)
        q = q_ref[:, sl] * scale
        k = k_ref[:, sl].astype(BF16)
        v = v_ref[:, sl].astype(BF16)
        if has_cache:
            kc = kc_ref[:, sl].astype(BF16)
            vc = vc_ref[:, sl].astype(BF16)
        a_own = None
        a_cache = None
        for m in range(2):
            ms = slice(m * HEAD_DIM, (m + 1) * HEAD_DIM)
            qm = q[:, ms]
            s = lax.dot_general(qm, k[:, ms], nt, preferred_element_type=F32)
            mx = jnp.max(s, axis=-1, keepdims=True)
            if has_cache:
                sc = lax.dot_general(qm, kc[:, ms], nt, preferred_element_type=F32)
                mx = jnp.maximum(mx, jnp.max(sc, axis=-1, keepdims=True))
                ec = jnp.exp(sc - mx)
            e = jnp.exp(s - mx)
            den = jnp.sum(e, axis=-1, keepdims=True)
            if has_cache:
                den = den + jnp.sum(ec, axis=-1, keepdims=True)
            r = 1.0 / den
            if m == 0:
                a_own = e * r
                if has_cache:
                    a_cache = ec * r
            else:
                r = r * lam
                a_own = a_own - e * r
                if has_cache:
                    a_cache = a_cache - ec * r
        o = jnp.dot(a_own.astype(BF16), v, preferred_element_type=F32)
        if has_cache:
            o = o + jnp.dot(a_cache.astype(BF16), vc, preferred_element_type=F32)
        o = _rms(o) * sub_ref[...] * (1.0 - lam_init)
        o_ref[:, sl] = o.astype(o_ref.dtype)


def _attention(q, q_col0, k4, k_slot, k_col0, v4, v_slot, cache4, lam, subln, layer_j, lam_init, *, batch, seq,
               heads_per_step, bq):
    d = v4.shape[-1]
    hw = heads_per_step * 2 * HEAD_DIM
    nk = k4.shape[2]
    per_seq = seq // bq
    qc0, kc0 = q_col0 // hw, k_col0 // hw
    in_specs = [
        pl.BlockSpec((None, 4, HEAD_DIM), lambda b, h, i: (layer_j, 0, 0)),
        pl.BlockSpec((None, 1, 2 * HEAD_DIM), lambda b, h, i: (layer_j, 0, 0)),
        pl.BlockSpec((bq, hw), lambda b, h, i: (b * per_seq + i, qc0 + h)),
        pl.BlockSpec((None, None, nk, hw), lambda b, h, i: (b, k_slot, 0, kc0 + h)),
        pl.BlockSpec((None, None, nk, hw), lambda b, h, i: (b, v_slot, 0, h)),
    ]
    args = [lam, subln.reshape(subln.shape[0], 1, subln.shape[1]), q, k4, v4]
    if cache4 is not None:
        ck, cv = cache4
        nc = ck.shape[2]
        in_specs += [pl.BlockSpec((None, None, nc, hw), lambda b, h, i: (b, layer_j, 0, h))] * 2
        args += [ck, cv]
    return pl.pallas_call(
        functools.partial(_attn_kernel, heads=heads_per_step, has_cache=cache4 is not None, lam_init=lam_init),
        grid=(batch, d // hw, per_seq),
        in_specs=in_specs,
        out_specs=pl.BlockSpec((bq, hw), lambda b, h, i: (b * per_seq + i, h)),
        out_shape=jax.ShapeDtypeStruct((batch * seq, d), BF16),
        compiler_params=_params("arbitrary", "arbitrary", "arbitrary"),
        name="diff_attention",
    )(*args)


def _gmlp_kernel(u_ref, v_ref, g_ref, b_ref, ws_ref, bs_ref, o_ref, *, chunks, groups, gd):
    v = v_ref[...].astype(F32)
    mu = jnp.mean(v, axis=-1, keepdims=True)
    vc = v - mu
    var = jnp.mean(vc * vc, axis=-1, keepdims=True)
    vn = (vc * lax.rsqrt(var + EPS) * g_ref[...] + b_ref[...]).astype(BF16)
    for g in range(groups):
        w = ws_ref[g].astype(BF16)
        cs = slice(g * gd, (g + 1) * gd)
        for c in range(chunks):
            rs = slice(c * CHUNK, (c + 1) * CHUNK)
            vm = jnp.dot(w, vn[rs, cs], preferred_element_type=F32) + bs_ref[:, cs]
            o_ref[rs, cs] = (u_ref[rs, cs].astype(F32) * vm).astype(o_ref.dtype)


def _gmlp_gate(uv, ln_g, ln_b, w_s, b_s, layer_j):
    m, d2 = uv.shape
    d = d2 // 2
    groups = w_s.shape[1]
    gd = d // groups
    tm = _pick(m, 512, 256, 128)
    bs_full = jnp.repeat(jnp.transpose(b_s[layer_j]), gd, axis=1)
    return pl.pallas_call(
        functools.partial(_gmlp_kernel, chunks=tm // CHUNK, groups=groups, gd=gd),
        grid=(m // tm,),
        in_specs=[
            pl.BlockSpec((tm, d), lambda i: (i, 0)),
            pl.BlockSpec((tm, d), lambda i: (i, 1)),
            pl.BlockSpec((None, 1, d), lambda i: (layer_j, 0, 0)),
            pl.BlockSpec((None, 1, d), lambda i: (layer_j, 0, 0)),
            pl.BlockSpec((None, groups, CHUNK, CHUNK), lambda i: (layer_j, 0, 0, 0)),
            pl.BlockSpec((CHUNK, d), lambda i: (0, 0)),
        ],
        out_specs=pl.BlockSpec((tm, d), lambda i: (i, 0)),
        out_shape=jax.ShapeDtypeStruct((m, d), BF16),
        compiler_params=_params("arbitrary"),
        name="gmlp_gate",
    )(uv, uv, ln_g.reshape(ln_g.shape[0], 1, d), ln_b.reshape(ln_b.shape[0], 1, d), w_s, bs_full)


def _dft_matrix(length):
    n = 2 * length
    split = _pick(length, 32, 16, 8)
    f = jnp.arange(length, dtype=jnp.int32)[:, None]

    def cos_sin(t):
        ang = ((f * t[None, :]) % n).astype(F32) * (2.0 * math.pi / n)
        return jnp.cos(ang), jnp.sin(ang)

    ca, sa = cos_sin(jnp.arange(length // split, dtype=jnp.int32) * split)
    cb, sb = cos_sin(jnp.arange(split, dtype=jnp.int32))
    cos_ft = (ca[:, :, None] * cb[:, None, :] - sa[:, :, None] * sb[:, None, :]).reshape(length, length)
    sin_ft = (sa[:, :, None] * cb[:, None, :] + ca[:, :, None] * sb[:, None, :]).reshape(length, length)
    return jnp.concatenate([cos_ft, -sin_ft], axis=0).astype(BF16)


def _filter_features(length):
    pos = jnp.arange(length, dtype=F32)
    t = jnp.linspace(0.0, 1.0, length, dtype=F32)[:, None]
    bands = jnp.linspace(1e-4, HYENA_BANDS - 1, HYENA_BANDS, dtype=F32)
    ang = (2.0 * math.pi / length) * pos[:, None] * bands[None, :]
    z = jnp.concatenate([t, jnp.cos(ang), -jnp.sin(ang)], axis=-1)
    return jnp.pad(z, ((0, 0), (0, LANES - z.shape[1])))


def _filt_kernel(z_ref, w1_ref, b1_ref, w2_ref, b2_ref, fr_ref, w00, w01, w10, w11, dl_ref, a_ref, b_ref):
    hi = lax.Precision.HIGHEST
    z = z_ref[...]
    h = jnp.sin(fr_ref[0:1, :] * (jnp.dot(z, w1_ref[...], precision=hi, preferred_element_type=F32) + b1_ref[...]))
    h = jnp.sin(fr_ref[1:2, :] * (jnp.dot(h, w2_ref[...], precision=hi, preferred_element_type=F32) + b2_ref[...]))
    hb = h.astype(BF16)
    decay = jnp.exp(-z[:, 0:1] * dl_ref[...])
    row = lax.broadcasted_iota(jnp.int32, (z.shape[0], 1), 0)
    for o, (wf, wb) in enumerate(((w00, w01), (w10, w11))):
        fwd = jnp.dot(hb, wf[...].astype(BF16), preferred_element_type=F32) * decay
        bwd = jnp.dot(hb, wb[...].astype(BF16), preferred_element_type=F32) * decay
        bwd = jnp.where(row == 0, 0.0, bwd)
        norm = (jnp.sum(jnp.abs(fwd), axis=0, keepdims=True) + jnp.sum(jnp.abs(bwd), axis=0, keepdims=True) + EPS)
        inv = 1.0 / norm
        a_ref[o] = ((fwd + bwd) * inv).astype(a_ref.dtype)
        b_ref[o] = ((fwd - bwd) * inv).astype(b_ref.dtype)


def _hyena_filters(length, d, w1, b1, w2, b2, w3, freq, layer_j):
    fh = w2.shape[-1]
    emb = w1.shape[1]
    bd = _pick(d, 512, 256, 128)
    nb = d // bd
    zfeat = _filter_features(length)
    w1p = jnp.pad(w1[layer_j], ((0, LANES - emb), (0, 0)))
    max_decay = math.log(HYENA_TARGET) / HYENA_FAST_DECAY
    min_decay = math.log(HYENA_TARGET) / HYENA_SLOW_DECAY
    deltas = jnp.abs(jnp.linspace(min_decay, max_decay, d, dtype=F32))[None, :]
    w3_specs = [pl.BlockSpec((None, fh, bd), functools.partial(lambda c, g: (layer_j, 0, g * nb + c), g=g))
                for g in range(4)]
    out_spec = pl.BlockSpec((2, length, bd), lambda c: (0, 0, c))
    return pl.pallas_call(
        _filt_kernel,
        grid=(nb,),
        in_specs=[
            pl.BlockSpec((length, LANES), lambda c: (0, 0)),
            pl.BlockSpec((LANES, fh), lambda c: (0, 0)),
            pl.BlockSpec((None, 1, fh), lambda c: (layer_j, 0, 0)),
            pl.BlockSpec((None, fh, fh), lambda c: (layer_j, 0, 0)),
            pl.BlockSpec((None, 1, fh), lambda c: (layer_j, 0, 0)),
            pl.BlockSpec((None, 2, fh), lambda c: (layer_j, 0, 0)),
            *w3_specs,
            pl.BlockSpec((1, bd), lambda c: (0, c)),
        ],
        out_specs=[out_spec, out_spec],
        out_shape=[jax.ShapeDtypeStruct((2, length, d), BF16)] * 2,
        compiler_params=_params("arbitrary"),
        name="hyena_filters",
    )(zfeat, w1p, b1.reshape(b1.shape[0], 1, fh), w2, b2.reshape(b2.shape[0], 1, fh), freq, w3, w3, w3, w3, deltas)


def _spectrum_kernel(f_ref, a_ref, b_ref, kh_ref, kn_ref, *, length):
    n = 2 * length
    row = lax.broadcasted_iota(jnp.int32, (length, 1), 0)
    wn = jnp.where(row == 0, 1.0 / n, 2.0 / n)
    sgn = jnp.where(row % 2 == 0, 1.0, -1.0)
    a = a_ref[...]
    kh_ref[0:length, :] = jnp.dot(f_ref[0:length, :], a, preferred_element_type=F32) * wn
    kh_ref[length:n, :] = jnp.dot(f_ref[length:n, :], b_ref[...], preferred_element_type=F32) * wn
    nyq = jnp.sum(a.astype(F32) * sgn, axis=0, keepdims=True) * (1.0 / n)
    kn_ref[...] = jnp.broadcast_to(nyq, kn_ref.shape)


def _hyena_spectrum(fmat, a_tab, b_tab):
    _, length, d = a_tab.shape
    bd = _pick(d, 256, 128)
    tab_spec = pl.BlockSpec((None, length, bd), lambda o, c: (o, 0, c))
    return pl.pallas_call(
        functools.partial(_spectrum_kernel, length=length),
        grid=(2, d // bd),
        in_specs=[_const_spec((2 * length, length)), tab_spec, tab_spec],
        out_specs=[pl.BlockSpec((None, 2 * length, bd), lambda o, c: (o, 0, c)),
                   pl.BlockSpec((None, 8, bd), lambda o, c: (o, 0, c))],
        out_shape=[jax.ShapeDtypeStruct((2, 2 * length, d), F32), jax.ShapeDtypeStruct((2, 8, d), F32)],
        compiler_params=_params("arbitrary", "arbitrary"),
        name="hyena_spectrum",
    )(fmat, a_tab, b_tab)


def _hconv_kernel(f_ref, kh_ref, kn_ref, pv_ref, p1_ref, p2_ref, cwv, cw1, cw2, cbv, cb1, cb2, hb_ref, o_ref, y_scr,
                  *, length):
    n = 2 * length
    row = lax.broadcasted_iota(jnp.int32, (length, 1), 0)
    sgn = jnp.where(row % 2 == 0, 1.0, -1.0)

    def short_conv(p_ref, cw, cb):
        p = p_ref[...].astype(F32)
        prev = jnp.where(row == 0, 0.0, pltpu.roll(p, 1, 0))
        nxt = jnp.where(row == length - 1, 0.0, pltpu.roll(p, length - 1, 0))
        return cb[...] + (prev * cw[0:1, :] + p * cw[1:2, :] + nxt * cw[2:3, :])

    fb = min(length, FREQ_ROWS)

    def long_conv(z, o):
        zb = z.astype(BF16)

        def freq_block(i, carry):
            re = pl.ds(pl.multiple_of(i * fb, fb), fb)
            im = pl.ds(pl.multiple_of(length + i * fb, fb), fb)
            zr = jnp.dot(f_ref[re, :], zb, preferred_element_type=F32)
            zi = jnp.dot(f_ref[im, :], zb, preferred_element_type=F32)
            kr, ki = kh_ref[o, re, :], kh_ref[o, im, :]
            y_scr[re, :] = (zr * kr - zi * ki).astype(BF16)
            y_scr[im, :] = (zr * ki + zi * kr).astype(BF16)
            return carry

        lax.fori_loop(0, length // fb, freq_block, 0)
        nyq = jnp.sum(zb.astype(F32) * sgn, axis=0, keepdims=True) * kn_ref[o, 0:1, :]
        y = (jnp.dot(f_ref[0:length, :], y_scr[0:length, :], preferred_element_type=F32)
             + jnp.dot(f_ref[length:n, :], y_scr[length:n, :], preferred_element_type=F32) + sgn * nyq)
        return y + z * hb_ref[o:o + 1, :]

    z = short_conv(p1_ref, cw1, cb1) * long_conv(short_conv(pv_ref, cwv, cbv), 0)
    z = short_conv(p2_ref, cw2, cb2) * long_conv(z, 1)
    o_ref[...] = z.astype(o_ref.dtype)


def _hyena_conv(p, row0, batch, length, fmat, kh, kn, conv_w, conv_b, hbias, layer_j):
    d = p.shape[1] // 3
    bd = _pick(d, 256, 128) if length > 512 else _pick(d, 512, 256, 128)
    nb = d // bd
    r0 = row0 // length

    def pspec(g):
        return pl.BlockSpec((length, bd), lambda c, b: (r0 + b, g * nb + c))

    def wspec(g, rows_):
        return pl.BlockSpec((None, rows_, bd), lambda c, b: (layer_j, 0, g * nb + c))

    return pl.pallas_call(
        functools.partial(_hconv_kernel, length=length),
        grid=(nb, batch),
        in_specs=[
            _const_spec((2 * length, length)),
            pl.BlockSpec((2, 2 * length, bd), lambda c, b: (0, 0, c), pipeline_mode=pl.Buffered(1)),
            pl.BlockSpec((2, 8, bd), lambda c, b: (0, 0, c)),
            pspec(0), pspec(1), pspec(2),
            wspec(0, 3), wspec(1, 3), wspec(2, 3),
            wspec(0, 1), wspec(1, 1), wspec(2, 1),
            pl.BlockSpec((None, 2, bd), lambda c, b: (layer_j, 0, c)),
        ],
        out_specs=pl.BlockSpec((length, bd), lambda c, b: (b, c)),
        out_shape=jax.ShapeDtypeStruct((batch * length, d), BF16),
        scratch_shapes=[pltpu.VMEM((2 * length, bd), BF16)],
        compiler_params=_params("arbitrary", "arbitrary"),
        name="hyena_conv",
    )(fmat, kh, kn, p, p, p, conv_w, conv_w, conv_w, conv_b.reshape(conv_b.shape[0], 1, -1),
      conv_b.reshape(conv_b.shape[0], 1, -1), conv_b.reshape(conv_b.shape[0], 1, -1), hbias)


def _up_kernel(te_ref, x_ref, wg_ref, wu_ref, o_ref):
    del te_ref
    x = x_ref[...]
    g = jnp.dot(x, wg_ref[...].astype(BF16), preferred_element_type=F32)
    u = jnp.dot(x, wu_ref[...].astype(BF16), preferred_element_type=F32)
    o_ref[...] = (g * jax.nn.sigmoid(g) * u).astype(o_ref.dtype)


def _swiglu_up(x, wg4, wu4, layer, tile_expert, tm):
    rows_, d = x.shape
    f = wg4.shape[-1]
    bn = _pick(f, 512, 256, 128)
    wspec = pl.BlockSpec((None, None, d, bn), lambda j, i, te: (layer, te[i], 0, j))
    return pl.pallas_call(
        _up_kernel,
        grid_spec=pltpu.PrefetchScalarGridSpec(
            num_scalar_prefetch=1,
            grid=(f // bn, rows_ // tm),
            in_specs=[pl.BlockSpec((tm, d), lambda j, i, te: (i, 0)), wspec, wspec],
            out_specs=pl.BlockSpec((tm, bn), lambda j, i, te: (i, j)),
        ),
        out_shape=jax.ShapeDtypeStruct((rows_, f), BF16),
        compiler_params=_params("arbitrary", "arbitrary"),
        name="swiglu_up",
    )(tile_expert, x, wg4, wu4)


def _down_kernel(te_ref, a_ref, w_ref, o_ref):
    del te_ref
    acc = jnp.dot(a_ref[...], w_ref[...].astype(BF16), preferred_element_type=F32)
    for cc in range(o_ref.shape[1]):
        o_ref[:, cc, :] = acc[:, cc * LANES:(cc + 1) * LANES]


def _expert_down(a, wd4, layer, tile_expert, tm):
    rows_, f = a.shape
    d = wd4.shape[-1]
    bn = 1024 if d % 1024 == 0 else d
    return pl.pallas_call(
        _down_kernel,
        grid_spec=pltpu.PrefetchScalarGridSpec(
            num_scalar_prefetch=1,
            grid=(d // bn, rows_ // tm),
            in_specs=[
                pl.BlockSpec((tm, f), lambda j, i, te: (i, 0)),
                pl.BlockSpec((None, None, f, bn), lambda j, i, te: (layer, te[i], 0, j)),
            ],
            out_specs=pl.BlockSpec((tm, bn // LANES, LANES), lambda j, i, te: (i, j, 0)),
        ),
        out_shape=jax.ShapeDtypeStruct((rows_, d // LANES, LANES), F32),
        compiler_params=_params("arbitrary", "arbitrary"),
        name="expert_down",
    )(tile_expert, a, wd4)


def _router_kernel(lg_ref, idx_ref, w_ref, *, n_experts):
    lane = lax.broadcasted_iota(jnp.int32, lg_ref.shape, 1)
    logits = jnp.where(lane < n_experts, lg_ref[...], -jnp.inf)
    m1 = jnp.max(logits, axis=-1, keepdims=True)
    i1 = jnp.min(jnp.where(logits == m1, lane, LANES), axis=-1, keepdims=True)
    rest = jnp.where(lane == i1, -jnp.inf, logits)
    m2 = jnp.max(rest, axis=-1, keepdims=True)
    i2 = jnp.min(jnp.where(rest == m2, lane, LANES), axis=-1, keepdims=True)
    e2 = jnp.exp(m2 - m1)
    den = 1.0 + e2
    idx_ref[...] = jnp.where(lane == 0, i1, jnp.where(lane == 1, i2, 0))
    w_ref[...] = jnp.where(lane == 0, 1.0 / den, jnp.where(lane == 1, e2 / den, 0.0))


def _router(logits, n_experts):
    m = logits.shape[0]
    tm = _pick(m, 1024, 512, 256, 128)
    spec = pl.BlockSpec((tm, LANES), lambda i: (i, 0))
    return pl.pallas_call(
        functools.partial(_router_kernel, n_experts=n_experts),
        grid=(m // tm,),
        in_specs=[spec],
        out_specs=[spec, spec],
        out_shape=[jax.ShapeDtypeStruct((m, LANES), jnp.int32), jax.ShapeDtypeStruct((m, LANES), F32)],
        compiler_params=_params("arbitrary"),
        name="router",
    )(logits)


def _gather_kernel(src_ref, h_hbm, o_ref, buf, sem, *, tm, nc):
    def tok_copy(r, t):
        return pltpu.make_async_copy(h_hbm.at[pl.ds(pl.multiple_of(t * nc, nc), nc), :],
                                     buf.at[pl.ds(pl.multiple_of(r * nc, nc), nc), :], sem)

    def start(r, carry):
        tok_copy(r, src_ref[0, 0, r]).start()
        return carry

    def wait(r, carry):
        tok_copy(r, 0).wait()
        return carry

    lax.fori_loop(0, tm, start, 0)
    lax.fori_loop(0, tm, wait, 0)
    for cc in range(nc):
        o_ref[:, cc * LANES:(cc + 1) * LANES] = buf[pl.ds(cc, tm, stride=nc), :].astype(o_ref.dtype)


def _gather_rows(h3, src, tm, nc):
    p = src.shape[0]
    return pl.pallas_call(
        functools.partial(_gather_kernel, tm=tm, nc=nc),
        grid=(p // tm,),
        in_specs=[
            pl.BlockSpec((1, 1, tm), lambda i: (i, 0, 0), memory_space=pltpu.SMEM),
            pl.BlockSpec(memory_space=pl.ANY),
        ],
        out_specs=pl.BlockSpec((tm, nc * LANES), lambda i: (i, 0)),
        out_shape=jax.ShapeDtypeStruct((p, nc * LANES), BF16),
        scratch_shapes=[pltpu.VMEM((tm * nc, LANES), F32), pltpu.SemaphoreType.DMA(())],
        compiler_params=_params("arbitrary"),
        name="gather_rows",
    )(src.reshape(p // tm, 1, tm), h3)


class _TailParams:
    def __init__(self, nw_ref, mod_ref, gate_row, nxt):
        self.nw = nw_ref[...]
        self.gate = mod_ref[gate_row:gate_row + 1, :]
        self.has_next = nxt is not None
        if self.has_next:
            ng_ref, nmod_ref, sc_row, sh_row = nxt
            self.ng = ng_ref[...]
            self.sc1 = 1.0 + nmod_ref[sc_row:sc_row + 1, :]
            self.sh = nmod_ref[sh_row:sh_row + 1, :]

    def apply(self, out, x):
        xn = x + self.gate * (_rms(out) * self.nw)
        h = _rms(xn) * self.ng * self.sc1 + self.sh if self.has_next else None
        return xn, h


def _mm_tail_kernel(*refs, tp, two_lhs, gate_row, rows_next, moe, ch):
    refs = list(refs)
    ap_ref = refs.pop(0)
    as_ref = refs.pop(0) if two_lhs else None
    w_ref, x_ref, nw_ref, mod_ref = refs[:4]
    refs = refs[4:]
    nxt = None
    if rows_next is not None:
        nxt = (refs[0], refs[1], rows_next[0], rows_next[1])
        refs = refs[2:]
    rt_ref = refs.pop(0) if moe else None
    xo_ref = refs.pop(0)
    h_ref = refs.pop(0) if rows_next is not None else None
    lg_ref = refs.pop(0) if moe else None
    i = pl.program_id(0)
    tail = _TailParams(nw_ref, mod_ref, gate_row, nxt)
    w = w_ref[...]
    for c in range(x_ref.shape[0] // ch):
        rs = slice(c * ch, (c + 1) * ch)
        a = ap_ref[rs, :]
        if two_lhs:
            a = jnp.where(i < tp, a, as_ref[rs, :])
        xn, h = tail.apply(jnp.dot(a, w, preferred_element_type=F32), x_ref[rs, :])
        xo_ref[rs, :] = xn
        if moe:
            nc = h.shape[1] // LANES
            for cc in range(nc):
                h_ref[pl.ds(c * ch * nc + cc, ch, stride=nc), :] = h[:, cc * LANES:(cc + 1) * LANES]
            h_hi = h.astype(BF16)
            h_lo = (h - h_hi.astype(F32)).astype(BF16)
            lg_ref[rs, :] = (jnp.dot(h_hi, rt_ref[0], preferred_element_type=F32)
                             + (jnp.dot(h_lo, rt_ref[0], preferred_element_type=F32)
                                + jnp.dot(h_hi, rt_ref[1], preferred_element_type=F32)))
        elif h is not None:
            h_ref[rs, :] = h.astype(h_ref.dtype)


def _mm_tail(a, w_bf, x, rows, mod3, layer, gate_row, ng3, nw_row, nxt, router_pad=None):
    two = isinstance(a, tuple)
    k, d = w_bf.shape
    moe = router_pad is not None
    bm = rows.tile(512, 256, 128) if k <= 2048 else rows.tile(256, 128)
    ch = min(bm, 256)
    tp = rows.n_p // bm
    mt = rows.m // bm
    if two:
        lhs_specs = [pl.BlockSpec((bm, k), lambda i: (jnp.minimum(i, tp - 1), 0)),
                     pl.BlockSpec((bm, k), lambda i: (jnp.maximum(i - tp, 0), 0))]
        lhs = list(a)
    else:
        lhs_specs = [pl.BlockSpec((bm, k), lambda i: (i, 0))]
        lhs = [a]

    def modspec(lyr):
        return pl.BlockSpec((None, 6, d), lambda i: (lyr * rows.nseg + rows.seg(i, bm), 0, 0))

    in_specs = lhs_specs + [
        _const_spec((k, d)),
        pl.BlockSpec((bm, d), lambda i: (i, 0)),
        pl.BlockSpec((None, 1, d), lambda i: (nw_row, 0, 0)),
        modspec(layer),
    ]
    args = lhs + [w_bf, x, ng3, mod3]
    out_specs = [pl.BlockSpec((bm, d), lambda i: (i, 0))]
    out_shape = [jax.ShapeDtypeStruct((rows.m, d), F32)]
    rows_next = None
    if nxt is not None:
        n_row, n_layer, sc_row, sh_row = nxt
        in_specs += [pl.BlockSpec((None, 1, d), lambda i: (n_row, 0, 0)), modspec(n_layer)]
        args += [ng3, mod3]
        rows_next = (sc_row, sh_row)
        if moe:
            in_specs.append(pl.BlockSpec((2, d, LANES), lambda i: (0, 0, 0)))
            r_hi = router_pad.astype(BF16)
            args.append(jnp.stack([r_hi, (router_pad - r_hi.astype(F32)).astype(BF16)]))
            out_specs += [pl.BlockSpec((bm * d // LANES, LANES), lambda i: (i, 0)),
                          pl.BlockSpec((bm, LANES), lambda i: (i, 0))]
            out_shape += [jax.ShapeDtypeStruct((rows.m * d // LANES, LANES), F32),
                          jax.ShapeDtypeStruct((rows.m, LANES), F32)]
        else:
            out_specs.append(pl.BlockSpec((bm, d), lambda i: (i, 0)))
            out_shape.append(jax.ShapeDtypeStruct((rows.m, d), BF16))
    res = pl.pallas_call(
        functools.partial(_mm_tail_kernel, tp=tp, two_lhs=two, gate_row=gate_row, rows_next=rows_next, moe=moe,
                          ch=ch),
        grid=(mt,),
        in_specs=in_specs,
        out_specs=out_specs,
        out_shape=out_shape,
        compiler_params=_params("arbitrary"),
        name="matmul_tail",
    )(*args)
    return tuple(res) + (None,) * (3 - len(res))


def _combine_kernel(*refs, tm, nc, gate_row, rows_next):
    refs = list(refs)
    pos_ref, y_hbm, wt_ref, x_ref, nw_ref, mod_ref = refs[:6]
    refs = refs[6:]
    nxt = None
    if rows_next is not None:
        nxt = (refs[0], refs[1], rows_next[0], rows_next[1])
        refs = refs[2:]
    xo_ref = refs.pop(0)
    h_ref = refs.pop(0) if rows_next is not None else None
    buf, sem = refs

    def tok_copy(r, slot, t):
        return pltpu.make_async_copy(y_hbm.at[pl.ds(pl.multiple_of(t * nc, nc), nc), :],
                                     buf.at[slot, pl.ds(pl.multiple_of(r * nc, nc), nc), :], sem)

    def start(r, carry):
        for slot in range(TOP_K):
            tok_copy(r, slot, pos_ref[0, slot, r]).start()
        return carry

    def wait(r, carry):
        for slot in range(TOP_K):
            tok_copy(r, slot, 0).wait()
        return carry

    lax.fori_loop(0, tm, start, 0)
    lax.fori_loop(0, tm, wait, 0)
    tail = _TailParams(nw_ref, mod_ref, gate_row, nxt)

    def chunk(r, carry):
        rs = pl.ds(pl.multiple_of(r * TAIL_ROWS, TAIL_ROWS), TAIL_ROWS)
        gates = [wt_ref[rs, slot:slot + 1] for slot in range(TOP_K)]
        cols = []
        for cc in range(nc):
            slab = pl.ds(r * (TAIL_ROWS * nc) + cc, TAIL_ROWS, stride=nc)
            acc = gates[0] * buf[0, slab, :]
            for slot in range(1, TOP_K):
                acc = acc + gates[slot] * buf[slot, slab, :]
            cols.append(acc)
        xn, h = tail.apply(jnp.concatenate(cols, axis=1), x_ref[rs, :])
        xo_ref[rs, :] = xn
        if h is not None:
            h_ref[rs, :] = h.astype(h_ref.dtype)
        return carry

    lax.fori_loop(0, tm // TAIL_ROWS, chunk, 0)


def _combine_tail(ys3, pos, wts, x, rows, mod3, layer, gate_row, ng3, nw_row, nxt, *, row0, nrows):
    nc = ys3.shape[1]
    d = nc * LANES
    ys3 = ys3.reshape(-1, LANES)
    tm = _pick(math.gcd(row0, nrows) if row0 else nrows, 256, 128)
    t0 = row0 // tm
    nt = nrows // tm

    def modspec(lyr):
        return pl.BlockSpec((None, 6, d), lambda i: (lyr * rows.nseg + rows.seg(t0 + i, tm), 0, 0))

    in_specs = [
        pl.BlockSpec((1, TOP_K, tm), lambda i: (t0 + i, 0, 0), memory_space=pltpu.SMEM),
        pl.BlockSpec(memory_space=pl.ANY),
        pl.BlockSpec((tm, LANES), lambda i: (t0 + i, 0)),
        pl.BlockSpec((tm, d), lambda i: (t0 + i, 0)),
        pl.BlockSpec((None, 1, d), lambda i: (nw_row, 0, 0)),
        modspec(layer),
    ]
    pos3 = jnp.transpose(pos.reshape(TOP_K, rows.m // tm, tm), (1, 0, 2))
    args = [pos3, ys3, wts, x, ng3, mod3]
    rows_next = None
    out_specs = [pl.BlockSpec((tm, d), lambda i: (i, 0))]
    out_shape = [jax.ShapeDtypeStruct((nrows, d), F32)]
    if nxt is not None:
        n_row, n_layer, sc_row, sh_row = nxt
        in_specs += [pl.BlockSpec((None, 1, d), lambda i: (n_row, 0, 0)), modspec(n_layer)]
        args += [ng3, mod3]
        out_specs.append(pl.BlockSpec((tm, d), lambda i: (i, 0)))
        out_shape.append(jax.ShapeDtypeStruct((nrows, d), BF16))
        rows_next = (sc_row, sh_row)
    res = pl.pallas_call(
        functools.partial(_combine_kernel, tm=tm, nc=nc, gate_row=gate_row, rows_next=rows_next),
        grid=(nt,),
        in_specs=in_specs,
        out_specs=out_specs,
        out_shape=out_shape,
        scratch_shapes=[pltpu.VMEM((TOP_K, tm * nc, LANES), F32), pltpu.SemaphoreType.DMA(())],
        compiler_params=_params("arbitrary"),
        name="combine_tail",
    )(*args)
    return (res[0], res[1]) if nxt is not None else (res[0], None)


def _routing_tables(idx, n_experts, tm):
    m = idx.shape[1]
    p = TOP_K * m + n_experts * tm
    e = idx.reshape(-1)
    onehot = (e[:, None] == jnp.arange(n_experts, dtype=jnp.int32)[None, :]).astype(jnp.int32)
    rank = jnp.sum((jnp.cumsum(onehot, axis=0) - 1) * onehot, axis=1)
    counts = jnp.sum(onehot, axis=0)
    padded = ((counts + tm - 1) // tm) * tm
    ends = jnp.cumsum(padded)
    starts = ends - padded
    pos = starts[e] + rank
    tok = jnp.tile(jnp.arange(m, dtype=jnp.int32), TOP_K)
    src = jnp.zeros((p,), jnp.int32).at[pos].set(tok, unique_indices=True)
    tile_start = jnp.arange(p // tm, dtype=jnp.int32) * tm
    tile_expert = jnp.sum((tile_start[:, None] >= ends[None, :]).astype(jnp.int32), axis=1)
    return src, pos.reshape(TOP_K, m), jnp.minimum(tile_expert, n_experts - 1)


def kernel(x_prompt, x_sample, cache_k, cache_v, c, c_ctx, ada_w, ada_b, norm_g, attn_w_in, attn_lambda, attn_subln, attn_w_out, gmlp_w_in, gmlp_ln_g, gmlp_ln_b, gmlp_w_s, gmlp_b_s, gmlp_w_out, hyena_w_in, hyena_b_in, hyena_conv_w, hyena_conv_b, hyena_ffn_w1, hyena_ffn_b1, hyena_ffn_w2, hyena_ffn_b2, hyena_ffn_w3, hyena_sin_freq, hyena_bias, hyena_w_out, ffn_w_gate, ffn_w_up, ffn_w_down, moe_router, moe_w_gate, moe_w_up, moe_w_down):
    batch, seq, d = x_prompt.shape
    dec_batch, dec_seq, _ = x_sample.shape
    depth = ada_w.shape[0]
    n_attn = attn_w_in.shape[0]
    past = cache_k.shape[2]
    n_experts = moe_router.shape[-1]
    rows = _Rows(batch * seq, dec_batch * dec_seq, dec_seq)
    n_p, n_s, m = rows.n_p, rows.n_s, rows.m
    heads = d // (2 * HEAD_DIM)

    cond8 = jnp.concatenate([c_ctx[None, :], c, jnp.zeros((8 - rows.nseg, d), F32)], axis=0)
    mod = _modulation(cond8, ada_w, ada_b)[:, :rows.nseg]
    mod3 = mod.reshape(depth * rows.nseg, 6, d)
    ng3 = norm_g.reshape(depth * 4, 1, d)

    x, h = _prenorm_join(x_prompt.reshape(n_p, d), x_sample.reshape(n_s, d), ng3, mod3, rows, 0)

    rope = _rope_tables(dec_seq)
    ck4 = cache_k.reshape(dec_batch, n_attn, past, d)
    cv4 = cache_v.reshape(dec_batch, n_attn, past, d)
    new_k = new_v = None
    y_prompt = y_sample = None

    for i in range(depth):
        kind, j = i % 3, i // 3
        if kind == 0:
            lam_init = 0.8 - 0.6 * math.exp(-0.3 * i)
            q_p = _proj(h, attn_w_in, j, row0=0, nrows=n_p, col0=0, ncols=d)
            new_k = _proj(h, attn_w_in, j, row0=0, nrows=n_p, col0=d, ncols=d, out_dtype=F32,
                          cache=(new_k, n_attn, j, batch, seq))
            new_v = _proj(h, attn_w_in, j, row0=0, nrows=n_p, col0=2 * d, ncols=d, out_dtype=F32,
                          cache=(new_v, n_attn, j, batch, seq))
            qk_s = _proj(h, attn_w_in, j, row0=n_p, nrows=n_s, col0=0, ncols=2 * d, epilogue="rope", rope=rope,
                         seq=dec_seq)
            v_s = _proj(h, attn_w_in, j, row0=n_p, nrows=n_s, col0=2 * d, ncols=d)
            o_p = _attention(q_p, 0, new_k, j, 0, new_v, j, None, attn_lambda, attn_subln, j, lam_init,
                             batch=batch, seq=seq, heads_per_step=heads, bq=seq)
            o_s = _attention(qk_s, 0, qk_s.reshape(dec_batch, 1, dec_seq, 2 * d), 0, d,
                             v_s.reshape(dec_batch, 1, dec_seq, d), 0, (ck4, cv4), attn_lambda, attn_subln, j,
                             lam_init, batch=dec_batch, seq=dec_seq, heads_per_step=1,
                             bq=_pick(dec_seq, 256, 128))
            mix, w_out = (o_p, o_s), attn_w_out
        elif kind == 1:
            uv = _proj(h, gmlp_w_in, j, row0=0, nrows=m, col0=0, ncols=2 * d, epilogue="gelu")
            mix, w_out = _gmlp_gate(uv, gmlp_ln_g, gmlp_ln_b, gmlp_w_s, gmlp_b_s, j), gmlp_w_out
        else:
            p = _proj(h, hyena_w_in, j, row0=0, nrows=m, col0=0, ncols=3 * d, epilogue="bias", bias=hyena_b_in)
            outs = []
            for row0, nb, length in ((0, batch, seq), (n_p, dec_batch, dec_seq)):
                fmat = _dft_matrix(length)
                a_tab, b_tab = _hyena_filters(length, d, hyena_ffn_w1, hyena_ffn_b1, hyena_ffn_w2, hyena_ffn_b2,
                                              hyena_ffn_w3, hyena_sin_freq, j)
                kh, kn = _hyena_spectrum(fmat, a_tab, b_tab)
                outs.append(_hyena_conv(p, row0, nb, length, fmat, kh, kn, hyena_conv_w, hyena_conv_b, hyena_bias, j))
            mix, w_out = tuple(outs), hyena_w_out

        moe_layer = i % 2 == 1
        jj = i // 2
        router_pad = jnp.pad(moe_router[jj], ((0, 0), (0, LANES - n_experts))) if moe_layer else None
        x, h2, logits = _mm_tail(mix, w_out[j].astype(BF16), x, rows, mod3, i, G1, ng3, i * 4 + 1,
                                 (i * 4 + 2, i, SC2, SH2), router_pad=router_pad)

        last = i == depth - 1
        nxt = None if last else ((i + 1) * 4, i + 1, SC1, SH1)
        if not moe_layer:
            tm = rows.tile(1024, 512, 256, 128)
            a = _swiglu_up(h2, ffn_w_gate[:, None], ffn_w_up[:, None], jj, jnp.zeros((m // tm,), jnp.int32), tm)
            x, h, _ = _mm_tail(a, ffn_w_down[jj].astype(BF16), x, rows, mod3, i, G2, ng3, i * 4 + 3, nxt)
        else:
            tm = rows.tile(512, 256, 128)
            idx, wts = _router(logits, n_experts)
            src, pos, tile_expert = _routing_tables(jnp.transpose(idx[:, :TOP_K]), n_experts, tm)
            xs = _gather_rows(h2, src, tm, d // LANES)
            a = _swiglu_up(xs, moe_w_gate, moe_w_up, jj, tile_expert, tm)
            ys = _expert_down(a, moe_w_down, jj, tile_expert, tm)
            tail_args = (ys, pos, wts, x, rows, mod3, i, G2, ng3, i * 4 + 3)
            if last:
                y_prompt, _ = _combine_tail(*tail_args, None, row0=0, nrows=n_p)
                y_sample, _ = _combine_tail(*tail_args, None, row0=n_p, nrows=n_s)
            else:
                x, h = _combine_tail(*tail_args, nxt, row0=0, nrows=m)

    if y_prompt is None:
        y_prompt, y_sample = x[:n_p], x[n_p:]
    y_prompt = y_prompt.reshape(batch, seq, d)
    y_sample = y_sample.reshape(dec_batch, dec_seq, d)
    hk = cache_k.shape[3]
    hv = cache_v.shape[3]
    return (y_prompt, y_sample, new_k.reshape(batch, n_attn, seq, hk, d // hk),
            new_v.reshape(batch, n_attn, seq, hv, d // hv))
```

```python
import functools
import math

import jax
import jax.numpy as jnp
from jax import lax
from jax.experimental import pallas as pl
from jax.experimental.pallas import tpu as pltpu

F32 = jnp.float32
BF16 = jnp.bfloat16

EPS = 1e-6
GRID_W = 64
ROPE_THETA = 10000.0
CHUNK = 128
HYENA_BANDS = 16
HYENA_FAST_DECAY = 0.3
HYENA_SLOW_DECAY = 1.5
HYENA_TARGET = 0.01
HEAD_DIM = 64
TOP_K = 2

V7X_VMEM_BYTES = 64 * 1024 * 1024
VMEM_LIMIT = V7X_VMEM_BYTES - 8 * 1024 * 1024
LANES = 128
TAIL_ROWS = 128
FREQ_ROWS = 512
DMA_UNROLL = 8

SH1, SC1, G1, SH2, SC2, G2 = range(6)


def _pick(n, *cands):
    for c in cands:
        if n % c == 0:
            return c
    return n


def _params(*sem):
    return pltpu.CompilerParams(dimension_semantics=sem, vmem_limit_bytes=VMEM_LIMIT)


def _const_spec(shape):
    nd = len(shape)
    return pl.BlockSpec(shape, lambda *_: (0,) * nd, pipeline_mode=pl.Buffered(1))


def _rms(x):
    return x * lax.rsqrt(jnp.mean(x * x, axis=-1, keepdims=True) + EPS)


class _Rows:
    def __init__(self, n_p, n_s, dec_seq):
        self.n_p, self.n_s, self.dec_seq = n_p, n_s, dec_seq
        self.m = n_p + n_s
        self.nseg = 1 + n_s // dec_seq

    def seg(self, i, bm):
        r = i * bm
        return jnp.where(r < self.n_p, 0, 1 + (r - self.n_p) // self.dec_seq)

    def tile(self, *cands):
        return _pick(math.gcd(self.n_p, self.dec_seq), *cands)


def _mod_kernel(cond_ref, w_ref, b_ref, o_ref):
    s = cond_ref[...]
    s = s * jax.nn.sigmoid(s)
    o_ref[...] = jnp.dot(s.astype(BF16), w_ref[...].astype(BF16), preferred_element_type=F32) + b_ref[...]


def _modulation(cond8, ada_w, ada_b):
    depth, d, n6 = ada_w.shape
    bn = _pick(n6, 1024, 512, 256, 128)
    return pl.pallas_call(
        _mod_kernel,
        grid=(depth, n6 // bn),
        in_specs=[
            pl.BlockSpec((8, d), lambda l, j: (0, 0)),
            pl.BlockSpec((None, d, bn), lambda l, j: (l, 0, j)),
            pl.BlockSpec((None, 1, bn), lambda l, j: (l, 0, j)),
        ],
        out_specs=pl.BlockSpec((None, 8, bn), lambda l, j: (l, 0, j)),
        out_shape=jax.ShapeDtypeStruct((depth, 8, n6), F32),
        compiler_params=_params("arbitrary", "arbitrary"),
        name="modulation",
    )(cond8, ada_w, ada_b.reshape(depth, 1, n6))


def _pre_kernel(xp_ref, xs_ref, g_ref, mod_ref, x_ref, h_ref, *, tp):
    i = pl.program_id(0)

    def emit(src_ref):
        x = src_ref[...]
        x_ref[...] = x
        h = _rms(x) * g_ref[...] * (1.0 + mod_ref[SC1:SC1 + 1, :]) + mod_ref[SH1:SH1 + 1, :]
        h_ref[...] = h.astype(h_ref.dtype)

    @pl.when(i < tp)
    def _():
        emit(xp_ref)

    @pl.when(i >= tp)
    def _():
        emit(xs_ref)


def _prenorm_join(xp, xs, ng3, mod3, rows, layer):
    d = xp.shape[-1]
    bm = rows.tile(512, 256, 128)
    tp = rows.n_p // bm
    ts = rows.n_s // bm
    return pl.pallas_call(
        functools.partial(_pre_kernel, tp=tp),
        grid=(tp + ts,),
        in_specs=[
            pl.BlockSpec((bm, d), lambda i: (jnp.minimum(i, tp - 1), 0)),
            pl.BlockSpec((bm, d), lambda i: (jnp.maximum(i - tp, 0), 0)),
            pl.BlockSpec((None, 1, d), lambda i: (layer * 4, 0, 0)),
            pl.BlockSpec((None, 6, d), lambda i: (layer * rows.nseg + rows.seg(i, bm), 0, 0)),
        ],
        out_specs=[pl.BlockSpec((bm, d), lambda i: (i, 0)), pl.BlockSpec((bm, d), lambda i: (i, 0))],
        out_shape=[jax.ShapeDtypeStruct((rows.m, d), F32), jax.ShapeDtypeStruct((rows.m, d), BF16)],
        compiler_params=_params("arbitrary"),
        name="prenorm_join",
    )(xp, xs, ng3, mod3)


def _proj_kernel(*refs, epilogue, half, cache_rows, cache_slot=None):
    if epilogue == "bias":
        x_ref, w_ref, b_ref, o_ref = refs
    elif epilogue == "rope":
        x_ref, w_ref, cos_ref, sin_ref, o_ref = refs
    else:
        x_ref, w_ref, o_ref = refs
    acc = jnp.dot(x_ref[...], w_ref[...].astype(BF16), preferred_element_type=F32)
    if epilogue == "bias":
        acc = acc + b_ref[...]
    elif epilogue == "gelu":
        acc = 0.5 * acc * (1.0 + lax.erf(acc * (2.0 ** -0.5)))
    if epilogue == "rope":
        cos_t = cos_ref[...]
        sin_t = sin_ref[...]
        lane = lax.broadcasted_iota(jnp.int32, cos_t.shape, 1)
        first = (lane % (2 * half)) < half
        for cblk in range(acc.shape[1] // LANES):
            blk = acc[:, cblk * LANES:(cblk + 1) * LANES]
            partner = jnp.where(first, pltpu.roll(blk, LANES - half, 1), pltpu.roll(blk, half, 1))
            o_ref[:, cblk * LANES:(cblk + 1) * LANES] = (blk * cos_t + partner * sin_t).astype(o_ref.dtype)
    elif cache_slot is not None:
        o_ref[...] = jnp.zeros(o_ref.shape, o_ref.dtype)
        o_ref[:, cache_slot] = acc.reshape(o_ref.shape[0], *o_ref.shape[2:]).astype(o_ref.dtype)
    elif cache_rows:
        o_ref[...] = acc.reshape(o_ref.shape).astype(o_ref.dtype)
    else:
        o_ref[...] = acc.astype(o_ref.dtype)


def _proj(h, w, layer, *, row0, nrows, col0, ncols, out_dtype=BF16, epilogue="none", bias=None,
          rope=None, seq=None, cache=None):
    k = h.shape[1]
    bm = _pick(math.gcd(math.gcd(row0, nrows), seq or 0), 1024, 512, 256, 128)
    bn = _pick(math.gcd(col0, ncols) if col0 else ncols, 1024, 512, 256, 128)
    r0, c0 = row0 // bm, col0 // bn
    in_specs = [
        pl.BlockSpec((bm, k), lambda j, i: (r0 + i, 0)),
        pl.BlockSpec((None, k, bn), lambda j, i: (layer, 0, c0 + j)),
    ]
    args = [h, w]
    half = 0
    if epilogue == "bias":
        in_specs.append(pl.BlockSpec((None, 1, bn), lambda j, i: (layer, 0, c0 + j)))
        args.append(bias.reshape(bias.shape[0], 1, bias.shape[1]))
    elif epilogue == "rope":
        cos_t, sin_t, half = rope
        per_seq = seq // bm
        in_specs += [pl.BlockSpec((bm, LANES), lambda j, i: (i % per_seq, 0))] * 2
        args += [cos_t, sin_t]
    aliases = {}
    first_slot = None
    if cache is None:
        out_spec = pl.BlockSpec((bm, bn), lambda j, i: (i, j))
        out_shape = jax.ShapeDtypeStruct((nrows, ncols), out_dtype)
    else:
        arr, n_slots, slot, batch, cseq = cache
        bb = bm // cseq
        out_shape = jax.ShapeDtypeStruct((batch, n_slots, cseq, ncols), out_dtype)
        if arr is None:
            out_spec = pl.BlockSpec((bb, n_slots, cseq, bn), lambda j, i: (i, 0, 0, j))
            first_slot = slot
        else:
            out_spec = pl.BlockSpec((bb, None, cseq, bn), lambda j, i: (i, slot, 0, j))
            in_specs.append(pl.BlockSpec(memory_space=pl.ANY))
            args.append(arr)
            aliases = {len(args) - 1: 0}

    def body(*refs):
        if aliases:
            refs = refs[:len(args) - 1] + refs[len(args):]
        _proj_kernel(*refs, epilogue=epilogue, half=half, cache_rows=cache is not None, cache_slot=first_slot)

    return pl.pallas_call(
        body,
        grid=(ncols // bn, nrows // bm),
        in_specs=in_specs,
        out_specs=out_spec,
        out_shape=out_shape,
        input_output_aliases=aliases,
        compiler_params=_params("arbitrary", "arbitrary"),
        name="proj_" + epilogue,
    )(*args)


def _rope_tables(dec_seq):
    rows = dec_seq // GRID_W
    row = jnp.repeat(jnp.arange(rows), GRID_W).astype(F32)
    col = jnp.tile(jnp.arange(GRID_W), rows).astype(F32)
    half = HEAD_DIM // 4
    inv = ROPE_THETA ** (-jnp.arange(half, dtype=F32) / half)
    ar = row[:, None] * inv[None, :]
    ac = col[:, None] * inv[None, :]
    cos64 = jnp.concatenate([jnp.cos(ar), jnp.cos(ar), jnp.cos(ac), jnp.cos(ac)], axis=-1)
    sin64 = jnp.concatenate([-jnp.sin(ar), jnp.sin(ar), -jnp.sin(ac), jnp.sin(ac)], axis=-1)
    reps = LANES // HEAD_DIM
    return jnp.tile(cos64, (1, reps)), jnp.tile(sin64, (1, reps)), half


def _attn_kernel(*refs, heads, has_cache, lam_init):
    if has_cache:
        lam_ref, sub_ref, q_ref, k_ref, v_ref, kc_ref, vc_ref, o_ref = refs
    else:
        lam_ref, sub_ref, q_ref, k_ref, v_ref, o_ref = refs
    p = lam_ref[...]
    lam = (jnp.exp(jnp.sum(p[0:1] * p[1:2], axis=-1, keepdims=True))
           - jnp.exp(jnp.sum(p[2:3] * p[3:4], axis=-1, keepdims=True)) + lam_init)
    scale = HEAD_DIM ** -0.5
    assert math.log2(HEAD_DIM) % 2 == 0, "folding the score scale into bf16 queries needs a power-of-two scale"
    nt = (((1,), (1,)), ((), ()))
    for h in range(heads):
        sl = slice(h * 2 * HEAD_DIM, (h + 1) * 2 * HEAD_DIM)
        q = q_ref[:, sl] * scale
        k = k_ref[:, sl].astype(BF16)
        v = v_ref[:, sl].astype(BF16)
        if has_cache:
            kc = kc_ref[:, sl].astype(BF16)
            vc = vc_ref[:, sl].astype(BF16)
        a_own = None
        a_cache = None
        for m in range(2):
            ms = slice(m * HEAD_DIM, (m + 1) * HEAD_DIM)
            qm = q[:, ms]
            s = lax.dot_general(qm, k[:, ms], nt, preferred_element_type=F32)
            mx = jnp.max(s, axis=-1, keepdims=True)
            if has_cache:
                sc = lax.dot_general(qm, kc[:, ms], nt, preferred_element_type=F32)
                mx = jnp.maximum(mx, jnp.max(sc, axis=-1, keepdims=True))
                ec = jnp.exp(sc - mx)
            e = jnp.exp(s - mx)
            den = jnp.sum(e, axis=-1, keepdims=True)
            if has_cache:
                den = den + jnp.sum(ec, axis=-1, keepdims=True)
            r = 1.0 / den
            if m == 0:
                a_own = e * r
                if has_cache:
                    a_cache = ec * r
            else:
                r = r * lam
                a_own = a_own - e * r
                if has_cache:
                    a_cache = a_cache - ec * r
        o = jnp.dot(a_own.astype(BF16), v, preferred_element_type=F32)
        if has_cache:
            o = o + jnp.dot(a_cache.astype(BF16), vc, preferred_element_type=F32)
        o = _rms(o) * sub_ref[...] * (1.0 - lam_init)
        o_ref[:, sl] = o.astype(o_ref.dtype)


def _attention(q, q_col0, k4, k_slot, k_col0, v4, v_slot, cache4, lam, subln, layer_j, lam_init, *, batch, seq,
               heads_per_step, bq):
    d = v4.shape[-1]
    hw = heads_per_step * 2 * HEAD_DIM
    nk = k4.shape[2]
    per_seq = seq // bq
    qc0, kc0 = q_col0 // hw, k_col0 // hw
    in_specs = [
        pl.BlockSpec((None, 4, HEAD_DIM), lambda b, h, i: (layer_j, 0, 0)),
        pl.BlockSpec((None, 1, 2 * HEAD_DIM), lambda b, h, i: (layer_j, 0, 0)),
        pl.BlockSpec((bq, hw), lambda b, h, i: (b * per_seq + i, qc0 + h)),
        pl.BlockSpec((None, None, nk, hw), lambda b, h, i: (b, k_slot, 0, kc0 + h)),
        pl.BlockSpec((None, None, nk, hw), lambda b, h, i: (b, v_slot, 0, h)),
    ]
    args = [lam, subln.reshape(subln.shape[0], 1, subln.shape[1]), q, k4, v4]
    if cache4 is not None:
        ck, cv = cache4
        nc = ck.shape[2]
        in_specs += [pl.BlockSpec((None, None, nc, hw), lambda b, h, i: (b, layer_j, 0, h))] * 2
        args += [ck, cv]
    return pl.pallas_call(
        functools.partial(_attn_kernel, heads=heads_per_step, has_cache=cache4 is not None, lam_init=lam_init),
        grid=(batch, d // hw, per_seq),
        in_specs=in_specs,
        out_specs=pl.BlockSpec((bq, hw), lambda b, h, i: (b * per_seq + i, h)),
        out_shape=jax.ShapeDtypeStruct((batch * seq, d), BF16),
        compiler_params=_params("arbitrary", "arbitrary", "arbitrary"),
        name="diff_attention",
    )(*args)


def _gmlp_kernel(u_ref, v_ref, g_ref, b_ref, ws_ref, bs_ref, o_ref, *, chunks, groups, gd):
    v = v_ref[...].astype(F32)
    mu = jnp.mean(v, axis=-1, keepdims=True)
    vc = v - mu
    var = jnp.mean(vc * vc, axis=-1, keepdims=True)
    vn = (vc * lax.rsqrt(var + EPS) * g_ref[...] + b_ref[...]).astype(BF16)
    for g in range(groups):
        w = ws_ref[g].astype(BF16)
        cs = slice(g * gd, (g + 1) * gd)
        for c in range(chunks):
            rs = slice(c * CHUNK, (c + 1) * CHUNK)
            vm = jnp.dot(w, vn[rs, cs], preferred_element_type=F32) + bs_ref[:, cs]
            o_ref[rs, cs] = (u_ref[rs, cs].astype(F32) * vm).astype(o_ref.dtype)


def _gmlp_gate(uv, ln_g, ln_b, w_s, b_s, layer_j):
    m, d2 = uv.shape
    d = d2 // 2
    groups = w_s.shape[1]
    gd = d // groups
    tm = _pick(m, 512, 256, 128)
    bs_full = jnp.repeat(jnp.transpose(b_s[layer_j]), gd, axis=1)
    return pl.pallas_call(
        functools.partial(_gmlp_kernel, chunks=tm // CHUNK, groups=groups, gd=gd),
        grid=(m // tm,),
        in_specs=[
            pl.BlockSpec((tm, d), lambda i: (i, 0)),
            pl.BlockSpec((tm, d), lambda i: (i, 1)),
            pl.BlockSpec((None, 1, d), lambda i: (layer_j, 0, 0)),
            pl.BlockSpec((None, 1, d), lambda i: (layer_j, 0, 0)),
            pl.BlockSpec((None, groups, CHUNK, CHUNK), lambda i: (layer_j, 0, 0, 0)),
            pl.BlockSpec((CHUNK, d), lambda i: (0, 0)),
        ],
        out_specs=pl.BlockSpec((tm, d), lambda i: (i, 0)),
        out_shape=jax.ShapeDtypeStruct((m, d), BF16),
        compiler_params=_params("arbitrary"),
        name="gmlp_gate",
    )(uv, uv, ln_g.reshape(ln_g.shape[0], 1, d), ln_b.reshape(ln_b.shape[0], 1, d), w_s, bs_full)


def _dft_matrix(length):
    n = 2 * length
    split = _pick(length, 32, 16, 8)
    f = jnp.arange(length, dtype=jnp.int32)[:, None]

    def cos_sin(t):
        ang = ((f * t[None, :]) % n).astype(F32) * (2.0 * math.pi / n)
        return jnp.cos(ang), jnp.sin(ang)

    ca, sa = cos_sin(jnp.arange(length // split, dtype=jnp.int32) * split)
    cb, sb = cos_sin(jnp.arange(split, dtype=jnp.int32))
    cos_ft = (ca[:, :, None] * cb[:, None, :] - sa[:, :, None] * sb[:, None, :]).reshape(length, length)
    sin_ft = (sa[:, :, None] * cb[:, None, :] + ca[:, :, None] * sb[:, None, :]).reshape(length, length)
    return jnp.concatenate([cos_ft, -sin_ft], axis=0).astype(BF16)


def _filter_features(length):
    pos = jnp.arange(length, dtype=F32)
    t = jnp.linspace(0.0, 1.0, length, dtype=F32)[:, None]
    bands = jnp.linspace(1e-4, HYENA_BANDS - 1, HYENA_BANDS, dtype=F32)
    ang = (2.0 * math.pi / length) * pos[:, None] * bands[None, :]
    z = jnp.concatenate([t, jnp.cos(ang), -jnp.sin(ang)], axis=-1)
    return jnp.pad(z, ((0, 0), (0, LANES - z.shape[1])))


def _filt_kernel(z_ref, w1_ref, b1_ref, w2_ref, b2_ref, fr_ref, w00, w01, w10, w11, dl_ref, a_ref, b_ref):
    hi = lax.Precision.HIGHEST
    z = z_ref[...]
    h = jnp.sin(fr_ref[0:1, :] * (jnp.dot(z, w1_ref[...], precision=hi, preferred_element_type=F32) + b1_ref[...]))
    h = jnp.sin(fr_ref[1:2, :] * (jnp.dot(h, w2_ref[...], precision=hi, preferred_element_type=F32) + b2_ref[...]))
    hb = h.astype(BF16)
    decay = jnp.exp(-z[:, 0:1] * dl_ref[...])
    row = lax.broadcasted_iota(jnp.int32, (z.shape[0], 1), 0)
    for o, (wf, wb) in enumerate(((w00, w01), (w10, w11))):
        fwd = jnp.dot(hb, wf[...].astype(BF16), preferred_element_type=F32) * decay
        bwd = jnp.dot(hb, wb[...].astype(BF16), preferred_element_type=F32) * decay
        bwd = jnp.where(row == 0, 0.0, bwd)
        norm = (jnp.sum(jnp.abs(fwd), axis=0, keepdims=True) + jnp.sum(jnp.abs(bwd), axis=0, keepdims=True) + EPS)
        inv = 1.0 / norm
        a_ref[o] = ((fwd + bwd) * inv).astype(a_ref.dtype)
        b_ref[o] = ((fwd - bwd) * inv).astype(b_ref.dtype)


def _hyena_filters(length, d, w1, b1, w2, b2, w3, freq, layer_j):
    fh = w2.shape[-1]
    emb = w1.shape[1]
    bd = _pick(d, 512, 256, 128)
    nb = d // bd
    zfeat = _filter_features(length)
    w1p = jnp.pad(w1[layer_j], ((0, LANES - emb), (0, 0)))
    max_decay = math.log(HYENA_TARGET) / HYENA_FAST_DECAY
    min_decay = math.log(HYENA_TARGET) / HYENA_SLOW_DECAY
    deltas = jnp.abs(jnp.linspace(min_decay, max_decay, d, dtype=F32))[None, :]
    w3_specs = [pl.BlockSpec((None, fh, bd), functools.partial(lambda c, g: (layer_j, 0, g * nb + c), g=g))
                for g in range(4)]
    out_spec = pl.BlockSpec((2, length, bd), lambda c: (0, 0, c))
    return pl.pallas_call(
        _filt_kernel,
        grid=(nb,),
        in_specs=[
            pl.BlockSpec((length, LANES), lambda c: (0, 0)),
            pl.BlockSpec((LANES, fh), lambda c: (0, 0)),
            pl.BlockSpec((None, 1, fh), lambda c: (layer_j, 0, 0)),
            pl.BlockSpec((None, fh, fh), lambda c: (layer_j, 0, 0)),
            pl.BlockSpec((None, 1, fh), lambda c: (layer_j, 0, 0)),
            pl.BlockSpec((None, 2, fh), lambda c: (layer_j, 0, 0)),
            *w3_specs,
            pl.BlockSpec((1, bd), lambda c: (0, c)),
        ],
        out_specs=[out_spec, out_spec],
        out_shape=[jax.ShapeDtypeStruct((2, length, d), BF16)] * 2,
        compiler_params=_params("arbitrary"),
        name="hyena_filters",
    )(zfeat, w1p, b1.reshape(b1.shape[0], 1, fh), w2, b2.reshape(b2.shape[0], 1, fh), freq, w3, w3, w3, w3, deltas)


def _spectrum_kernel(f_ref, a_ref, b_ref, kh_ref, kn_ref, *, length):
    n = 2 * length
    row = lax.broadcasted_iota(jnp.int32, (length, 1), 0)
    wn = jnp.where(row == 0, 1.0 / n, 2.0 / n)
    sgn = jnp.where(row % 2 == 0, 1.0, -1.0)
    a = a_ref[...]
    kh_ref[0:length, :] = jnp.dot(f_ref[0:length, :], a, preferred_element_type=F32) * wn
    kh_ref[length:n, :] = jnp.dot(f_ref[length:n, :], b_ref[...], preferred_element_type=F32) * wn
    nyq = jnp.sum(a.astype(F32) * sgn, axis=0, keepdims=True) * (1.0 / n)
    kn_ref[...] = jnp.broadcast_to(nyq, kn_ref.shape)


def _hyena_spectrum(fmat, a_tab, b_tab):
    _, length, d = a_tab.shape
    bd = _pick(d, 256, 128)
    tab_spec = pl.BlockSpec((None, length, bd), lambda o, c: (o, 0, c))
    return pl.pallas_call(
        functools.partial(_spectrum_kernel, length=length),
        grid=(2, d // bd),
        in_specs=[_const_spec((2 * length, length)), tab_spec, tab_spec],
        out_specs=[pl.BlockSpec((None, 2 * length, bd), lambda o, c: (o, 0, c)),
                   pl.BlockSpec((None, 8, bd), lambda o, c: (o, 0, c))],
        out_shape=[jax.ShapeDtypeStruct((2, 2 * length, d), F32), jax.ShapeDtypeStruct((2, 8, d), F32)],
        compiler_params=_params("arbitrary", "arbitrary"),
        name="hyena_spectrum",
    )(fmat, a_tab, b_tab)


def _hconv_kernel(f_ref, kh_ref, kn_ref, pv_ref, p1_ref, p2_ref, cwv, cw1, cw2, cbv, cb1, cb2, hb_ref, o_ref, y_scr,
                  *, length):
    n = 2 * length
    row = lax.broadcasted_iota(jnp.int32, (length, 1), 0)
    sgn = jnp.where(row % 2 == 0, 1.0, -1.0)

    def short_conv(p_ref, cw, cb):
        p = p_ref[...].astype(F32)
        prev = jnp.where(row == 0, 0.0, pltpu.roll(p, 1, 0))
        nxt = jnp.where(row == length - 1, 0.0, pltpu.roll(p, length - 1, 0))
        return cb[...] + (prev * cw[0:1, :] + p * cw[1:2, :] + nxt * cw[2:3, :])

    fb = min(length, FREQ_ROWS)

    def long_conv(z, o):
        zb = z.astype(BF16)

        def freq_block(i, carry):
            re = pl.ds(pl.multiple_of(i * fb, fb), fb)
            im = pl.ds(pl.multiple_of(length + i * fb, fb), fb)
            zr = jnp.dot(f_ref[re, :], zb, preferred_element_type=F32)
            zi = jnp.dot(f_ref[im, :], zb, preferred_element_type=F32)
            kr, ki = kh_ref[o, re, :], kh_ref[o, im, :]
            y_scr[re, :] = (zr * kr - zi * ki).astype(BF16)
            y_scr[im, :] = (zr * ki + zi * kr).astype(BF16)
            return carry

        lax.fori_loop(0, length // fb, freq_block, 0)
        nyq = jnp.sum(zb.astype(F32) * sgn, axis=0, keepdims=True) * kn_ref[o, 0:1, :]
        y = (jnp.dot(f_ref[0:length, :], y_scr[0:length, :], preferred_element_type=F32)
             + jnp.dot(f_ref[length:n, :], y_scr[length:n, :], preferred_element_type=F32) + sgn * nyq)
        return y + z * hb_ref[o:o + 1, :]

    z = short_conv(p1_ref, cw1, cb1) * long_conv(short_conv(pv_ref, cwv, cbv), 0)
    z = short_conv(p2_ref, cw2, cb2) * long_conv(z, 1)
    o_ref[...] = z.astype(o_ref.dtype)


def _hyena_conv(p, row0, batch, length, fmat, kh, kn, conv_w, conv_b, hbias, layer_j):
    d = p.shape[1] // 3
    bd = _pick(d, 256, 128) if length > 512 else _pick(d, 512, 256, 128)
    nb = d // bd
    r0 = row0 // length

    def pspec(g):
        return pl.BlockSpec((length, bd), lambda c, b: (r0 + b, g * nb + c))

    def wspec(g, rows_):
        return pl.BlockSpec((None, rows_, bd), lambda c, b: (layer_j, 0, g * nb + c))

    return pl.pallas_call(
        functools.partial(_hconv_kernel, length=length),
        grid=(nb, batch),
        in_specs=[
            _const_spec((2 * length, length)),
            pl.BlockSpec((2, 2 * length, bd), lambda c, b: (0, 0, c), pipeline_mode=pl.Buffered(1)),
            pl.BlockSpec((2, 8, bd), lambda c, b: (0, 0, c)),
            pspec(0), pspec(1), pspec(2),
            wspec(0, 3), wspec(1, 3), wspec(2, 3),
            wspec(0, 1), wspec(1, 1), wspec(2, 1),
            pl.BlockSpec((None, 2, bd), lambda c, b: (layer_j, 0, c)),
        ],
        out_specs=pl.BlockSpec((length, bd), lambda c, b: (b, c)),
        out_shape=jax.ShapeDtypeStruct((batch * length, d), BF16),
        scratch_shapes=[pltpu.VMEM((2 * length, bd), BF16)],
        compiler_params=_params("arbitrary", "arbitrary"),
        name="hyena_conv",
    )(fmat, kh, kn, p, p, p, conv_w, conv_w, conv_w, conv_b.reshape(conv_b.shape[0], 1, -1),
      conv_b.reshape(conv_b.shape[0], 1, -1), conv_b.reshape(conv_b.shape[0], 1, -1), hbias)


def _up_kernel(te_ref, x_ref, wg_ref, wu_ref, o_ref):
    del te_ref
    x = x_ref[...]
    g = jnp.dot(x, wg_ref[...].astype(BF16), preferred_element_type=F32)
    u = jnp.dot(x, wu_ref[...].astype(BF16), preferred_element_type=F32)
    o_ref[...] = (g * jax.nn.sigmoid(g) * u).astype(o_ref.dtype)


def _swiglu_up(x, wg4, wu4, layer, tile_expert, tm, bn=None):
    rows_, d = x.shape
    f = wg4.shape[-1]
    bn = bn or _pick(f, 512, 256, 128)
    wspec = pl.BlockSpec((None, None, d, bn), lambda j, i, te: (layer, te[i], 0, j))
    return pl.pallas_call(
        _up_kernel,
        grid_spec=pltpu.PrefetchScalarGridSpec(
            num_scalar_prefetch=1,
            grid=(pl.cdiv(f, bn), rows_ // tm),
            in_specs=[pl.BlockSpec((tm, d), lambda j, i, te: (i, 0)), wspec, wspec],
            out_specs=pl.BlockSpec((tm, bn), lambda j, i, te: (i, j)),
        ),
        out_shape=jax.ShapeDtypeStruct((rows_, f), BF16),
        compiler_params=_params("arbitrary", "arbitrary"),
        name="swiglu_up",
    )(tile_expert, x, wg4, wu4)


def _down_kernel(te_ref, a_ref, w_ref, o_ref):
    del te_ref
    acc = jnp.dot(a_ref[...], w_ref[...].astype(BF16), preferred_element_type=F32)
    for cc in range(o_ref.shape[1]):
        o_ref[:, cc, :] = acc[:, cc * LANES:(cc + 1) * LANES]


def _expert_down(a, wd4, layer, tile_expert, tm):
    rows_, f = a.shape
    d = wd4.shape[-1]
    bn = 1024 if d % 1024 == 0 else d
    return pl.pallas_call(
        _down_kernel,
        grid_spec=pltpu.PrefetchScalarGridSpec(
            num_scalar_prefetch=1,
            grid=(d // bn, rows_ // tm),
            in_specs=[
                pl.BlockSpec((tm, f), lambda j, i, te: (i, 0)),
                pl.BlockSpec((None, None, f, bn), lambda j, i, te: (layer, te[i], 0, j)),
            ],
            out_specs=pl.BlockSpec((tm, bn // LANES, LANES), lambda j, i, te: (i, j, 0)),
        ),
        out_shape=jax.ShapeDtypeStruct((rows_, d // LANES, LANES), F32),
        compiler_params=_params("arbitrary", "arbitrary"),
        name="expert_down",
    )(tile_expert, a, wd4)


def _router_kernel(lg_ref, idx_ref, w_ref, *, n_experts):
    lane = lax.broadcasted_iota(jnp.int32, lg_ref.shape, 1)
    logits = jnp.where(lane < n_experts, lg_ref[...], -jnp.inf)
    m1 = jnp.max(logits, axis=-1, keepdims=True)
    i1 = jnp.min(jnp.where(logits == m1, lane, LANES), axis=-1, keepdims=True)
    rest = jnp.where(lane == i1, -jnp.inf, logits)
    m2 = jnp.max(rest, axis=-1, keepdims=True)
    i2 = jnp.min(jnp.where(rest == m2, lane, LANES), axis=-1, keepdims=True)
    e2 = jnp.exp(m2 - m1)
    den = 1.0 + e2
    idx_ref[...] = jnp.where(lane == 0, i1, jnp.where(lane == 1, i2, 0))
    w_ref[...] = jnp.where(lane == 0, 1.0 / den, jnp.where(lane == 1, e2 / den, 0.0))


def _router(logits, n_experts):
    m = logits.shape[0]
    tm = _pick(m, 1024, 512, 256, 128)
    spec = pl.BlockSpec((tm, LANES), lambda i: (i, 0))
    return pl.pallas_call(
        functools.partial(_router_kernel, n_experts=n_experts),
        grid=(m // tm,),
        in_specs=[spec],
        out_specs=[spec, spec],
        out_shape=[jax.ShapeDtypeStruct((m, LANES), jnp.int32), jax.ShapeDtypeStruct((m, LANES), F32)],
        compiler_params=_params("arbitrary"),
        name="router",
    )(logits)


def _issue_unrolled(n, copy_of):
    def body(g, carry):
        for u in range(DMA_UNROLL):
            copy_of(g * DMA_UNROLL + u).start(priority=u % 2)
        return carry

    lax.fori_loop(0, n // DMA_UNROLL, body, 0)


def _wait_unrolled(n, copy_of):
    def body(g, carry):
        for u in range(DMA_UNROLL):
            copy_of(g * DMA_UNROLL + u).wait()
        return carry

    lax.fori_loop(0, n // DMA_UNROLL, body, 0)


def _gather_kernel(src_ref, nxt_ref, h_hbm, o_ref, buf, sem, *, tm, nc, nsteps):
    i = pl.program_id(0)

    def tok_copy(b, r, t):
        return pltpu.make_async_copy(h_hbm.at[pl.ds(pl.multiple_of(t * nc, nc), nc), :],
                                     buf.at[b, pl.ds(pl.multiple_of(r * nc, nc), nc), :], sem.at[b])

    @pl.when(i == 0)
    def _():
        _issue_unrolled(tm, lambda r: tok_copy(0, r, src_ref[0, 0, r]))

    for b in (0, 1):
        @pl.when(i % 2 == b)
        def _(b=b):
            @pl.when(i + 1 < nsteps)
            def _():
                _issue_unrolled(tm, lambda r: tok_copy(1 - b, r, nxt_ref[0, 0, r]))

            _wait_unrolled(tm, lambda r: tok_copy(b, r, 0))
            for cc in range(nc):
                o_ref[:, cc * LANES:(cc + 1) * LANES] = buf[b, pl.ds(cc, tm, stride=nc), :].astype(o_ref.dtype)


def _gather_rows(h3, src, tm, nc):
    p = src.shape[0]
    nsteps = p // tm
    src3 = src.reshape(nsteps, 1, tm)
    return pl.pallas_call(
        functools.partial(_gather_kernel, tm=tm, nc=nc, nsteps=nsteps),
        grid=(nsteps,),
        in_specs=[
            pl.BlockSpec((1, 1, tm), lambda i: (i, 0, 0), memory_space=pltpu.SMEM),
            pl.BlockSpec((1, 1, tm), lambda i: (jnp.minimum(i + 1, nsteps - 1), 0, 0), memory_space=pltpu.SMEM),
            pl.BlockSpec(memory_space=pl.ANY),
        ],
        out_specs=pl.BlockSpec((tm, nc * LANES), lambda i: (i, 0)),
        out_shape=jax.ShapeDtypeStruct((p, nc * LANES), BF16),
        scratch_shapes=[pltpu.VMEM((2, tm * nc, LANES), F32), pltpu.SemaphoreType.DMA((2,))],
        compiler_params=_params("arbitrary"),
        name="gather_rows",
    )(src3, src3, h3)


class _TailParams:
    def __init__(self, nw_ref, mod_ref, gate_row, nxt):
        self.nw = nw_ref[...]
        self.gate = mod_ref[gate_row:gate_row + 1, :]
        self.has_next = nxt is not None
        if self.has_next:
            ng_ref, nmod_ref, sc_row, sh_row = nxt
            self.ng = ng_ref[...]
            self.sc1 = 1.0 + nmod_ref[sc_row:sc_row + 1, :]
            self.sh = nmod_ref[sh_row:sh_row + 1, :]

    def apply(self, out, x):
        xn = x + self.gate * (_rms(out) * self.nw)
        h = _rms(xn) * self.ng * self.sc1 + self.sh if self.has_next else None
        return xn, h


def _mm_tail_kernel(*refs, tp, two_lhs, gate_row, rows_next, moe, ch):
    refs = list(refs)
    ap_ref = refs.pop(0)
    as_ref = refs.pop(0) if two_lhs else None
    w_ref, x_ref, nw_ref, mod_ref = refs[:4]
    refs = refs[4:]
    nxt = None
    if rows_next is not None:
        nxt = (refs[0], refs[1], rows_next[0], rows_next[1])
        refs = refs[2:]
    rt_ref = refs.pop(0) if moe else None
    xo_ref = refs.pop(0)
    h_ref = refs.pop(0) if rows_next is not None else None
    lg_ref = refs.pop(0) if moe else None
    i = pl.program_id(0)
    tail = _TailParams(nw_ref, mod_ref, gate_row, nxt)
    w = w_ref[...]
    for c in range(x_ref.shape[0] // ch):
        rs = slice(c * ch, (c + 1) * ch)
        a = ap_ref[rs, :]
        if two_lhs:
            a = jnp.where(i < tp, a, as_ref[rs, :])
        xn, h = tail.apply(jnp.dot(a, w, preferred_element_type=F32), x_ref[rs, :])
        xo_ref[rs, :] = xn
        if moe:
            nc = h.shape[1] // LANES
            for cc in range(nc):
                h_ref[pl.ds(c * ch * nc + cc, ch, stride=nc), :] = h[:, cc * LANES:(cc + 1) * LANES]
            h_hi = h.astype(BF16)
            h_lo = (h - h_hi.astype(F32)).astype(BF16)
            lg_ref[rs, :] = (jnp.dot(h_hi, rt_ref[0], preferred_element_type=F32)
                             + (jnp.dot(h_lo, rt_ref[0], preferred_element_type=F32)
                                + jnp.dot(h_hi, rt_ref[1], preferred_element_type=F32)))
        elif h is not None:
            h_ref[rs, :] = h.astype(h_ref.dtype)


def _mm_tail(a, w_bf, x, rows, mod3, layer, gate_row, ng3, nw_row, nxt, router_pad=None):
    two = isinstance(a, tuple)
    k, d = w_bf.shape
    moe = router_pad is not None
    bm = rows.tile(512, 256, 128) if k <= 2048 else rows.tile(256, 128)
    ch = min(bm, 256)
    tp = rows.n_p // bm
    mt = rows.m // bm
    if two:
        lhs_specs = [pl.BlockSpec((bm, k), lambda i: (jnp.minimum(i, tp - 1), 0)),
                     pl.BlockSpec((bm, k), lambda i: (jnp.maximum(i - tp, 0), 0))]
        lhs = list(a)
    else:
        lhs_specs = [pl.BlockSpec((bm, k), lambda i: (i, 0))]
        lhs = [a]

    def modspec(lyr):
        return pl.BlockSpec((None, 6, d), lambda i: (lyr * rows.nseg + rows.seg(i, bm), 0, 0))

    in_specs = lhs_specs + [
        _const_spec((k, d)),
        pl.BlockSpec((bm, d), lambda i: (i, 0)),
        pl.BlockSpec((None, 1, d), lambda i: (nw_row, 0, 0)),
        modspec(layer),
    ]
    args = lhs + [w_bf, x, ng3, mod3]
    out_specs = [pl.BlockSpec((bm, d), lambda i: (i, 0))]
    out_shape = [jax.ShapeDtypeStruct((rows.m, d), F32)]
    rows_next = None
    if nxt is not None:
        n_row, n_layer, sc_row, sh_row = nxt
        in_specs += [pl.BlockSpec((None, 1, d), lambda i: (n_row, 0, 0)), modspec(n_layer)]
        args += [ng3, mod3]
        rows_next = (sc_row, sh_row)
        if moe:
            in_specs.append(pl.BlockSpec((2, d, LANES), lambda i: (0, 0, 0)))
            r_hi = router_pad.astype(BF16)
            args.append(jnp.stack([r_hi, (router_pad - r_hi.astype(F32)).astype(BF16)]))
            out_specs += [pl.BlockSpec((bm * d // LANES, LANES), lambda i: (i, 0)),
                          pl.BlockSpec((bm, LANES), lambda i: (i, 0))]
            out_shape += [jax.ShapeDtypeStruct((rows.m * d // LANES, LANES), F32),
                          jax.ShapeDtypeStruct((rows.m, LANES), F32)]
        else:
            out_specs.append(pl.BlockSpec((bm, d), lambda i: (i, 0)))
            out_shape.append(jax.ShapeDtypeStruct((rows.m, d), BF16))
    res = pl.pallas_call(
        functools.partial(_mm_tail_kernel, tp=tp, two_lhs=two, gate_row=gate_row, rows_next=rows_next, moe=moe,
                          ch=ch),
        grid=(mt,),
        in_specs=in_specs,
        out_specs=out_specs,
        out_shape=out_shape,
        compiler_params=_params("arbitrary"),
        name="matmul_tail",
    )(*args)
    return tuple(res) + (None,) * (3 - len(res))


def _combine_kernel(*refs, tm, nc, gate_row, rows_next):
    refs = list(refs)
    pos_ref, y_hbm, wt_ref, x_ref, nw_ref, mod_ref = refs[:6]
    refs = refs[6:]
    nxt = None
    if rows_next is not None:
        nxt = (refs[0], refs[1], rows_next[0], rows_next[1])
        refs = refs[2:]
    xo_ref = refs.pop(0)
    h_ref = refs.pop(0) if rows_next is not None else None
    buf, sem = refs

    def tok_copy(r, slot, t):
        return pltpu.make_async_copy(y_hbm.at[pl.ds(pl.multiple_of(t * nc, nc), nc), :],
                                     buf.at[slot, pl.ds(pl.multiple_of(r * nc, nc), nc), :], sem)

    for slot in range(TOP_K):
        _issue_unrolled(tm, lambda r, slot=slot: tok_copy(r, slot, pos_ref[0, slot, r]))
    for slot in range(TOP_K):
        _wait_unrolled(tm, lambda r, slot=slot: tok_copy(r, slot, 0))
    tail = _TailParams(nw_ref, mod_ref, gate_row, nxt)

    def chunk(r, carry):
        rs = pl.ds(pl.multiple_of(r * TAIL_ROWS, TAIL_ROWS), TAIL_ROWS)
        gates = [wt_ref[rs, slot:slot + 1] for slot in range(TOP_K)]
        cols = []
        for cc in range(nc):
            slab = pl.ds(r * (TAIL_ROWS * nc) + cc, TAIL_ROWS, stride=nc)
            acc = gates[0] * buf[0, slab, :]
            for slot in range(1, TOP_K):
                acc = acc + gates[slot] * buf[slot, slab, :]
            cols.append(acc)
        xn, h = tail.apply(jnp.concatenate(cols, axis=1), x_ref[rs, :])
        xo_ref[rs, :] = xn
        if h is not None:
            h_ref[rs, :] = h.astype(h_ref.dtype)
        return carry

    lax.fori_loop(0, tm // TAIL_ROWS, chunk, 0)


def _combine_tail(ys3, pos, wts, x, rows, mod3, layer, gate_row, ng3, nw_row, nxt, *, row0, nrows):
    nc = ys3.shape[1]
    d = nc * LANES
    ys3 = ys3.reshape(-1, LANES)
    tm = _pick(math.gcd(row0, nrows) if row0 else nrows, 256, 128)
    t0 = row0 // tm
    nt = nrows // tm

    def modspec(lyr):
        return pl.BlockSpec((None, 6, d), lambda i: (lyr * rows.nseg + rows.seg(t0 + i, tm), 0, 0))

    in_specs = [
        pl.BlockSpec((1, TOP_K, tm), lambda i: (t0 + i, 0, 0), memory_space=pltpu.SMEM),
        pl.BlockSpec(memory_space=pl.ANY),
        pl.BlockSpec((tm, LANES), lambda i: (t0 + i, 0)),
        pl.BlockSpec((tm, d), lambda i: (t0 + i, 0)),
        pl.BlockSpec((None, 1, d), lambda i: (nw_row, 0, 0)),
        modspec(layer),
    ]
    pos3 = jnp.transpose(pos.reshape(TOP_K, rows.m // tm, tm), (1, 0, 2))
    args = [pos3, ys3, wts, x, ng3, mod3]
    rows_next = None
    out_specs = [pl.BlockSpec((tm, d), lambda i: (i, 0))]
    out_shape = [jax.ShapeDtypeStruct((nrows, d), F32)]
    if nxt is not None:
        n_row, n_layer, sc_row, sh_row = nxt
        in_specs += [pl.BlockSpec((None, 1, d), lambda i: (n_row, 0, 0)), modspec(n_layer)]
        args += [ng3, mod3]
        out_specs.append(pl.BlockSpec((tm, d), lambda i: (i, 0)))
        out_shape.append(jax.ShapeDtypeStruct((nrows, d), BF16))
        rows_next = (sc_row, sh_row)
    res = pl.pallas_call(
        functools.partial(_combine_kernel, tm=tm, nc=nc, gate_row=gate_row, rows_next=rows_next),
        grid=(nt,),
        in_specs=in_specs,
        out_specs=out_specs,
        out_shape=out_shape,
        scratch_shapes=[pltpu.VMEM((TOP_K, tm * nc, LANES), F32), pltpu.SemaphoreType.DMA(())],
        compiler_params=_params("arbitrary"),
        name="combine_tail",
    )(*args)
    return (res[0], res[1]) if nxt is not None else (res[0], None)


def _routing_tables(idx, n_experts, tm):
    m = idx.shape[1]
    p = TOP_K * m + n_experts * tm
    e = idx.reshape(-1)
    onehot = (e[:, None] == jnp.arange(n_experts, dtype=jnp.int32)[None, :]).astype(jnp.int32)
    rank = jnp.sum((jnp.cumsum(onehot, axis=0) - 1) * onehot, axis=1)
    counts = jnp.sum(onehot, axis=0)
    padded = ((counts + tm - 1) // tm) * tm
    ends = jnp.cumsum(padded)
    starts = ends - padded
    pos = starts[e] + rank
    tok = jnp.tile(jnp.arange(m, dtype=jnp.int32), TOP_K)
    src = jnp.zeros((p,), jnp.int32).at[pos].set(tok, unique_indices=True)
    tile_start = jnp.arange(p // tm, dtype=jnp.int32) * tm
    tile_expert = jnp.sum((tile_start[:, None] >= ends[None, :]).astype(jnp.int32), axis=1)
    return src, pos.reshape(TOP_K, m), jnp.minimum(tile_expert, n_experts - 1)


def kernel(x_prompt, x_sample, cache_k, cache_v, c, c_ctx, ada_w, ada_b, norm_g, attn_w_in, attn_lambda, attn_subln, attn_w_out, gmlp_w_in, gmlp_ln_g, gmlp_ln_b, gmlp_w_s, gmlp_b_s, gmlp_w_out, hyena_w_in, hyena_b_in, hyena_conv_w, hyena_conv_b, hyena_ffn_w1, hyena_ffn_b1, hyena_ffn_w2, hyena_ffn_b2, hyena_ffn_w3, hyena_sin_freq, hyena_bias, hyena_w_out, ffn_w_gate, ffn_w_up, ffn_w_down, moe_router, moe_w_gate, moe_w_up, moe_w_down):
    batch, seq, d = x_prompt.shape
    dec_batch, dec_seq, _ = x_sample.shape
    depth = ada_w.shape[0]
    n_attn = attn_w_in.shape[0]
    past = cache_k.shape[2]
    n_experts = moe_router.shape[-1]
    rows = _Rows(batch * seq, dec_batch * dec_seq, dec_seq)
    n_p, n_s, m = rows.n_p, rows.n_s, rows.m
    heads = d // (2 * HEAD_DIM)

    cond8 = jnp.concatenate([c_ctx[None, :], c, jnp.zeros((8 - rows.nseg, d), F32)], axis=0)
    mod = _modulation(cond8, ada_w, ada_b)[:, :rows.nseg]
    mod3 = mod.reshape(depth * rows.nseg, 6, d)
    ng3 = norm_g.reshape(depth * 4, 1, d)

    x, h = _prenorm_join(x_prompt.reshape(n_p, d), x_sample.reshape(n_s, d), ng3, mod3, rows, 0)

    rope = _rope_tables(dec_seq)
    ck4 = cache_k.reshape(dec_batch, n_attn, past, d)
    cv4 = cache_v.reshape(dec_batch, n_attn, past, d)
    new_k = new_v = None
    y_prompt = y_sample = None

    for i in range(depth):
        kind, j = i % 3, i // 3
        if kind == 0:
            lam_init = 0.8 - 0.6 * math.exp(-0.3 * i)
            q_p = _proj(h, attn_w_in, j, row0=0, nrows=n_p, col0=0, ncols=d)
            new_k = _proj(h, attn_w_in, j, row0=0, nrows=n_p, col0=d, ncols=d, out_dtype=F32,
                          cache=(new_k, n_attn, j, batch, seq))
            new_v = _proj(h, attn_w_in, j, row0=0, nrows=n_p, col0=2 * d, ncols=d, out_dtype=F32,
                          cache=(new_v, n_attn, j, batch, seq))
            qk_s = _proj(h, attn_w_in, j, row0=n_p, nrows=n_s, col0=0, ncols=2 * d, epilogue="rope", rope=rope,
                         seq=dec_seq)
            v_s = _proj(h, attn_w_in, j, row0=n_p, nrows=n_s, col0=2 * d, ncols=d)
            o_p = _attention(q_p, 0, new_k, j, 0, new_v, j, None, attn_lambda, attn_subln, j, lam_init,
                             batch=batch, seq=seq, heads_per_step=heads, bq=seq)
            o_s = _attention(qk_s, 0, qk_s.reshape(dec_batch, 1, dec_seq, 2 * d), 0, d,
                             v_s.reshape(dec_batch, 1, dec_seq, d), 0, (ck4, cv4), attn_lambda, attn_subln, j,
                             lam_init, batch=dec_batch, seq=dec_seq, heads_per_step=1,
                             bq=_pick(dec_seq, 256, 128))
            mix, w_out = (o_p, o_s), attn_w_out
        elif kind == 1:
            uv = _proj(h, gmlp_w_in, j, row0=0, nrows=m, col0=0, ncols=2 * d, epilogue="gelu")
            mix, w_out = _gmlp_gate(uv, gmlp_ln_g, gmlp_ln_b, gmlp_w_s, gmlp_b_s, j), gmlp_w_out
        else:
            p = _proj(h, hyena_w_in, j, row0=0, nrows=m, col0=0, ncols=3 * d, epilogue="bias", bias=hyena_b_in)
            outs = []
            for row0, nb, length in ((0, batch, seq), (n_p, dec_batch, dec_seq)):
                fmat = _dft_matrix(length)
                a_tab, b_tab = _hyena_filters(length, d, hyena_ffn_w1, hyena_ffn_b1, hyena_ffn_w2, hyena_ffn_b2,
                                              hyena_ffn_w3, hyena_sin_freq, j)
                kh, kn = _hyena_spectrum(fmat, a_tab, b_tab)
                outs.append(_hyena_conv(p, row0, nb, length, fmat, kh, kn, hyena_conv_w, hyena_conv_b, hyena_bias, j))
            mix, w_out = tuple(outs), hyena_w_out

        moe_layer = i % 2 == 1
        jj = i // 2
        router_pad = jnp.pad(moe_router[jj], ((0, 0), (0, LANES - n_experts))) if moe_layer else None
        x, h2, logits = _mm_tail(mix, w_out[j].astype(BF16), x, rows, mod3, i, G1, ng3, i * 4 + 1,
                                 (i * 4 + 2, i, SC2, SH2), router_pad=router_pad)

        last = i == depth - 1
        nxt = None if last else ((i + 1) * 4, i + 1, SC1, SH1)
        if not moe_layer:
            tm = rows.tile(1024, 512, 256, 128)
            a = _swiglu_up(h2, ffn_w_gate[:, None], ffn_w_up[:, None], jj, jnp.zeros((m // tm,), jnp.int32), tm)
            x, h, _ = _mm_tail(a, ffn_w_down[jj].astype(BF16), x, rows, mod3, i, G2, ng3, i * 4 + 3, nxt)
        else:
            tm = rows.tile(512, 256, 128)
            idx, wts = _router(logits, n_experts)
            src, pos, tile_expert = _routing_tables(jnp.transpose(idx[:, :TOP_K]), n_experts, tm)
            xs = _gather_rows(h2, src, tm, d // LANES)
            a = _swiglu_up(xs, moe_w_gate, moe_w_up, jj, tile_expert, tm, bn=min(1024, moe_w_gate.shape[-1]))
            ys = _expert_down(a, moe_w_down, jj, tile_expert, tm)
            tail_args = (ys, pos, wts, x, rows, mod3, i, G2, ng3, i * 4 + 3)
            if last:
                y_prompt, _ = _combine_tail(*tail_args, None, row0=0, nrows=n_p)
                y_sample, _ = _combine_tail(*tail_args, None, row0=n_p, nrows=n_s)
            else:
                x, h = _combine_tail(*tail_args, nxt, row0=0, nrows=m)

    if y_prompt is None:
        y_prompt, y_sample = x[:n_p], x[n_p:]
    y_prompt = y_prompt.reshape(batch, seq, d)
    y_sample = y_sample.reshape(dec_batch, dec_seq, d)
    hk = cache_k.shape[3]
    hv = cache_v.shape[3]
    return (y_prompt, y_sample, new_k.reshape(batch, n_attn, seq, hk, d // hk),
            new_v.reshape(batch, n_attn, seq, hv, d // hv))
```

```python
import functools
import math

import jax
import jax.numpy as jnp
from jax import lax
from jax.experimental import pallas as pl
from jax.experimental.pallas import tpu as pltpu

F32 = jnp.float32
BF16 = jnp.bfloat16

EPS = 1e-6
GRID_W = 64
ROPE_THETA = 10000.0
CHUNK = 128
HYENA_BANDS = 16
HYENA_FAST_DECAY = 0.3
HYENA_SLOW_DECAY = 1.5
HYENA_TARGET = 0.01
HEAD_DIM = 64
TOP_K = 2

V7X_VMEM_BYTES = 64 * 1024 * 1024
VMEM_LIMIT = V7X_VMEM_BYTES - 8 * 1024 * 1024
LANES = 128
TAIL_ROWS = 128
FREQ_ROWS = 512
DMA_UNROLL = 8

SH1, SC1, G1, SH2, SC2, G2 = range(6)


def _pick(n, *cands):
    for c in cands:
        if n % c == 0:
            return c
    return n


def _params(*sem):
    return pltpu.CompilerParams(dimension_semantics=sem, vmem_limit_bytes=VMEM_LIMIT)


def _const_spec(shape):
    nd = len(shape)
    return pl.BlockSpec(shape, lambda *_: (0,) * nd, pipeline_mode=pl.Buffered(1))


def _rms(x):
    return x * lax.rsqrt(jnp.mean(x * x, axis=-1, keepdims=True) + EPS)


class _Rows:
    def __init__(self, n_p, n_s, dec_seq):
        self.n_p, self.n_s, self.dec_seq = n_p, n_s, dec_seq
        self.m = n_p + n_s
        self.nseg = 1 + n_s // dec_seq

    def seg(self, i, bm):
        r = i * bm
        return jnp.where(r < self.n_p, 0, 1 + (r - self.n_p) // self.dec_seq)

    def tile(self, *cands):
        return _pick(math.gcd(self.n_p, self.dec_seq), *cands)


def _mod_kernel(cond_ref, w_ref, b_ref, o_ref):
    s = cond_ref[...]
    s = s * jax.nn.sigmoid(s)
    o_ref[...] = jnp.dot(s.astype(BF16), w_ref[...].astype(BF16), preferred_element_type=F32) + b_ref[...]


def _modulation(cond8, ada_w, ada_b):
    depth, d, n6 = ada_w.shape
    bn = _pick(n6, 1024, 512, 256, 128)
    return pl.pallas_call(
        _mod_kernel,
        grid=(depth, n6 // bn),
        in_specs=[
            pl.BlockSpec((8, d), lambda l, j: (0, 0)),
            pl.BlockSpec((None, d, bn), lambda l, j: (l, 0, j)),
            pl.BlockSpec((None, 1, bn), lambda l, j: (l, 0, j)),
        ],
        out_specs=pl.BlockSpec((None, 8, bn), lambda l, j: (l, 0, j)),
        out_shape=jax.ShapeDtypeStruct((depth, 8, n6), F32),
        compiler_params=_params("arbitrary", "arbitrary"),
        name="modulation",
    )(cond8, ada_w, ada_b.reshape(depth, 1, n6))


def _pre_kernel(xp_ref, xs_ref, g_ref, mod_ref, x_ref, h_ref, *, tp):
    i = pl.program_id(0)

    def emit(src_ref):
        x = src_ref[...]
        x_ref[...] = x
        h = _rms(x) * g_ref[...] * (1.0 + mod_ref[SC1:SC1 + 1, :]) + mod_ref[SH1:SH1 + 1, :]
        h_ref[...] = h.astype(h_ref.dtype)

    @pl.when(i < tp)
    def _():
        emit(xp_ref)

    @pl.when(i >= tp)
    def _():
        emit(xs_ref)


def _prenorm_join(xp, xs, ng3, mod3, rows, layer):
    d = xp.shape[-1]
    bm = rows.tile(512, 256, 128)
    tp = rows.n_p // bm
    ts = rows.n_s // bm
    return pl.pallas_call(
        functools.partial(_pre_kernel, tp=tp),
        grid=(tp + ts,),
        in_specs=[
            pl.BlockSpec((bm, d), lambda i: (jnp.minimum(i, tp - 1), 0)),
            pl.BlockSpec((bm, d), lambda i: (jnp.maximum(i - tp, 0), 0)),
            pl.BlockSpec((None, 1, d), lambda i: (layer * 4, 0, 0)),
            pl.BlockSpec((None, 6, d), lambda i: (layer * rows.nseg + rows.seg(i, bm), 0, 0)),
        ],
        out_specs=[pl.BlockSpec((bm, d), lambda i: (i, 0)), pl.BlockSpec((bm, d), lambda i: (i, 0))],
        out_shape=[jax.ShapeDtypeStruct((rows.m, d), F32), jax.ShapeDtypeStruct((rows.m, d), BF16)],
        compiler_params=_params("arbitrary"),
        name="prenorm_join",
    )(xp, xs, ng3, mod3)


def _proj_kernel(*refs, epilogue, half, cache_rows, cache_slot=None):
    if epilogue == "bias":
        x_ref, w_ref, b_ref, o_ref = refs
    elif epilogue == "rope":
        x_ref, w_ref, cos_ref, sin_ref, o_ref = refs
    else:
        x_ref, w_ref, o_ref = refs
    acc = jnp.dot(x_ref[...], w_ref[...].astype(BF16), preferred_element_type=F32)
    if epilogue == "bias":
        acc = acc + b_ref[...]
    elif epilogue == "gelu":
        acc = 0.5 * acc * (1.0 + lax.erf(acc * (2.0 ** -0.5)))
    if epilogue == "rope":
        cos_t = cos_ref[...]
        sin_t = sin_ref[...]
        lane = lax.broadcasted_iota(jnp.int32, cos_t.shape, 1)
        first = (lane % (2 * half)) < half
        for cblk in range(acc.shape[1] // LANES):
            blk = acc[:, cblk * LANES:(cblk + 1) * LANES]
            partner = jnp.where(first, pltpu.roll(blk, LANES - half, 1), pltpu.roll(blk, half, 1))
            o_ref[:, cblk * LANES:(cblk + 1) * LANES] = (blk * cos_t + partner * sin_t).astype(o_ref.dtype)
    elif cache_slot is not None:
        o_ref[...] = jnp.zeros(o_ref.shape, o_ref.dtype)
        o_ref[:, cache_slot] = acc.reshape(o_ref.shape[0], *o_ref.shape[2:]).astype(o_ref.dtype)
    elif cache_rows:
        o_ref[...] = acc.reshape(o_ref.shape).astype(o_ref.dtype)
    else:
        o_ref[...] = acc.astype(o_ref.dtype)


def _proj(h, w, layer, *, row0, nrows, col0, ncols, out_dtype=BF16, epilogue="none", bias=None,
          rope=None, seq=None, cache=None):
    k = h.shape[1]
    bm = _pick(math.gcd(math.gcd(row0, nrows), seq or 0), 1024, 512, 256, 128)
    bn = _pick(math.gcd(col0, ncols) if col0 else ncols, 1024, 512, 256, 128)
    r0, c0 = row0 // bm, col0 // bn
    in_specs = [
        pl.BlockSpec((bm, k), lambda j, i: (r0 + i, 0)),
        pl.BlockSpec((None, k, bn), lambda j, i: (layer, 0, c0 + j)),
    ]
    args = [h, w]
    half = 0
    if epilogue == "bias":
        in_specs.append(pl.BlockSpec((None, 1, bn), lambda j, i: (layer, 0, c0 + j)))
        args.append(bias.reshape(bias.shape[0], 1, bias.shape[1]))
    elif epilogue == "rope":
        cos_t, sin_t, half = rope
        per_seq = seq // bm
        in_specs += [pl.BlockSpec((bm, LANES), lambda j, i: (i % per_seq, 0))] * 2
        args += [cos_t, sin_t]
    aliases = {}
    first_slot = None
    if cache is None:
        out_spec = pl.BlockSpec((bm, bn), lambda j, i: (i, j))
        out_shape = jax.ShapeDtypeStruct((nrows, ncols), out_dtype)
    else:
        arr, n_slots, slot, batch, cseq = cache
        bb = bm // cseq
        out_shape = jax.ShapeDtypeStruct((batch, n_slots, cseq, ncols), out_dtype)
        if arr is None:
            out_spec = pl.BlockSpec((bb, n_slots, cseq, bn), lambda j, i: (i, 0, 0, j))
            first_slot = slot
        else:
            out_spec = pl.BlockSpec((bb, None, cseq, bn), lambda j, i: (i, slot, 0, j))
            in_specs.append(pl.BlockSpec(memory_space=pl.ANY))
            args.append(arr)
            aliases = {len(args) - 1: 0}

    def body(*refs):
        if aliases:
            refs = refs[:len(args) - 1] + refs[len(args):]
        _proj_kernel(*refs, epilogue=epilogue, half=half, cache_rows=cache is not None, cache_slot=first_slot)

    return pl.pallas_call(
        body,
        grid=(ncols // bn, nrows // bm),
        in_specs=in_specs,
        out_specs=out_spec,
        out_shape=out_shape,
        input_output_aliases=aliases,
        compiler_params=_params("arbitrary", "arbitrary"),
        name="proj_" + epilogue,
    )(*args)


def _rope_tables(dec_seq):
    rows = dec_seq // GRID_W
    row = jnp.repeat(jnp.arange(rows), GRID_W).astype(F32)
    col = jnp.tile(jnp.arange(GRID_W), rows).astype(F32)
    half = HEAD_DIM // 4
    inv = ROPE_THETA ** (-jnp.arange(half, dtype=F32) / half)
    ar = row[:, None] * inv[None, :]
    ac = col[:, None] * inv[None, :]
    cos64 = jnp.concatenate([jnp.cos(ar), jnp.cos(ar), jnp.cos(ac), jnp.cos(ac)], axis=-1)
    sin64 = jnp.concatenate([-jnp.sin(ar), jnp.sin(ar), -jnp.sin(ac), jnp.sin(ac)], axis=-1)
    reps = LANES // HEAD_DIM
    return jnp.tile(cos64, (1, reps)), jnp.tile(sin64, (1, reps)), half


def _attn_kernel(*refs, heads, has_cache, lam_init, lag):
    if has_cache:
        lam_ref, sub_ref, q_ref, k_ref, v_ref, kc_ref, vc_ref, o_ref = refs
    else:
        lam_ref, sub_ref, q_ref, k_ref, v_ref, o_ref = refs
    p = lam_ref[...]
    lam = (jnp.exp(jnp.sum(p[0:1] * p[1:2], axis=-1, keepdims=True))
           - jnp.exp(jnp.sum(p[2:3] * p[3:4], axis=-1, keepdims=True)) + lam_init)
    scale = HEAD_DIM ** -0.5
    assert math.log2(HEAD_DIM) % 2 == 0, "folding the score scale into bf16 queries needs a power-of-two scale"
    nt = (((1,), (1,)), ((), ()))
    head_slices = [slice(h * 2 * HEAD_DIM, (h + 1) * 2 * HEAD_DIM) for h in range(heads)]
    n_units = 2 * heads

    def score_stage(u):
        sl, ms = head_slices[u // 2], slice((u % 2) * HEAD_DIM, (u % 2 + 1) * HEAD_DIM)
        q = q_ref[:, sl][:, ms] * scale
        s = lax.dot_general(q, k_ref[:, sl][:, ms].astype(BF16), nt, preferred_element_type=F32)
        sc = None
        if has_cache:
            sc = lax.dot_general(q, kc_ref[:, sl][:, ms].astype(BF16), nt, preferred_element_type=F32)
        return s, sc

    def exp_stage(s, sc):
        mx = jnp.max(s, axis=-1, keepdims=True)
        if has_cache:
            mx = jnp.maximum(mx, jnp.max(sc, axis=-1, keepdims=True))
        return jnp.exp((s - mx).astype(BF16)), jnp.exp((sc - mx).astype(BF16)) if has_cache else None

    def value_stage(u, e, ec):
        sl = head_slices[u // 2]
        v = v_ref[:, sl].astype(BF16)
        acc = jnp.dot(e, jnp.concatenate([v, jnp.ones_like(v)], axis=1), preferred_element_type=F32)
        if has_cache:
            vc = vc_ref[:, sl].astype(BF16)
            acc = acc + jnp.dot(ec, jnp.concatenate([vc, jnp.ones_like(vc)], axis=1), preferred_element_type=F32)
        vd = acc.shape[1] // 2
        return acc[:, :vd] / acc[:, vd:]

    half = lag // 2
    scores, weights, maps = {}, {}, {}
    for t in range(n_units + lag):
        if 0 <= t - lag:
            maps[t - lag] = value_stage(t - lag, *weights.pop(t - lag))
        if 0 <= t - half < n_units:
            weights[t - half] = exp_stage(*scores.pop(t - half))
        if t < n_units:
            scores[t] = score_stage(t)
    for h, sl in enumerate(head_slices):
        o = maps[2 * h] - lam * maps[2 * h + 1]
        o = _rms(o) * sub_ref[...] * (1.0 - lam_init)
        o_ref[:, sl] = o.astype(o_ref.dtype)


def _attention(q, q_col0, k4, k_slot, k_col0, v4, v_slot, cache4, lam, subln, layer_j, lam_init, *, batch, seq,
               heads_per_step, bq, lag):
    d = v4.shape[-1]
    hw = heads_per_step * 2 * HEAD_DIM
    nk = k4.shape[2]
    per_seq = seq // bq
    qc0, kc0 = q_col0 // hw, k_col0 // hw
    in_specs = [
        pl.BlockSpec((None, 4, HEAD_DIM), lambda b, h, i: (layer_j, 0, 0)),
        pl.BlockSpec((None, 1, 2 * HEAD_DIM), lambda b, h, i: (layer_j, 0, 0)),
        pl.BlockSpec((bq, hw), lambda b, h, i: (b * per_seq + i, qc0 + h)),
        pl.BlockSpec((None, None, nk, hw), lambda b, h, i: (b, k_slot, 0, kc0 + h)),
        pl.BlockSpec((None, None, nk, hw), lambda b, h, i: (b, v_slot, 0, h)),
    ]
    args = [lam, subln.reshape(subln.shape[0], 1, subln.shape[1]), q, k4, v4]
    if cache4 is not None:
        ck, cv = cache4
        nc = ck.shape[2]
        in_specs += [pl.BlockSpec((None, None, nc, hw), lambda b, h, i: (b, layer_j, 0, h))] * 2
        args += [ck, cv]
    return pl.pallas_call(
        functools.partial(_attn_kernel, heads=heads_per_step, has_cache=cache4 is not None, lam_init=lam_init,
                          lag=lag),
        grid=(batch, d // hw, per_seq),
        in_specs=in_specs,
        out_specs=pl.BlockSpec((bq, hw), lambda b, h, i: (b * per_seq + i, h)),
        out_shape=jax.ShapeDtypeStruct((batch * seq, d), BF16),
        compiler_params=_params("arbitrary", "arbitrary", "arbitrary"),
        name="diff_attention",
    )(*args)


def _gmlp_kernel(u_ref, v_ref, g_ref, b_ref, ws_ref, bs_ref, o_ref, *, chunks, groups, gd):
    v = v_ref[...].astype(F32)
    mu = jnp.mean(v, axis=-1, keepdims=True)
    vc = v - mu
    var = jnp.mean(vc * vc, axis=-1, keepdims=True)
    vn = (vc * lax.rsqrt(var + EPS) * g_ref[...] + b_ref[...]).astype(BF16)
    for g in range(groups):
        w = ws_ref[g].astype(BF16)
        cs = slice(g * gd, (g + 1) * gd)
        for c in range(chunks):
            rs = slice(c * CHUNK, (c + 1) * CHUNK)
            vm = jnp.dot(w, vn[rs, cs], preferred_element_type=F32) + bs_ref[:, cs]
            o_ref[rs, cs] = (u_ref[rs, cs].astype(F32) * vm).astype(o_ref.dtype)


def _gmlp_gate(uv, ln_g, ln_b, w_s, b_s, layer_j):
    m, d2 = uv.shape
    d = d2 // 2
    groups = w_s.shape[1]
    gd = d // groups
    tm = _pick(m, 512, 256, 128)
    bs_full = jnp.repeat(jnp.transpose(b_s[layer_j]), gd, axis=1)
    return pl.pallas_call(
        functools.partial(_gmlp_kernel, chunks=tm // CHUNK, groups=groups, gd=gd),
        grid=(m // tm,),
        in_specs=[
            pl.BlockSpec((tm, d), lambda i: (i, 0)),
            pl.BlockSpec((tm, d), lambda i: (i, 1)),
            pl.BlockSpec((None, 1, d), lambda i: (layer_j, 0, 0)),
            pl.BlockSpec((None, 1, d), lambda i: (layer_j, 0, 0)),
            pl.BlockSpec((None, groups, CHUNK, CHUNK), lambda i: (layer_j, 0, 0, 0)),
            pl.BlockSpec((CHUNK, d), lambda i: (0, 0)),
        ],
        out_specs=pl.BlockSpec((tm, d), lambda i: (i, 0)),
        out_shape=jax.ShapeDtypeStruct((m, d), BF16),
        compiler_params=_params("arbitrary"),
        name="gmlp_gate",
    )(uv, uv, ln_g.reshape(ln_g.shape[0], 1, d), ln_b.reshape(ln_b.shape[0], 1, d), w_s, bs_full)


def _dft_matrix(length):
    n = 2 * length
    split = _pick(length, 32, 16, 8)
    f = jnp.arange(length, dtype=jnp.int32)[:, None]

    def cos_sin(t):
        ang = ((f * t[None, :]) % n).astype(F32) * (2.0 * math.pi / n)
        return jnp.cos(ang), jnp.sin(ang)

    ca, sa = cos_sin(jnp.arange(length // split, dtype=jnp.int32) * split)
    cb, sb = cos_sin(jnp.arange(split, dtype=jnp.int32))
    cos_ft = (ca[:, :, None] * cb[:, None, :] - sa[:, :, None] * sb[:, None, :]).reshape(length, length)
    sin_ft = (sa[:, :, None] * cb[:, None, :] + ca[:, :, None] * sb[:, None, :]).reshape(length, length)
    return jnp.concatenate([cos_ft, -sin_ft], axis=0).astype(BF16)


def _filter_features(length):
    pos = jnp.arange(length, dtype=F32)
    t = jnp.linspace(0.0, 1.0, length, dtype=F32)[:, None]
    bands = jnp.linspace(1e-4, HYENA_BANDS - 1, HYENA_BANDS, dtype=F32)
    ang = (2.0 * math.pi / length) * pos[:, None] * bands[None, :]
    z = jnp.concatenate([t, jnp.cos(ang), -jnp.sin(ang)], axis=-1)
    return jnp.pad(z, ((0, 0), (0, LANES - z.shape[1])))


def _filt_kernel(z_ref, w1_ref, b1_ref, w2_ref, b2_ref, fr_ref, w00, w01, w10, w11, dl_ref, a_ref, b_ref):
    hi = lax.Precision.HIGHEST
    z = z_ref[...]
    h = jnp.sin(fr_ref[0:1, :] * (jnp.dot(z, w1_ref[...], precision=hi, preferred_element_type=F32) + b1_ref[...]))
    h = jnp.sin(fr_ref[1:2, :] * (jnp.dot(h, w2_ref[...], precision=hi, preferred_element_type=F32) + b2_ref[...]))
    hb = h.astype(BF16)
    decay = jnp.exp(-z[:, 0:1] * dl_ref[...])
    row = lax.broadcasted_iota(jnp.int32, (z.shape[0], 1), 0)
    for o, (wf, wb) in enumerate(((w00, w01), (w10, w11))):
        fwd = jnp.dot(hb, wf[...].astype(BF16), preferred_element_type=F32) * decay
        bwd = jnp.dot(hb, wb[...].astype(BF16), preferred_element_type=F32) * decay
        bwd = jnp.where(row == 0, 0.0, bwd)
        norm = (jnp.sum(jnp.abs(fwd), axis=0, keepdims=True) + jnp.sum(jnp.abs(bwd), axis=0, keepdims=True) + EPS)
        inv = 1.0 / norm
        a_ref[o] = ((fwd + bwd) * inv).astype(a_ref.dtype)
        b_ref[o] = ((fwd - bwd) * inv).astype(b_ref.dtype)


def _hyena_filters(length, d, w1, b1, w2, b2, w3, freq, layer_j):
    fh = w2.shape[-1]
    emb = w1.shape[1]
    bd = _pick(d, 512, 256, 128)
    nb = d // bd
    zfeat = _filter_features(length)
    w1p = jnp.pad(w1[layer_j], ((0, LANES - emb), (0, 0)))
    max_decay = math.log(HYENA_TARGET) / HYENA_FAST_DECAY
    min_decay = math.log(HYENA_TARGET) / HYENA_SLOW_DECAY
    deltas = jnp.abs(jnp.linspace(min_decay, max_decay, d, dtype=F32))[None, :]
    w3_specs = [pl.BlockSpec((None, fh, bd), functools.partial(lambda c, g: (layer_j, 0, g * nb + c), g=g))
                for g in range(4)]
    out_spec = pl.BlockSpec((2, length, bd), lambda c: (0, 0, c))
    return pl.pallas_call(
        _filt_kernel,
        grid=(nb,),
        in_specs=[
            pl.BlockSpec((length, LANES), lambda c: (0, 0)),
            pl.BlockSpec((LANES, fh), lambda c: (0, 0)),
            pl.BlockSpec((None, 1, fh), lambda c: (layer_j, 0, 0)),
            pl.BlockSpec((None, fh, fh), lambda c: (layer_j, 0, 0)),
            pl.BlockSpec((None, 1, fh), lambda c: (layer_j, 0, 0)),
            pl.BlockSpec((None, 2, fh), lambda c: (layer_j, 0, 0)),
            *w3_specs,
            pl.BlockSpec((1, bd), lambda c: (0, c)),
        ],
        out_specs=[out_spec, out_spec],
        out_shape=[jax.ShapeDtypeStruct((2, length, d), BF16)] * 2,
        compiler_params=_params("arbitrary"),
        name="hyena_filters",
    )(zfeat, w1p, b1.reshape(b1.shape[0], 1, fh), w2, b2.reshape(b2.shape[0], 1, fh), freq, w3, w3, w3, w3, deltas)


def _spectrum_kernel(f_ref, a_ref, b_ref, kh_ref, kn_ref, *, length):
    n = 2 * length
    row = lax.broadcasted_iota(jnp.int32, (length, 1), 0)
    wn = jnp.where(row == 0, 1.0 / n, 2.0 / n)
    sgn = jnp.where(row % 2 == 0, 1.0, -1.0)
    a = a_ref[...]
    kh_ref[0:length, :] = jnp.dot(f_ref[0:length, :], a, preferred_element_type=F32) * wn
    kh_ref[length:n, :] = jnp.dot(f_ref[length:n, :], b_ref[...], preferred_element_type=F32) * wn
    nyq = jnp.sum(a.astype(F32) * sgn, axis=0, keepdims=True) * (1.0 / n)
    kn_ref[...] = jnp.broadcast_to(nyq, kn_ref.shape)


def _hyena_spectrum(fmat, a_tab, b_tab):
    _, length, d = a_tab.shape
    bd = _pick(d, 256, 128)
    tab_spec = pl.BlockSpec((None, length, bd), lambda o, c: (o, 0, c))
    return pl.pallas_call(
        functools.partial(_spectrum_kernel, length=length),
        grid=(2, d // bd),
        in_specs=[_const_spec((2 * length, length)), tab_spec, tab_spec],
        out_specs=[pl.BlockSpec((None, 2 * length, bd), lambda o, c: (o, 0, c)),
                   pl.BlockSpec((None, 8, bd), lambda o, c: (o, 0, c))],
        out_shape=[jax.ShapeDtypeStruct((2, 2 * length, d), F32), jax.ShapeDtypeStruct((2, 8, d), F32)],
        compiler_params=_params("arbitrary", "arbitrary"),
        name="hyena_spectrum",
    )(fmat, a_tab, b_tab)


def _hconv_kernel(f_ref, kh_ref, kn_ref, pv_ref, p1_ref, p2_ref, cwv, cw1, cw2, cbv, cb1, cb2, hb_ref, o_ref, y_scr,
                  *, length):
    n = 2 * length
    row = lax.broadcasted_iota(jnp.int32, (length, 1), 0)
    sgn = jnp.where(row % 2 == 0, 1.0, -1.0)

    def short_conv(p_ref, cw, cb):
        p = p_ref[...].astype(F32)
        prev = jnp.where(row == 0, 0.0, pltpu.roll(p, 1, 0))
        nxt = jnp.where(row == length - 1, 0.0, pltpu.roll(p, length - 1, 0))
        return cb[...] + (prev * cw[0:1, :] + p * cw[1:2, :] + nxt * cw[2:3, :])

    fb = min(length, FREQ_ROWS)

    def long_conv(z, o):
        zb = z.astype(BF16)

        def freq_block(i, carry):
            re = pl.ds(pl.multiple_of(i * fb, fb), fb)
            im = pl.ds(pl.multiple_of(length + i * fb, fb), fb)
            zr = jnp.dot(f_ref[re, :], zb, preferred_element_type=F32)
            zi = jnp.dot(f_ref[im, :], zb, preferred_element_type=F32)
            kr, ki = kh_ref[o, re, :], kh_ref[o, im, :]
            y_scr[re, :] = (zr * kr - zi * ki).astype(BF16)
            y_scr[im, :] = (zr * ki + zi * kr).astype(BF16)
            return carry

        lax.fori_loop(0, length // fb, freq_block, 0)
        nyq = jnp.sum(zb.astype(F32) * sgn, axis=0, keepdims=True) * kn_ref[o, 0:1, :]
        y = (jnp.dot(f_ref[0:length, :], y_scr[0:length, :], preferred_element_type=F32)
             + jnp.dot(f_ref[length:n, :], y_scr[length:n, :], preferred_element_type=F32) + sgn * nyq)
        return y + z * hb_ref[o:o + 1, :]

    z = short_conv(p1_ref, cw1, cb1) * long_conv(short_conv(pv_ref, cwv, cbv), 0)
    z = short_conv(p2_ref, cw2, cb2) * long_conv(z, 1)
    o_ref[...] = z.astype(o_ref.dtype)


def _hyena_conv(p, row0, batch, length, fmat, kh, kn, conv_w, conv_b, hbias, layer_j):
    d = p.shape[1] // 3
    bd = _pick(d, 256, 128) if length > 512 else _pick(d, 512, 256, 128)
    nb = d // bd
    r0 = row0 // length

    def pspec(g):
        return pl.BlockSpec((length, bd), lambda c, b: (r0 + b, g * nb + c))

    def wspec(g, rows_):
        return pl.BlockSpec((None, rows_, bd), lambda c, b: (layer_j, 0, g * nb + c))

    return pl.pallas_call(
        functools.partial(_hconv_kernel, length=length),
        grid=(nb, batch),
        in_specs=[
            _const_spec((2 * length, length)),
            pl.BlockSpec((2, 2 * length, bd), lambda c, b: (0, 0, c), pipeline_mode=pl.Buffered(1)),
            pl.BlockSpec((2, 8, bd), lambda c, b: (0, 0, c)),
            pspec(0), pspec(1), pspec(2),
            wspec(0, 3), wspec(1, 3), wspec(2, 3),
            wspec(0, 1), wspec(1, 1), wspec(2, 1),
            pl.BlockSpec((None, 2, bd), lambda c, b: (layer_j, 0, c)),
        ],
        out_specs=pl.BlockSpec((length, bd), lambda c, b: (b, c)),
        out_shape=jax.ShapeDtypeStruct((batch * length, d), BF16),
        scratch_shapes=[pltpu.VMEM((2 * length, bd), BF16)],
        compiler_params=_params("arbitrary", "arbitrary"),
        name="hyena_conv",
    )(fmat, kh, kn, p, p, p, conv_w, conv_w, conv_w, conv_b.reshape(conv_b.shape[0], 1, -1),
      conv_b.reshape(conv_b.shape[0], 1, -1), conv_b.reshape(conv_b.shape[0], 1, -1), hbias)


def _up_kernel(te_ref, x_ref, wg_ref, wu_ref, o_ref):
    del te_ref
    x = x_ref[...]
    g = jnp.dot(x, wg_ref[...].astype(BF16), preferred_element_type=F32)
    u = jnp.dot(x, wu_ref[...].astype(BF16), preferred_element_type=F32)
    o_ref[...] = (g * jax.nn.sigmoid(g) * u).astype(o_ref.dtype)


def _swiglu_up(x, wg4, wu4, layer, tile_expert, tm, bn=None):
    rows_, d = x.shape
    f = wg4.shape[-1]
    bn = bn or _pick(f, 512, 256, 128)
    wspec = pl.BlockSpec((None, None, d, bn), lambda j, i, te: (layer, te[i], 0, j))
    return pl.pallas_call(
        _up_kernel,
        grid_spec=pltpu.PrefetchScalarGridSpec(
            num_scalar_prefetch=1,
            grid=(pl.cdiv(f, bn), rows_ // tm),
            in_specs=[pl.BlockSpec((tm, d), lambda j, i, te: (i, 0)), wspec, wspec],
            out_specs=pl.BlockSpec((tm, bn), lambda j, i, te: (i, j)),
        ),
        out_shape=jax.ShapeDtypeStruct((rows_, f), BF16),
        compiler_params=_params("arbitrary", "arbitrary"),
        name="swiglu_up",
    )(tile_expert, x, wg4, wu4)


def _down_kernel(te_ref, a_ref, w_ref, o_ref):
    del te_ref
    acc = jnp.dot(a_ref[...], w_ref[...].astype(BF16), preferred_element_type=F32)
    for cc in range(o_ref.shape[1]):
        o_ref[:, cc, :] = acc[:, cc * LANES:(cc + 1) * LANES]


def _expert_down(a, wd4, layer, tile_expert, tm):
    rows_, f = a.shape
    d = wd4.shape[-1]
    bn = 1024 if d % 1024 == 0 else d
    return pl.pallas_call(
        _down_kernel,
        grid_spec=pltpu.PrefetchScalarGridSpec(
            num_scalar_prefetch=1,
            grid=(d // bn, rows_ // tm),
            in_specs=[
                pl.BlockSpec((tm, f), lambda j, i, te: (i, 0)),
                pl.BlockSpec((None, None, f, bn), lambda j, i, te: (layer, te[i], 0, j)),
            ],
            out_specs=pl.BlockSpec((tm, bn // LANES, LANES), lambda j, i, te: (i, j, 0)),
        ),
        out_shape=jax.ShapeDtypeStruct((rows_, d // LANES, LANES), F32),
        compiler_params=_params("arbitrary", "arbitrary"),
        name="expert_down",
    )(tile_expert, a, wd4)


def _router_kernel(lg_ref, idx_ref, w_ref, *, n_experts):
    lane = lax.broadcasted_iota(jnp.int32, lg_ref.shape, 1)
    logits = jnp.where(lane < n_experts, lg_ref[...], -jnp.inf)
    m1 = jnp.max(logits, axis=-1, keepdims=True)
    i1 = jnp.min(jnp.where(logits == m1, lane, LANES), axis=-1, keepdims=True)
    rest = jnp.where(lane == i1, -jnp.inf, logits)
    m2 = jnp.max(rest, axis=-1, keepdims=True)
    i2 = jnp.min(jnp.where(rest == m2, lane, LANES), axis=-1, keepdims=True)
    e2 = jnp.exp(m2 - m1)
    den = 1.0 + e2
    idx_ref[...] = jnp.where(lane == 0, i1, jnp.where(lane == 1, i2, 0))
    w_ref[...] = jnp.where(lane == 0, 1.0 / den, jnp.where(lane == 1, e2 / den, 0.0))


def _router(logits, n_experts):
    m = logits.shape[0]
    tm = _pick(m, 1024, 512, 256, 128)
    spec = pl.BlockSpec((tm, LANES), lambda i: (i, 0))
    return pl.pallas_call(
        functools.partial(_router_kernel, n_experts=n_experts),
        grid=(m // tm,),
        in_specs=[spec],
        out_specs=[spec, spec],
        out_shape=[jax.ShapeDtypeStruct((m, LANES), jnp.int32), jax.ShapeDtypeStruct((m, LANES), F32)],
        compiler_params=_params("arbitrary"),
        name="router",
    )(logits)


def _issue_unrolled(n, copy_of):
    def body(g, carry):
        for u in range(DMA_UNROLL):
            copy_of(g * DMA_UNROLL + u).start(priority=u % 2)
        return carry

    lax.fori_loop(0, n // DMA_UNROLL, body, 0)


def _wait_unrolled(n, copy_of):
    def body(g, carry):
        for u in range(DMA_UNROLL):
            copy_of(g * DMA_UNROLL + u).wait()
        return carry

    lax.fori_loop(0, n // DMA_UNROLL, body, 0)


def _gather_kernel(src_ref, nxt_ref, h_hbm, o_ref, buf, sem, *, tm, nc, nsteps):
    i = pl.program_id(0)

    def tok_copy(b, r, t):
        return pltpu.make_async_copy(h_hbm.at[pl.ds(pl.multiple_of(t * nc, nc), nc), :],
                                     buf.at[b, pl.ds(pl.multiple_of(r * nc, nc), nc), :], sem.at[b])

    @pl.when(i == 0)
    def _():
        _issue_unrolled(tm, lambda r: tok_copy(0, r, src_ref[0, 0, r]))

    for b in (0, 1):
        @pl.when(i % 2 == b)
        def _(b=b):
            @pl.when(i + 1 < nsteps)
            def _():
                _issue_unrolled(tm, lambda r: tok_copy(1 - b, r, nxt_ref[0, 0, r]))

            _wait_unrolled(tm, lambda r: tok_copy(b, r, 0))
            for cc in range(nc):
                o_ref[:, cc * LANES:(cc + 1) * LANES] = buf[b, pl.ds(cc, tm, stride=nc), :].astype(o_ref.dtype)


def _gather_rows(h3, src, tm, nc):
    p = src.shape[0]
    nsteps = p // tm
    src3 = src.reshape(nsteps, 1, tm)
    return pl.pallas_call(
        functools.partial(_gather_kernel, tm=tm, nc=nc, nsteps=nsteps),
        grid=(nsteps,),
        in_specs=[
            pl.BlockSpec((1, 1, tm), lambda i: (i, 0, 0), memory_space=pltpu.SMEM),
            pl.BlockSpec((1, 1, tm), lambda i: (jnp.minimum(i + 1, nsteps - 1), 0, 0), memory_space=pltpu.SMEM),
            pl.BlockSpec(memory_space=pl.ANY),
        ],
        out_specs=pl.BlockSpec((tm, nc * LANES), lambda i: (i, 0)),
        out_shape=jax.ShapeDtypeStruct((p, nc * LANES), BF16),
        scratch_shapes=[pltpu.VMEM((2, tm * nc, LANES), F32), pltpu.SemaphoreType.DMA((2,))],
        compiler_params=_params("arbitrary"),
        name="gather_rows",
    )(src3, src3, h3)


class _TailParams:
    def __init__(self, nw_ref, mod_ref, gate_row, nxt):
        self.nw = nw_ref[...]
        self.gate = mod_ref[gate_row:gate_row + 1, :]
        self.has_next = nxt is not None
        if self.has_next:
            ng_ref, nmod_ref, sc_row, sh_row = nxt
            self.ng = ng_ref[...]
            self.sc1 = 1.0 + nmod_ref[sc_row:sc_row + 1, :]
            self.sh = nmod_ref[sh_row:sh_row + 1, :]

    def apply(self, out, x):
        xn = x + self.gate * (_rms(out) * self.nw)
        h = _rms(xn) * self.ng * self.sc1 + self.sh if self.has_next else None
        return xn, h


def _mm_tail_kernel(*refs, tp, two_lhs, gate_row, rows_next, moe, ch):
    refs = list(refs)
    ap_ref = refs.pop(0)
    as_ref = refs.pop(0) if two_lhs else None
    w_ref, x_ref, nw_ref, mod_ref = refs[:4]
    refs = refs[4:]
    nxt = None
    if rows_next is not None:
        nxt = (refs[0], refs[1], rows_next[0], rows_next[1])
        refs = refs[2:]
    rt_ref = refs.pop(0) if moe else None
    xo_ref = refs.pop(0)
    h_ref = refs.pop(0) if rows_next is not None else None
    lg_ref = refs.pop(0) if moe else None
    i = pl.program_id(0)
    tail = _TailParams(nw_ref, mod_ref, gate_row, nxt)
    w = w_ref[...]
    for c in range(x_ref.shape[0] // ch):
        rs = slice(c * ch, (c + 1) * ch)
        a = ap_ref[rs, :]
        if two_lhs:
            a = jnp.where(i < tp, a, as_ref[rs, :])
        xn, h = tail.apply(jnp.dot(a, w, preferred_element_type=F32), x_ref[rs, :])
        xo_ref[rs, :] = xn
        if moe:
            nc = h.shape[1] // LANES
            for cc in range(nc):
                h_ref[pl.ds(c * ch * nc + cc, ch, stride=nc), :] = h[:, cc * LANES:(cc + 1) * LANES]
            h_hi = h.astype(BF16)
            h_lo = (h - h_hi.astype(F32)).astype(BF16)
            lg_ref[rs, :] = (jnp.dot(h_hi, rt_ref[0], preferred_element_type=F32)
                             + (jnp.dot(h_lo, rt_ref[0], preferred_element_type=F32)
                                + jnp.dot(h_hi, rt_ref[1], preferred_element_type=F32)))
        elif h is not None:
            h_ref[rs, :] = h.astype(h_ref.dtype)


def _mm_tail(a, w_bf, x, rows, mod3, layer, gate_row, ng3, nw_row, nxt, router_pad=None):
    two = isinstance(a, tuple)
    k, d = w_bf.shape
    moe = router_pad is not None
    bm = rows.tile(512, 256, 128) if k <= 2048 else rows.tile(256, 128)
    ch = min(bm, 256)
    tp = rows.n_p // bm
    mt = rows.m // bm
    if two:
        lhs_specs = [pl.BlockSpec((bm, k), lambda i: (jnp.minimum(i, tp - 1), 0)),
                     pl.BlockSpec((bm, k), lambda i: (jnp.maximum(i - tp, 0), 0))]
        lhs = list(a)
    else:
        lhs_specs = [pl.BlockSpec((bm, k), lambda i: (i, 0))]
        lhs = [a]

    def modspec(lyr):
        return pl.BlockSpec((None, 6, d), lambda i: (lyr * rows.nseg + rows.seg(i, bm), 0, 0))

    in_specs = lhs_specs + [
        _const_spec((k, d)),
        pl.BlockSpec((bm, d), lambda i: (i, 0)),
        pl.BlockSpec((None, 1, d), lambda i: (nw_row, 0, 0)),
        modspec(layer),
    ]
    args = lhs + [w_bf, x, ng3, mod3]
    out_specs = [pl.BlockSpec((bm, d), lambda i: (i, 0))]
    out_shape = [jax.ShapeDtypeStruct((rows.m, d), F32)]
    rows_next = None
    if nxt is not None:
        n_row, n_layer, sc_row, sh_row = nxt
        in_specs += [pl.BlockSpec((None, 1, d), lambda i: (n_row, 0, 0)), modspec(n_layer)]
        args += [ng3, mod3]
        rows_next = (sc_row, sh_row)
        if moe:
            in_specs.append(pl.BlockSpec((2, d, LANES), lambda i: (0, 0, 0)))
            r_hi = router_pad.astype(BF16)
            args.append(jnp.stack([r_hi, (router_pad - r_hi.astype(F32)).astype(BF16)]))
            out_specs += [pl.BlockSpec((bm * d // LANES, LANES), lambda i: (i, 0)),
                          pl.BlockSpec((bm, LANES), lambda i: (i, 0))]
            out_shape += [jax.ShapeDtypeStruct((rows.m * d // LANES, LANES), F32),
                          jax.ShapeDtypeStruct((rows.m, LANES), F32)]
        else:
            out_specs.append(pl.BlockSpec((bm, d), lambda i: (i, 0)))
            out_shape.append(jax.ShapeDtypeStruct((rows.m, d), BF16))
    res = pl.pallas_call(
        functools.partial(_mm_tail_kernel, tp=tp, two_lhs=two, gate_row=gate_row, rows_next=rows_next, moe=moe,
                          ch=ch),
        grid=(mt,),
        in_specs=in_specs,
        out_specs=out_specs,
        out_shape=out_shape,
        compiler_params=_params("arbitrary"),
        name="matmul_tail",
    )(*args)
    return tuple(res) + (None,) * (3 - len(res))


def _combine_kernel(*refs, tm, nc, gate_row, rows_next):
    refs = list(refs)
    pos_ref, y_hbm, wt_ref, x_ref, nw_ref, mod_ref = refs[:6]
    refs = refs[6:]
    nxt = None
    if rows_next is not None:
        nxt = (refs[0], refs[1], rows_next[0], rows_next[1])
        refs = refs[2:]
    xo_ref = refs.pop(0)
    h_ref = refs.pop(0) if rows_next is not None else None
    buf, sem = refs

    def tok_copy(r, slot, t):
        return pltpu.make_async_copy(y_hbm.at[pl.ds(pl.multiple_of(t * nc, nc), nc), :],
                                     buf.at[slot, pl.ds(pl.multiple_of(r * nc, nc), nc), :], sem)

    for slot in range(TOP_K):
        _issue_unrolled(tm, lambda r, slot=slot: tok_copy(r, slot, pos_ref[0, slot, r]))
    for slot in range(TOP_K):
        _wait_unrolled(tm, lambda r, slot=slot: tok_copy(r, slot, 0))
    tail = _TailParams(nw_ref, mod_ref, gate_row, nxt)

    def chunk(r, carry):
        rs = pl.ds(pl.multiple_of(r * TAIL_ROWS, TAIL_ROWS), TAIL_ROWS)
        gates = [wt_ref[rs, slot:slot + 1] for slot in range(TOP_K)]
        cols = []
        for cc in range(nc):
            slab = pl.ds(r * (TAIL_ROWS * nc) + cc, TAIL_ROWS, stride=nc)
            acc = gates[0] * buf[0, slab, :]
            for slot in range(1, TOP_K):
                acc = acc + gates[slot] * buf[slot, slab, :]
            cols.append(acc)
        xn, h = tail.apply(jnp.concatenate(cols, axis=1), x_ref[rs, :])
        xo_ref[rs, :] = xn
        if h is not None:
            h_ref[rs, :] = h.astype(h_ref.dtype)
        return carry

    lax.fori_loop(0, tm // TAIL_ROWS, chunk, 0)


def _combine_tail(ys3, pos, wts, x, rows, mod3, layer, gate_row, ng3, nw_row, nxt, *, row0, nrows):
    nc = ys3.shape[1]
    d = nc * LANES
    ys3 = ys3.reshape(-1, LANES)
    tm = _pick(math.gcd(row0, nrows) if row0 else nrows, 256, 128)
    t0 = row0 // tm
    nt = nrows // tm

    def modspec(lyr):
        return pl.BlockSpec((None, 6, d), lambda i: (lyr * rows.nseg + rows.seg(t0 + i, tm), 0, 0))

    in_specs = [
        pl.BlockSpec((1, TOP_K, tm), lambda i: (t0 + i, 0, 0), memory_space=pltpu.SMEM),
        pl.BlockSpec(memory_space=pl.ANY),
        pl.BlockSpec((tm, LANES), lambda i: (t0 + i, 0)),
        pl.BlockSpec((tm, d), lambda i: (t0 + i, 0)),
        pl.BlockSpec((None, 1, d), lambda i: (nw_row, 0, 0)),
        modspec(layer),
    ]
    pos3 = jnp.transpose(pos.reshape(TOP_K, rows.m // tm, tm), (1, 0, 2))
    args = [pos3, ys3, wts, x, ng3, mod3]
    rows_next = None
    out_specs = [pl.BlockSpec((tm, d), lambda i: (i, 0))]
    out_shape = [jax.ShapeDtypeStruct((nrows, d), F32)]
    if nxt is not None:
        n_row, n_layer, sc_row, sh_row = nxt
        in_specs += [pl.BlockSpec((None, 1, d), lambda i: (n_row, 0, 0)), modspec(n_layer)]
        args += [ng3, mod3]
        out_specs.append(pl.BlockSpec((tm, d), lambda i: (i, 0)))
        out_shape.append(jax.ShapeDtypeStruct((nrows, d), BF16))
        rows_next = (sc_row, sh_row)
    res = pl.pallas_call(
        functools.partial(_combine_kernel, tm=tm, nc=nc, gate_row=gate_row, rows_next=rows_next),
        grid=(nt,),
        in_specs=in_specs,
        out_specs=out_specs,
        out_shape=out_shape,
        scratch_shapes=[pltpu.VMEM((TOP_K, tm * nc, LANES), F32), pltpu.SemaphoreType.DMA(())],
        compiler_params=_params("arbitrary"),
        name="combine_tail",
    )(*args)
    return (res[0], res[1]) if nxt is not None else (res[0], None)


def _routing_tables(idx, n_experts, tm):
    m = idx.shape[1]
    p = TOP_K * m + n_experts * tm
    e = idx.reshape(-1)
    onehot = (e[:, None] == jnp.arange(n_experts, dtype=jnp.int32)[None, :]).astype(jnp.int32)
    rank = jnp.sum((jnp.cumsum(onehot, axis=0) - 1) * onehot, axis=1)
    counts = jnp.sum(onehot, axis=0)
    padded = ((counts + tm - 1) // tm) * tm
    ends = jnp.cumsum(padded)
    starts = ends - padded
    pos = starts[e] + rank
    tok = jnp.tile(jnp.arange(m, dtype=jnp.int32), TOP_K)
    src = jnp.zeros((p,), jnp.int32).at[pos].set(tok, unique_indices=True)
    tile_start = jnp.arange(p // tm, dtype=jnp.int32) * tm
    tile_expert = jnp.sum((tile_start[:, None] >= ends[None, :]).astype(jnp.int32), axis=1)
    return src, pos.reshape(TOP_K, m), jnp.minimum(tile_expert, n_experts - 1)


def kernel(x_prompt, x_sample, cache_k, cache_v, c, c_ctx, ada_w, ada_b, norm_g, attn_w_in, attn_lambda, attn_subln, attn_w_out, gmlp_w_in, gmlp_ln_g, gmlp_ln_b, gmlp_w_s, gmlp_b_s, gmlp_w_out, hyena_w_in, hyena_b_in, hyena_conv_w, hyena_conv_b, hyena_ffn_w1, hyena_ffn_b1, hyena_ffn_w2, hyena_ffn_b2, hyena_ffn_w3, hyena_sin_freq, hyena_bias, hyena_w_out, ffn_w_gate, ffn_w_up, ffn_w_down, moe_router, moe_w_gate, moe_w_up, moe_w_down):
    batch, seq, d = x_prompt.shape
    dec_batch, dec_seq, _ = x_sample.shape
    depth = ada_w.shape[0]
    n_attn = attn_w_in.shape[0]
    past = cache_k.shape[2]
    n_experts = moe_router.shape[-1]
    rows = _Rows(batch * seq, dec_batch * dec_seq, dec_seq)
    n_p, n_s, m = rows.n_p, rows.n_s, rows.m
    heads = d // (2 * HEAD_DIM)

    cond8 = jnp.concatenate([c_ctx[None, :], c, jnp.zeros((8 - rows.nseg, d), F32)], axis=0)
    mod = _modulation(cond8, ada_w, ada_b)[:, :rows.nseg]
    mod3 = mod.reshape(depth * rows.nseg, 6, d)
    ng3 = norm_g.reshape(depth * 4, 1, d)

    x, h = _prenorm_join(x_prompt.reshape(n_p, d), x_sample.reshape(n_s, d), ng3, mod3, rows, 0)

    rope = _rope_tables(dec_seq)
    ck4 = cache_k.reshape(dec_batch, n_attn, past, d)
    cv4 = cache_v.reshape(dec_batch, n_attn, past, d)
    new_k = new_v = None
    y_prompt = y_sample = None

    for i in range(depth):
        kind, j = i % 3, i // 3
        if kind == 0:
            lam_init = 0.8 - 0.6 * math.exp(-0.3 * i)
            q_p = _proj(h, attn_w_in, j, row0=0, nrows=n_p, col0=0, ncols=d)
            new_k = _proj(h, attn_w_in, j, row0=0, nrows=n_p, col0=d, ncols=d, out_dtype=F32,
                          cache=(new_k, n_attn, j, batch, seq))
            new_v = _proj(h, attn_w_in, j, row0=0, nrows=n_p, col0=2 * d, ncols=d, out_dtype=F32,
                          cache=(new_v, n_attn, j, batch, seq))
            qk_s = _proj(h, attn_w_in, j, row0=n_p, nrows=n_s, col0=0, ncols=2 * d, epilogue="rope", rope=rope,
                         seq=dec_seq)
            v_s = _proj(h, attn_w_in, j, row0=n_p, nrows=n_s, col0=2 * d, ncols=d)
            o_p = _attention(q_p, 0, new_k, j, 0, new_v, j, None, attn_lambda, attn_subln, j, lam_init,
                             batch=batch, seq=seq, heads_per_step=heads, bq=seq, lag=4 * heads)
            o_s = _attention(qk_s, 0, qk_s.reshape(dec_batch, 1, dec_seq, 2 * d), 0, d,
                             v_s.reshape(dec_batch, 1, dec_seq, d), 0, (ck4, cv4), attn_lambda, attn_subln, j,
                             lam_init, batch=dec_batch, seq=dec_seq, heads_per_step=min(2, heads),
                             bq=_pick(dec_seq, 256, 128), lag=8)
            mix, w_out = (o_p, o_s), attn_w_out
        elif kind == 1:
            uv = _proj(h, gmlp_w_in, j, row0=0, nrows=m, col0=0, ncols=2 * d, epilogue="gelu")
            mix, w_out = _gmlp_gate(uv, gmlp_ln_g, gmlp_ln_b, gmlp_w_s, gmlp_b_s, j), gmlp_w_out
        else:
            p = _proj(h, hyena_w_in, j, row0=0, nrows=m, col0=0, ncols=3 * d, epilogue="bias", bias=hyena_b_in)
            outs = []
            for row0, nb, length in ((0, batch, seq), (n_p, dec_batch, dec_seq)):
                fmat = _dft_matrix(length)
                a_tab, b_tab = _hyena_filters(length, d, hyena_ffn_w1, hyena_ffn_b1, hyena_ffn_w2, hyena_ffn_b2,
                                              hyena_ffn_w3, hyena_sin_freq, j)
                kh, kn = _hyena_spectrum(fmat, a_tab, b_tab)
                outs.append(_hyena_conv(p, row0, nb, length, fmat, kh, kn, hyena_conv_w, hyena_conv_b, hyena_bias, j))
            mix, w_out = tuple(outs), hyena_w_out

        moe_layer = i % 2 == 1
        jj = i // 2
        router_pad = jnp.pad(moe_router[jj], ((0, 0), (0, LANES - n_experts))) if moe_layer else None
        x, h2, logits = _mm_tail(mix, w_out[j].astype(BF16), x, rows, mod3, i, G1, ng3, i * 4 + 1,
                                 (i * 4 + 2, i, SC2, SH2), router_pad=router_pad)

        last = i == depth - 1
        nxt = None if last else ((i + 1) * 4, i + 1, SC1, SH1)
        if not moe_layer:
            tm = rows.tile(1024, 512, 256, 128)
            a = _swiglu_up(h2, ffn_w_gate[:, None], ffn_w_up[:, None], jj, jnp.zeros((m // tm,), jnp.int32), tm)
            x, h, _ = _mm_tail(a, ffn_w_down[jj].astype(BF16), x, rows, mod3, i, G2, ng3, i * 4 + 3, nxt)
        else:
            tm = rows.tile(512, 256, 128)
            idx, wts = _router(logits, n_experts)
            src, pos, tile_expert = _routing_tables(jnp.transpose(idx[:, :TOP_K]), n_experts, tm)
            xs = _gather_rows(h2, src, tm, d // LANES)
            a = _swiglu_up(xs, moe_w_gate, moe_w_up, jj, tile_expert, tm, bn=min(1024, moe_w_gate.shape[-1]))
            ys = _expert_down(a, moe_w_down, jj, tile_expert, tm)
            tail_args = (ys, pos, wts, x, rows, mod3, i, G2, ng3, i * 4 + 3)
            if last:
                y_prompt, _ = _combine_tail(*tail_args, None, row0=0, nrows=n_p)
                y_sample, _ = _combine_tail(*tail_args, None, row0=n_p, nrows=n_s)
            else:
                x, h = _combine_tail(*tail_args, nxt, row0=0, nrows=m)

    if y_prompt is None:
        y_prompt, y_sample = x[:n_p], x[n_p:]
    y_prompt = y_prompt.reshape(batch, seq, d)
    y_sample = y_sample.reshape(dec_batch, dec_seq, d)
    hk = cache_k.shape[3]
    hv = cache_v.shape[3]
    return (y_prompt, y_sample, new_k.reshape(batch, n_attn, seq, hk, d // hk),
            new_v.reshape(batch, n_attn, seq, hv, d // hv))
```

```python
import functools
import math

import jax
import jax.numpy as jnp
from jax import lax
from jax.experimental import pallas as pl
from jax.experimental.pallas import tpu as pltpu

F32 = jnp.float32
BF16 = jnp.bfloat16

EPS = 1e-6
GRID_W = 64
ROPE_THETA = 10000.0
CHUNK = 128
HYENA_BANDS = 16
HYENA_FAST_DECAY = 0.3
HYENA_SLOW_DECAY = 1.5
HYENA_TARGET = 0.01
HEAD_DIM = 64
TOP_K = 2

V7X_VMEM_BYTES = 64 * 1024 * 1024
VMEM_LIMIT = V7X_VMEM_BYTES - 8 * 1024 * 1024
LANES = 128
TAIL_ROWS = 128
FREQ_ROWS = 512
DMA_UNROLL = 8

SH1, SC1, G1, SH2, SC2, G2 = range(6)


def _pick(n, *cands):
    for c in cands:
        if n % c == 0:
            return c
    return n


def _params(*sem):
    return pltpu.CompilerParams(dimension_semantics=sem, vmem_limit_bytes=VMEM_LIMIT)


def _const_spec(shape):
    nd = len(shape)
    return pl.BlockSpec(shape, lambda *_: (0,) * nd, pipeline_mode=pl.Buffered(1))


def _rms(x):
    return x * lax.rsqrt(jnp.mean(x * x, axis=-1, keepdims=True) + EPS)


class _Rows:
    def __init__(self, n_p, n_s, dec_seq):
        self.n_p, self.n_s, self.dec_seq = n_p, n_s, dec_seq
        self.m = n_p + n_s
        self.nseg = 1 + n_s // dec_seq

    def seg(self, i, bm):
        r = i * bm
        return jnp.where(r < self.n_p, 0, 1 + (r - self.n_p) // self.dec_seq)

    def tile(self, *cands):
        return _pick(math.gcd(self.n_p, self.dec_seq), *cands)


def _mod_kernel(cond_ref, w_ref, b_ref, o_ref):
    s = cond_ref[...]
    s = s * jax.nn.sigmoid(s)
    o_ref[...] = jnp.dot(s.astype(BF16), w_ref[...].astype(BF16), preferred_element_type=F32) + b_ref[...]


def _modulation(cond8, ada_w, ada_b):
    depth, d, n6 = ada_w.shape
    bn = _pick(n6, 1024, 512, 256, 128)
    return pl.pallas_call(
        _mod_kernel,
        grid=(depth, n6 // bn),
        in_specs=[
            pl.BlockSpec((8, d), lambda l, j: (0, 0)),
            pl.BlockSpec((None, d, bn), lambda l, j: (l, 0, j)),
            pl.BlockSpec((None, 1, bn), lambda l, j: (l, 0, j)),
        ],
        out_specs=pl.BlockSpec((None, 8, bn), lambda l, j: (l, 0, j)),
        out_shape=jax.ShapeDtypeStruct((depth, 8, n6), F32),
        compiler_params=_params("arbitrary", "arbitrary"),
        name="modulation",
    )(cond8, ada_w, ada_b.reshape(depth, 1, n6))


def _pre_kernel(xp_ref, xs_ref, g_ref, mod_ref, x_ref, h_ref, *, tp):
    i = pl.program_id(0)

    def emit(src_ref):
        x = src_ref[...]
        x_ref[...] = x
        h = _rms(x) * g_ref[...] * (1.0 + mod_ref[SC1:SC1 + 1, :]) + mod_ref[SH1:SH1 + 1, :]
        h_ref[...] = h.astype(h_ref.dtype)

    @pl.when(i < tp)
    def _():
        emit(xp_ref)

    @pl.when(i >= tp)
    def _():
        emit(xs_ref)


def _prenorm_join(xp, xs, ng3, mod3, rows, layer):
    d = xp.shape[-1]
    bm = rows.tile(512, 256, 128)
    tp = rows.n_p // bm
    ts = rows.n_s // bm
    return pl.pallas_call(
        functools.partial(_pre_kernel, tp=tp),
        grid=(tp + ts,),
        in_specs=[
            pl.BlockSpec((bm, d), lambda i: (jnp.minimum(i, tp - 1), 0)),
            pl.BlockSpec((bm, d), lambda i: (jnp.maximum(i - tp, 0), 0)),
            pl.BlockSpec((None, 1, d), lambda i: (layer * 4, 0, 0)),
            pl.BlockSpec((None, 6, d), lambda i: (layer * rows.nseg + rows.seg(i, bm), 0, 0)),
        ],
        out_specs=[pl.BlockSpec((bm, d), lambda i: (i, 0)), pl.BlockSpec((bm, d), lambda i: (i, 0))],
        out_shape=[jax.ShapeDtypeStruct((rows.m, d), F32), jax.ShapeDtypeStruct((rows.m, d), BF16)],
        compiler_params=_params("arbitrary"),
        name="prenorm_join",
    )(xp, xs, ng3, mod3)


def _proj_kernel(*refs, epilogue, half, cache_rows, cache_slot=None):
    if epilogue == "bias":
        x_ref, w_ref, b_ref, o_ref = refs
    elif epilogue == "rope":
        x_ref, w_ref, cos_ref, sin_ref, o_ref = refs
    else:
        x_ref, w_ref, o_ref = refs
    acc = jnp.dot(x_ref[...], w_ref[...].astype(BF16), preferred_element_type=F32)
    if epilogue == "bias":
        acc = acc + b_ref[...]
    elif epilogue == "gelu":
        acc = 0.5 * acc * (1.0 + lax.erf(acc * (2.0 ** -0.5)))
    if epilogue == "rope":
        cos_t = cos_ref[...]
        sin_t = sin_ref[...]
        lane = lax.broadcasted_iota(jnp.int32, cos_t.shape, 1)
        first = (lane % (2 * half)) < half
        for cblk in range(acc.shape[1] // LANES):
            blk = acc[:, cblk * LANES:(cblk + 1) * LANES]
            partner = jnp.where(first, pltpu.roll(blk, LANES - half, 1), pltpu.roll(blk, half, 1))
            o_ref[:, cblk * LANES:(cblk + 1) * LANES] = (blk * cos_t + partner * sin_t).astype(o_ref.dtype)
    elif cache_slot is not None:
        o_ref[...] = jnp.zeros(o_ref.shape, o_ref.dtype)
        o_ref[:, cache_slot] = acc.reshape(o_ref.shape[0], *o_ref.shape[2:]).astype(o_ref.dtype)
    elif cache_rows:
        o_ref[...] = acc.reshape(o_ref.shape).astype(o_ref.dtype)
    else:
        o_ref[...] = acc.astype(o_ref.dtype)


def _proj(h, w, layer, *, row0, nrows, col0, ncols, out_dtype=BF16, epilogue="none", bias=None,
          rope=None, seq=None, cache=None):
    k = h.shape[1]
    bm = _pick(math.gcd(math.gcd(row0, nrows), seq or 0), 1024, 512, 256, 128)
    bn = _pick(math.gcd(col0, ncols) if col0 else ncols, 1024, 512, 256, 128)
    r0, c0 = row0 // bm, col0 // bn
    in_specs = [
        pl.BlockSpec((bm, k), lambda j, i: (r0 + i, 0)),
        pl.BlockSpec((None, k, bn), lambda j, i: (layer, 0, c0 + j)),
    ]
    args = [h, w]
    half = 0
    if epilogue == "bias":
        in_specs.append(pl.BlockSpec((None, 1, bn), lambda j, i: (layer, 0, c0 + j)))
        args.append(bias.reshape(bias.shape[0], 1, bias.shape[1]))
    elif epilogue == "rope":
        cos_t, sin_t, half = rope
        per_seq = seq // bm
        in_specs += [pl.BlockSpec((bm, LANES), lambda j, i: (i % per_seq, 0))] * 2
        args += [cos_t, sin_t]
    aliases = {}
    first_slot = None
    if cache is None:
        out_spec = pl.BlockSpec((bm, bn), lambda j, i: (i, j))
        out_shape = jax.ShapeDtypeStruct((nrows, ncols), out_dtype)
    else:
        arr, n_slots, slot, batch, cseq = cache
        bb = bm // cseq
        out_shape = jax.ShapeDtypeStruct((batch, n_slots, cseq, ncols), out_dtype)
        if arr is None:
            out_spec = pl.BlockSpec((bb, n_slots, cseq, bn), lambda j, i: (i, 0, 0, j))
            first_slot = slot
        else:
            out_spec = pl.BlockSpec((bb, None, cseq, bn), lambda j, i: (i, slot, 0, j))
            in_specs.append(pl.BlockSpec(memory_space=pl.ANY))
            args.append(arr)
            aliases = {len(args) - 1: 0}

    def body(*refs):
        if aliases:
            refs = refs[:len(args) - 1] + refs[len(args):]
        _proj_kernel(*refs, epilogue=epilogue, half=half, cache_rows=cache is not None, cache_slot=first_slot)

    return pl.pallas_call(
        body,
        grid=(ncols // bn, nrows // bm),
        in_specs=in_specs,
        out_specs=out_spec,
        out_shape=out_shape,
        input_output_aliases=aliases,
        compiler_params=_params("arbitrary", "arbitrary"),
        name="proj_" + epilogue,
    )(*args)


def _rope_tables(dec_seq):
    rows = dec_seq // GRID_W
    row = jnp.repeat(jnp.arange(rows), GRID_W).astype(F32)
    col = jnp.tile(jnp.arange(GRID_W), rows).astype(F32)
    half = HEAD_DIM // 4
    inv = ROPE_THETA ** (-jnp.arange(half, dtype=F32) / half)
    ar = row[:, None] * inv[None, :]
    ac = col[:, None] * inv[None, :]
    cos64 = jnp.concatenate([jnp.cos(ar), jnp.cos(ar), jnp.cos(ac), jnp.cos(ac)], axis=-1)
    sin64 = jnp.concatenate([-jnp.sin(ar), jnp.sin(ar), -jnp.sin(ac), jnp.sin(ac)], axis=-1)
    reps = LANES // HEAD_DIM
    return jnp.tile(cos64, (1, reps)), jnp.tile(sin64, (1, reps)), half


def _attn_kernel(*refs, heads, has_cache, lam_init, lag):
    if has_cache:
        lam_ref, sub_ref, q_ref, k_ref, v_ref, kc_ref, vc_ref, o_ref = refs
    else:
        lam_ref, sub_ref, q_ref, k_ref, v_ref, o_ref = refs
    p = lam_ref[...]
    lam = (jnp.exp(jnp.sum(p[0:1] * p[1:2], axis=-1, keepdims=True))
           - jnp.exp(jnp.sum(p[2:3] * p[3:4], axis=-1, keepdims=True)) + lam_init)
    scale = HEAD_DIM ** -0.5
    assert math.log2(HEAD_DIM) % 2 == 0, "folding the score scale into bf16 queries needs a power-of-two scale"
    nt = (((1,), (1,)), ((), ()))
    head_slices = [slice(h * 2 * HEAD_DIM, (h + 1) * 2 * HEAD_DIM) for h in range(heads)]
    n_units = 2 * heads

    def score_stage(u):
        sl, ms = head_slices[u // 2], slice((u % 2) * HEAD_DIM, (u % 2 + 1) * HEAD_DIM)
        q = q_ref[:, sl][:, ms] * scale
        s = lax.dot_general(q, k_ref[:, sl][:, ms].astype(BF16), nt, preferred_element_type=F32)
        sc = None
        if has_cache:
            sc = lax.dot_general(q, kc_ref[:, sl][:, ms].astype(BF16), nt, preferred_element_type=F32)
        return s, sc

    def exp_stage(s, sc):
        mx = jnp.max(s, axis=-1, keepdims=True)
        if has_cache:
            mx = jnp.maximum(mx, jnp.max(sc, axis=-1, keepdims=True))
        return jnp.exp((s - mx).astype(BF16)), jnp.exp((sc - mx).astype(BF16)) if has_cache else None

    def value_stage(u, e, ec):
        sl = head_slices[u // 2]
        v = v_ref[:, sl].astype(BF16)
        acc = jnp.dot(e, jnp.concatenate([v, jnp.ones_like(v)], axis=1), preferred_element_type=F32)
        if has_cache:
            vc = vc_ref[:, sl].astype(BF16)
            acc = acc + jnp.dot(ec, jnp.concatenate([vc, jnp.ones_like(vc)], axis=1), preferred_element_type=F32)
        vd = acc.shape[1] // 2
        return acc[:, :vd] / acc[:, vd:]

    half = lag // 2
    scores, weights, maps = {}, {}, {}
    for t in range(n_units + lag):
        if 0 <= t - lag:
            maps[t - lag] = value_stage(t - lag, *weights.pop(t - lag))
        if 0 <= t - half < n_units:
            weights[t - half] = exp_stage(*scores.pop(t - half))
        if t < n_units:
            scores[t] = score_stage(t)
    for h, sl in enumerate(head_slices):
        o = maps[2 * h] - lam * maps[2 * h + 1]
        o = _rms(o) * sub_ref[...] * (1.0 - lam_init)
        o_ref[:, sl] = o.astype(o_ref.dtype)


def _attention(q, q_col0, k4, k_slot, k_col0, v4, v_slot, cache4, lam, subln, layer_j, lam_init, *, batch, seq,
               heads_per_step, bq, lag):
    d = v4.shape[-1]
    hw = heads_per_step * 2 * HEAD_DIM
    nk = k4.shape[2]
    per_seq = seq // bq
    qc0, kc0 = q_col0 // hw, k_col0 // hw
    in_specs = [
        pl.BlockSpec((None, 4, HEAD_DIM), lambda b, h, i: (layer_j, 0, 0)),
        pl.BlockSpec((None, 1, 2 * HEAD_DIM), lambda b, h, i: (layer_j, 0, 0)),
        pl.BlockSpec((bq, hw), lambda b, h, i: (b * per_seq + i, qc0 + h)),
        pl.BlockSpec((None, None, nk, hw), lambda b, h, i: (b, k_slot, 0, kc0 + h)),
        pl.BlockSpec((None, None, nk, hw), lambda b, h, i: (b, v_slot, 0, h)),
    ]
    args = [lam, subln.reshape(subln.shape[0], 1, subln.shape[1]), q, k4, v4]
    if cache4 is not None:
        ck, cv = cache4
        nc = ck.shape[2]
        in_specs += [pl.BlockSpec((None, None, nc, hw), lambda b, h, i: (b, layer_j, 0, h))] * 2
        args += [ck, cv]
    return pl.pallas_call(
        functools.partial(_attn_kernel, heads=heads_per_step, has_cache=cache4 is not None, lam_init=lam_init,
                          lag=lag),
        grid=(batch, d // hw, per_seq),
        in_specs=in_specs,
        out_specs=pl.BlockSpec((bq, hw), lambda b, h, i: (b * per_seq + i, h)),
        out_shape=jax.ShapeDtypeStruct((batch * seq, d), BF16),
        compiler_params=_params("arbitrary", "arbitrary", "arbitrary"),
        name="diff_attention",
    )(*args)


def _gmlp_kernel(u_ref, v_ref, g_ref, b_ref, ws_ref, bs_ref, o_ref, *, chunks, groups, gd):
    v = v_ref[...].astype(F32)
    mu = jnp.mean(v, axis=-1, keepdims=True)
    vc = v - mu
    var = jnp.mean(vc * vc, axis=-1, keepdims=True)
    vn = (vc * lax.rsqrt(var + EPS) * g_ref[...] + b_ref[...]).astype(BF16)
    for g in range(groups):
        w = ws_ref[g].astype(BF16)
        cs = slice(g * gd, (g + 1) * gd)
        for c in range(chunks):
            rs = slice(c * CHUNK, (c + 1) * CHUNK)
            vm = jnp.dot(w, vn[rs, cs], preferred_element_type=F32) + bs_ref[:, cs]
            o_ref[rs, cs] = (u_ref[rs, cs].astype(F32) * vm).astype(o_ref.dtype)


def _gmlp_gate(uv, ln_g, ln_b, w_s, b_s, layer_j):
    m, d2 = uv.shape
    d = d2 // 2
    groups = w_s.shape[1]
    gd = d // groups
    tm = _pick(m, 512, 256, 128)
    bs_full = jnp.repeat(jnp.transpose(b_s[layer_j]), gd, axis=1)
    return pl.pallas_call(
        functools.partial(_gmlp_kernel, chunks=tm // CHUNK, groups=groups, gd=gd),
        grid=(m // tm,),
        in_specs=[
            pl.BlockSpec((tm, d), lambda i: (i, 0)),
            pl.BlockSpec((tm, d), lambda i: (i, 1)),
            pl.BlockSpec((None, 1, d), lambda i: (layer_j, 0, 0)),
            pl.BlockSpec((None, 1, d), lambda i: (layer_j, 0, 0)),
            pl.BlockSpec((None, groups, CHUNK, CHUNK), lambda i: (layer_j, 0, 0, 0)),
            pl.BlockSpec((CHUNK, d), lambda i: (0, 0)),
        ],
        out_specs=pl.BlockSpec((tm, d), lambda i: (i, 0)),
        out_shape=jax.ShapeDtypeStruct((m, d), BF16),
        compiler_params=_params("arbitrary"),
        name="gmlp_gate",
    )(uv, uv, ln_g.reshape(ln_g.shape[0], 1, d), ln_b.reshape(ln_b.shape[0], 1, d), w_s, bs_full)


def _dft_matrix(length):
    n = 2 * length
    split = _pick(length, 32, 16, 8)
    f = jnp.arange(length, dtype=jnp.int32)[:, None]

    def cos_sin(t):
        ang = ((f * t[None, :]) % n).astype(F32) * (2.0 * math.pi / n)
        return jnp.cos(ang), jnp.sin(ang)

    ca, sa = cos_sin(jnp.arange(length // split, dtype=jnp.int32) * split)
    cb, sb = cos_sin(jnp.arange(split, dtype=jnp.int32))
    cos_ft = (ca[:, :, None] * cb[:, None, :] - sa[:, :, None] * sb[:, None, :]).reshape(length, length)
    sin_ft = (sa[:, :, None] * cb[:, None, :] + ca[:, :, None] * sb[:, None, :]).reshape(length, length)
    return jnp.concatenate([cos_ft, -sin_ft], axis=0).astype(BF16)


def _filter_features(length):
    pos = jnp.arange(length, dtype=F32)
    t = jnp.linspace(0.0, 1.0, length, dtype=F32)[:, None]
    bands = jnp.linspace(1e-4, HYENA_BANDS - 1, HYENA_BANDS, dtype=F32)
    ang = (2.0 * math.pi / length) * pos[:, None] * bands[None, :]
    z = jnp.concatenate([t, jnp.cos(ang), -jnp.sin(ang)], axis=-1)
    return jnp.pad(z, ((0, 0), (0, LANES - z.shape[1])))


def _filt_kernel(z_ref, w1_ref, b1_ref, w2_ref, b2_ref, fr_ref, w00, w01, w10, w11, dl_ref, a_ref, b_ref):
    hi = lax.Precision.HIGHEST
    z = z_ref[...]
    h = jnp.sin(fr_ref[0:1, :] * (jnp.dot(z, w1_ref[...], precision=hi, preferred_element_type=F32) + b1_ref[...]))
    h = jnp.sin(fr_ref[1:2, :] * (jnp.dot(h, w2_ref[...], precision=hi, preferred_element_type=F32) + b2_ref[...]))
    hb = h.astype(BF16)
    decay = jnp.exp(-z[:, 0:1] * dl_ref[...])
    row = lax.broadcasted_iota(jnp.int32, (z.shape[0], 1), 0)
    for o, (wf, wb) in enumerate(((w00, w01), (w10, w11))):
        fwd = jnp.dot(hb, wf[...].astype(BF16), preferred_element_type=F32) * decay
        bwd = jnp.dot(hb, wb[...].astype(BF16), preferred_element_type=F32) * decay
        bwd = jnp.where(row == 0, 0.0, bwd)
        norm = (jnp.sum(jnp.abs(fwd), axis=0, keepdims=True) + jnp.sum(jnp.abs(bwd), axis=0, keepdims=True) + EPS)
        inv = 1.0 / norm
        a_ref[o] = ((fwd + bwd) * inv).astype(a_ref.dtype)
        b_ref[o] = ((fwd - bwd) * inv).astype(b_ref.dtype)


def _hyena_filters(length, d, w1, b1, w2, b2, w3, freq, layer_j):
    fh = w2.shape[-1]
    emb = w1.shape[1]
    bd = _pick(d, 512, 256, 128)
    nb = d // bd
    zfeat = _filter_features(length)
    w1p = jnp.pad(w1[layer_j], ((0, LANES - emb), (0, 0)))
    max_decay = math.log(HYENA_TARGET) / HYENA_FAST_DECAY
    min_decay = math.log(HYENA_TARGET) / HYENA_SLOW_DECAY
    deltas = jnp.abs(jnp.linspace(min_decay, max_decay, d, dtype=F32))[None, :]
    w3_specs = [pl.BlockSpec((None, fh, bd), functools.partial(lambda c, g: (layer_j, 0, g * nb + c), g=g))
                for g in range(4)]
    out_spec = pl.BlockSpec((2, length, bd), lambda c: (0, 0, c))
    return pl.pallas_call(
        _filt_kernel,
        grid=(nb,),
        in_specs=[
            pl.BlockSpec((length, LANES), lambda c: (0, 0)),
            pl.BlockSpec((LANES, fh), lambda c: (0, 0)),
            pl.BlockSpec((None, 1, fh), lambda c: (layer_j, 0, 0)),
            pl.BlockSpec((None, fh, fh), lambda c: (layer_j, 0, 0)),
            pl.BlockSpec((None, 1, fh), lambda c: (layer_j, 0, 0)),
            pl.BlockSpec((None, 2, fh), lambda c: (layer_j, 0, 0)),
            *w3_specs,
            pl.BlockSpec((1, bd), lambda c: (0, c)),
        ],
        out_specs=[out_spec, out_spec],
        out_shape=[jax.ShapeDtypeStruct((2, length, d), BF16)] * 2,
        compiler_params=_params("arbitrary"),
        name="hyena_filters",
    )(zfeat, w1p, b1.reshape(b1.shape[0], 1, fh), w2, b2.reshape(b2.shape[0], 1, fh), freq, w3, w3, w3, w3, deltas)


def _spectrum_kernel(f_ref, a_ref, b_ref, kh_ref, kn_ref, *, length):
    n = 2 * length
    row = lax.broadcasted_iota(jnp.int32, (length, 1), 0)
    wn = jnp.where(row == 0, 1.0 / n, 2.0 / n)
    sgn = jnp.where(row % 2 == 0, 1.0, -1.0)
    a = a_ref[...]
    kh_ref[0:length, :] = jnp.dot(f_ref[0:length, :], a, preferred_element_type=F32) * wn
    kh_ref[length:n, :] = jnp.dot(f_ref[length:n, :], b_ref[...], preferred_element_type=F32) * wn
    nyq = jnp.sum(a.astype(F32) * sgn, axis=0, keepdims=True) * (1.0 / n)
    kn_ref[...] = jnp.broadcast_to(nyq, kn_ref.shape)


def _hyena_spectrum(fmat, a_tab, b_tab):
    _, length, d = a_tab.shape
    bd = _pick(d, 256, 128)
    tab_spec = pl.BlockSpec((None, length, bd), lambda o, c: (o, 0, c))
    return pl.pallas_call(
        functools.partial(_spectrum_kernel, length=length),
        grid=(2, d // bd),
        in_specs=[_const_spec((2 * length, length)), tab_spec, tab_spec],
        out_specs=[pl.BlockSpec((None, 2 * length, bd), lambda o, c: (o, 0, c)),
                   pl.BlockSpec((None, 8, bd), lambda o, c: (o, 0, c))],
        out_shape=[jax.ShapeDtypeStruct((2, 2 * length, d), F32), jax.ShapeDtypeStruct((2, 8, d), F32)],
        compiler_params=_params("arbitrary", "arbitrary"),
        name="hyena_spectrum",
    )(fmat, a_tab, b_tab)


def _hconv_kernel(f_ref, kh_ref, kn_ref, pv_ref, p1_ref, p2_ref, cwv, cw1, cw2, cbv, cb1, cb2, hb_ref, o_ref, y_scr,
                  *, length):
    n = 2 * length
    row = lax.broadcasted_iota(jnp.int32, (length, 1), 0)
    sgn = jnp.where(row % 2 == 0, 1.0, -1.0)

    def short_conv(p_ref, cw, cb):
        p = p_ref[...].astype(F32)
        prev = jnp.where(row == 0, 0.0, pltpu.roll(p, 1, 0))
        nxt = jnp.where(row == length - 1, 0.0, pltpu.roll(p, length - 1, 0))
        return cb[...] + (prev * cw[0:1, :] + p * cw[1:2, :] + nxt * cw[2:3, :])

    fb = min(length, FREQ_ROWS)

    def long_conv(z, o):
        zb = z.astype(BF16)

        def freq_block(i, carry):
            re = pl.ds(pl.multiple_of(i * fb, fb), fb)
            im = pl.ds(pl.multiple_of(length + i * fb, fb), fb)
            zr = jnp.dot(f_ref[re, :], zb, preferred_element_type=F32)
            zi = jnp.dot(f_ref[im, :], zb, preferred_element_type=F32)
            kr, ki = kh_ref[o, re, :], kh_ref[o, im, :]
            y_scr[re, :] = (zr * kr - zi * ki).astype(BF16)
            y_scr[im, :] = (zr * ki + zi * kr).astype(BF16)
            return carry

        lax.fori_loop(0, length // fb, freq_block, 0)
        nyq = jnp.sum(zb.astype(F32) * sgn, axis=0, keepdims=True) * kn_ref[o, 0:1, :]
        y = (jnp.dot(f_ref[0:length, :], y_scr[0:length, :], preferred_element_type=F32)
             + jnp.dot(f_ref[length:n, :], y_scr[length:n, :], preferred_element_type=F32) + sgn * nyq)
        return y + z * hb_ref[o:o + 1, :]

    z = short_conv(p1_ref, cw1, cb1) * long_conv(short_conv(pv_ref, cwv, cbv), 0)
    z = short_conv(p2_ref, cw2, cb2) * long_conv(z, 1)
    o_ref[...] = z.astype(o_ref.dtype)


def _hyena_conv(p, row0, batch, length, fmat, kh, kn, conv_w, conv_b, hbias, layer_j):
    d = p.shape[1] // 3
    bd = _pick(d, 256, 128) if length > 512 else _pick(d, 512, 256, 128)
    nb = d // bd
    r0 = row0 // length

    def pspec(g):
        return pl.BlockSpec((length, bd), lambda c, b: (r0 + b, g * nb + c))

    def wspec(g, rows_):
        return pl.BlockSpec((None, rows_, bd), lambda c, b: (layer_j, 0, g * nb + c))

    return pl.pallas_call(
        functools.partial(_hconv_kernel, length=length),
        grid=(nb, batch),
        in_specs=[
            _const_spec((2 * length, length)),
            pl.BlockSpec((2, 2 * length, bd), lambda c, b: (0, 0, c), pipeline_mode=pl.Buffered(1)),
            pl.BlockSpec((2, 8, bd), lambda c, b: (0, 0, c)),
            pspec(0), pspec(1), pspec(2),
            wspec(0, 3), wspec(1, 3), wspec(2, 3),
            wspec(0, 1), wspec(1, 1), wspec(2, 1),
            pl.BlockSpec((None, 2, bd), lambda c, b: (layer_j, 0, c)),
        ],
        out_specs=pl.BlockSpec((length, bd), lambda c, b: (b, c)),
        out_shape=jax.ShapeDtypeStruct((batch * length, d), BF16),
        scratch_shapes=[pltpu.VMEM((2 * length, bd), BF16)],
        compiler_params=_params("arbitrary", "arbitrary"),
        name="hyena_conv",
    )(fmat, kh, kn, p, p, p, conv_w, conv_w, conv_w, conv_b.reshape(conv_b.shape[0], 1, -1),
      conv_b.reshape(conv_b.shape[0], 1, -1), conv_b.reshape(conv_b.shape[0], 1, -1), hbias)


def _up_kernel(te_ref, x_ref, wg_ref, wu_ref, o_ref):
    del te_ref
    x = x_ref[...].astype(BF16)
    g = jnp.dot(x, wg_ref[...].astype(BF16), preferred_element_type=F32)
    u = jnp.dot(x, wu_ref[...].astype(BF16), preferred_element_type=F32)
    o_ref[...] = (g * jax.nn.sigmoid(g) * u).astype(o_ref.dtype)


def _swiglu_up(x, wg4, wu4, layer, tile_expert, tm, bn=None):
    rows_, d = x.shape
    f = wg4.shape[-1]
    bn = bn or _pick(f, 512, 256, 128)
    wspec = pl.BlockSpec((None, None, d, bn), lambda j, i, te: (layer, te[i], 0, j))
    return pl.pallas_call(
        _up_kernel,
        grid_spec=pltpu.PrefetchScalarGridSpec(
            num_scalar_prefetch=1,
            grid=(pl.cdiv(f, bn), rows_ // tm),
            in_specs=[pl.BlockSpec((tm, d), lambda j, i, te: (i, 0)), wspec, wspec],
            out_specs=pl.BlockSpec((tm, bn), lambda j, i, te: (i, j)),
        ),
        out_shape=jax.ShapeDtypeStruct((rows_, f), BF16),
        compiler_params=_params("arbitrary", "arbitrary"),
        name="swiglu_up",
    )(tile_expert, x, wg4, wu4)


def _down_kernel(te_ref, a_ref, w_ref, o_ref):
    del te_ref
    o_ref[...] = jnp.dot(a_ref[...], w_ref[...].astype(BF16), preferred_element_type=F32)


def _expert_down(a, wd4, layer, tile_expert, tm):
    rows_, f = a.shape
    d = wd4.shape[-1]
    bn = _pick(d, 1024, 512, 256, 128)
    return pl.pallas_call(
        _down_kernel,
        grid_spec=pltpu.PrefetchScalarGridSpec(
            num_scalar_prefetch=1,
            grid=(d // bn, rows_ // tm),
            in_specs=[
                pl.BlockSpec((tm, f), lambda j, i, te: (i, 0)),
                pl.BlockSpec((None, None, f, bn), lambda j, i, te: (layer, te[i], 0, j)),
            ],
            out_specs=pl.BlockSpec((tm, bn), lambda j, i, te: (i, j)),
        ),
        out_shape=jax.ShapeDtypeStruct((rows_, d), F32),
        compiler_params=_params("arbitrary", "arbitrary"),
        name="expert_down",
    )(tile_expert, a, wd4)


def _router_kernel(lg_ref, idx_ref, w_ref, *, n_experts):
    lane = lax.broadcasted_iota(jnp.int32, lg_ref.shape, 1)
    logits = jnp.where(lane < n_experts, lg_ref[...], -jnp.inf)
    m1 = jnp.max(logits, axis=-1, keepdims=True)
    i1 = jnp.min(jnp.where(logits == m1, lane, LANES), axis=-1, keepdims=True)
    rest = jnp.where(lane == i1, -jnp.inf, logits)
    m2 = jnp.max(rest, axis=-1, keepdims=True)
    i2 = jnp.min(jnp.where(rest == m2, lane, LANES), axis=-1, keepdims=True)
    e2 = jnp.exp(m2 - m1)
    den = 1.0 + e2
    idx_ref[...] = jnp.where(lane == 0, i1, jnp.where(lane == 1, i2, 0))
    w_ref[...] = jnp.where(lane == 0, 1.0 / den, jnp.where(lane == 1, e2 / den, 0.0))


def _router(logits, n_experts):
    m = logits.shape[0]
    tm = _pick(m, 1024, 512, 256, 128)
    spec = pl.BlockSpec((tm, LANES), lambda i: (i, 0))
    return pl.pallas_call(
        functools.partial(_router_kernel, n_experts=n_experts),
        grid=(m // tm,),
        in_specs=[spec],
        out_specs=[spec, spec],
        out_shape=[jax.ShapeDtypeStruct((m, LANES), jnp.int32), jax.ShapeDtypeStruct((m, LANES), F32)],
        compiler_params=_params("arbitrary"),
        name="router",
    )(logits)


def _issue_unrolled(n, copy_of):
    def body(g, carry):
        for u in range(DMA_UNROLL):
            copy_of(g * DMA_UNROLL + u).start(priority=u % 2)
        return carry

    lax.fori_loop(0, n // DMA_UNROLL, body, 0)


def _wait_unrolled(n, copy_of):
    def body(g, carry):
        for u in range(DMA_UNROLL):
            copy_of(g * DMA_UNROLL + u).wait()
        return carry

    lax.fori_loop(0, n // DMA_UNROLL, body, 0)


def _dispatch_kernel(pad_ref, pos_ref, h_ref, xs_hbm, zbuf, sem, zsem, *, tm, n_pad):
    @pl.when(pl.program_id(0) == 0)
    def _():
        zbuf[...] = jnp.zeros(zbuf.shape, zbuf.dtype)
        for z in range(n_pad):
            zero = pltpu.make_async_copy(zbuf, xs_hbm.at[pl.ds(pl.multiple_of(pad_ref[z] * tm, tm), tm), :], zsem)
            zero.start()
            zero.wait()

    def row_copy(r, slot, t):
        return pltpu.make_async_copy(h_ref.at[pl.ds(r, 1), :], xs_hbm.at[pl.ds(t, 1), :], sem)

    for slot in range(TOP_K):
        _issue_unrolled(tm, lambda r, slot=slot: row_copy(r, slot, pos_ref[0, slot, r]))
    for slot in range(TOP_K):
        _wait_unrolled(tm, lambda r, slot=slot: row_copy(r, slot, 0))


def _dispatch_rows(h, pos, pad_tiles, n_rows, tm):
    m, d = h.shape
    n_pad = pad_tiles.shape[0]
    pos3 = jnp.transpose(pos.reshape(TOP_K, m // tm, tm), (1, 0, 2))
    return pl.pallas_call(
        functools.partial(_dispatch_kernel, tm=tm, n_pad=n_pad),
        grid_spec=pltpu.PrefetchScalarGridSpec(
            num_scalar_prefetch=1,
            grid=(m // tm,),
            in_specs=[
                pl.BlockSpec((1, TOP_K, tm), lambda i, pad: (i, 0, 0), memory_space=pltpu.SMEM),
                pl.BlockSpec((tm, d), lambda i, pad: (i, 0)),
            ],
            out_specs=pl.BlockSpec(memory_space=pl.ANY),
            scratch_shapes=[pltpu.VMEM((tm, d), F32), pltpu.SemaphoreType.DMA(()), pltpu.SemaphoreType.DMA(())],
        ),
        out_shape=jax.ShapeDtypeStruct((n_rows, d), F32),
        compiler_params=_params("arbitrary"),
        name="dispatch_rows",
    )(pad_tiles, pos3, h)


class _TailParams:
    def __init__(self, nw_ref, mod_ref, gate_row, nxt):
        self.nw = nw_ref[...]
        self.gate = mod_ref[gate_row:gate_row + 1, :]
        self.has_next = nxt is not None
        if self.has_next:
            ng_ref, nmod_ref, sc_row, sh_row = nxt
            self.ng = ng_ref[...]
            self.sc1 = 1.0 + nmod_ref[sc_row:sc_row + 1, :]
            self.sh = nmod_ref[sh_row:sh_row + 1, :]

    def apply(self, out, x):
        xn = x + self.gate * (_rms(out) * self.nw)
        h = _rms(xn) * self.ng * self.sc1 + self.sh if self.has_next else None
        return xn, h


def _mm_tail_kernel(*refs, tp, two_lhs, gate_row, rows_next, moe, ch):
    refs = list(refs)
    ap_ref = refs.pop(0)
    as_ref = refs.pop(0) if two_lhs else None
    w_ref, x_ref, nw_ref, mod_ref = refs[:4]
    refs = refs[4:]
    nxt = None
    if rows_next is not None:
        nxt = (refs[0], refs[1], rows_next[0], rows_next[1])
        refs = refs[2:]
    rt_ref = refs.pop(0) if moe else None
    xo_ref = refs.pop(0)
    h_ref = refs.pop(0) if rows_next is not None else None
    lg_ref = refs.pop(0) if moe else None
    i = pl.program_id(0)
    tail = _TailParams(nw_ref, mod_ref, gate_row, nxt)
    w = w_ref[...]
    for c in range(x_ref.shape[0] // ch):
        rs = slice(c * ch, (c + 1) * ch)
        a = ap_ref[rs, :]
        if two_lhs:
            a = jnp.where(i < tp, a, as_ref[rs, :])
        xn, h = tail.apply(jnp.dot(a, w, preferred_element_type=F32), x_ref[rs, :])
        xo_ref[rs, :] = xn
        if moe:
            h_ref[rs, :] = h
            h_hi = h.astype(BF16)
            h_lo = (h - h_hi.astype(F32)).astype(BF16)
            lg_ref[rs, :] = (jnp.dot(h_hi, rt_ref[0], preferred_element_type=F32)
                             + (jnp.dot(h_lo, rt_ref[0], preferred_element_type=F32)
                                + jnp.dot(h_hi, rt_ref[1], preferred_element_type=F32)))
        elif h is not None:
            h_ref[rs, :] = h.astype(h_ref.dtype)


def _mm_tail(a, w_bf, x, rows, mod3, layer, gate_row, ng3, nw_row, nxt, router_pad=None):
    two = isinstance(a, tuple)
    k, d = w_bf.shape
    moe = router_pad is not None
    bm = rows.tile(512, 256, 128) if k <= 2048 else rows.tile(256, 128)
    ch = min(bm, 256)
    tp = rows.n_p // bm
    mt = rows.m // bm
    if two:
        lhs_specs = [pl.BlockSpec((bm, k), lambda i: (jnp.minimum(i, tp - 1), 0)),
                     pl.BlockSpec((bm, k), lambda i: (jnp.maximum(i - tp, 0), 0))]
        lhs = list(a)
    else:
        lhs_specs = [pl.BlockSpec((bm, k), lambda i: (i, 0))]
        lhs = [a]

    def modspec(lyr):
        return pl.BlockSpec((None, 6, d), lambda i: (lyr * rows.nseg + rows.seg(i, bm), 0, 0))

    in_specs = lhs_specs + [
        _const_spec((k, d)),
        pl.BlockSpec((bm, d), lambda i: (i, 0)),
        pl.BlockSpec((None, 1, d), lambda i: (nw_row, 0, 0)),
        modspec(layer),
    ]
    args = lhs + [w_bf, x, ng3, mod3]
    out_specs = [pl.BlockSpec((bm, d), lambda i: (i, 0))]
    out_shape = [jax.ShapeDtypeStruct((rows.m, d), F32)]
    rows_next = None
    if nxt is not None:
        n_row, n_layer, sc_row, sh_row = nxt
        in_specs += [pl.BlockSpec((None, 1, d), lambda i: (n_row, 0, 0)), modspec(n_layer)]
        args += [ng3, mod3]
        rows_next = (sc_row, sh_row)
        if moe:
            in_specs.append(pl.BlockSpec((2, d, LANES), lambda i: (0, 0, 0)))
            r_hi = router_pad.astype(BF16)
            args.append(jnp.stack([r_hi, (router_pad - r_hi.astype(F32)).astype(BF16)]))
            out_specs += [pl.BlockSpec((bm, d), lambda i: (i, 0)), pl.BlockSpec((bm, LANES), lambda i: (i, 0))]
            out_shape += [jax.ShapeDtypeStruct((rows.m, d), F32), jax.ShapeDtypeStruct((rows.m, LANES), F32)]
        else:
            out_specs.append(pl.BlockSpec((bm, d), lambda i: (i, 0)))
            out_shape.append(jax.ShapeDtypeStruct((rows.m, d), BF16))
    res = pl.pallas_call(
        functools.partial(_mm_tail_kernel, tp=tp, two_lhs=two, gate_row=gate_row, rows_next=rows_next, moe=moe,
                          ch=ch),
        grid=(mt,),
        in_specs=in_specs,
        out_specs=out_specs,
        out_shape=out_shape,
        compiler_params=_params("arbitrary"),
        name="matmul_tail",
    )(*args)
    return tuple(res) + (None,) * (3 - len(res))


def _combine_kernel(*refs, tm, gate_row, rows_next):
    refs = list(refs)
    pos_ref, y_hbm, wt_ref, x_ref, nw_ref, mod_ref = refs[:6]
    refs = refs[6:]
    nxt = None
    if rows_next is not None:
        nxt = (refs[0], refs[1], rows_next[0], rows_next[1])
        refs = refs[2:]
    xo_ref = refs.pop(0)
    h_ref = refs.pop(0) if rows_next is not None else None
    buf, sem = refs

    def tok_copy(r, slot, t):
        return pltpu.make_async_copy(y_hbm.at[pl.ds(t, 1), :], buf.at[slot, pl.ds(r, 1), :], sem)

    for slot in range(TOP_K):
        _issue_unrolled(tm, lambda r, slot=slot: tok_copy(r, slot, pos_ref[0, slot, r]))
    for slot in range(TOP_K):
        _wait_unrolled(tm, lambda r, slot=slot: tok_copy(r, slot, 0))
    tail = _TailParams(nw_ref, mod_ref, gate_row, nxt)

    def chunk(r, carry):
        rs = pl.ds(pl.multiple_of(r * TAIL_ROWS, TAIL_ROWS), TAIL_ROWS)
        out = wt_ref[rs, 0:1] * buf[0, rs, :]
        for slot in range(1, TOP_K):
            out = out + wt_ref[rs, slot:slot + 1] * buf[slot, rs, :]
        xn, h = tail.apply(out, x_ref[rs, :])
        xo_ref[rs, :] = xn
        if h is not None:
            h_ref[rs, :] = h.astype(h_ref.dtype)
        return carry

    lax.fori_loop(0, tm // TAIL_ROWS, chunk, 0)


def _combine_tail(ys3, pos, wts, x, rows, mod3, layer, gate_row, ng3, nw_row, nxt, *, row0, nrows):
    d = ys3.shape[1]
    tm = _pick(math.gcd(row0, nrows) if row0 else nrows, 256, 128)
    t0 = row0 // tm
    nt = nrows // tm

    def modspec(lyr):
        return pl.BlockSpec((None, 6, d), lambda i: (lyr * rows.nseg + rows.seg(t0 + i, tm), 0, 0))

    in_specs = [
        pl.BlockSpec((1, TOP_K, tm), lambda i: (t0 + i, 0, 0), memory_space=pltpu.SMEM),
        pl.BlockSpec(memory_space=pl.ANY),
        pl.BlockSpec((tm, LANES), lambda i: (t0 + i, 0)),
        pl.BlockSpec((tm, d), lambda i: (t0 + i, 0)),
        pl.BlockSpec((None, 1, d), lambda i: (nw_row, 0, 0)),
        modspec(layer),
    ]
    pos3 = jnp.transpose(pos.reshape(TOP_K, rows.m // tm, tm), (1, 0, 2))
    args = [pos3, ys3, wts, x, ng3, mod3]
    rows_next = None
    out_specs = [pl.BlockSpec((tm, d), lambda i: (i, 0))]
    out_shape = [jax.ShapeDtypeStruct((nrows, d), F32)]
    if nxt is not None:
        n_row, n_layer, sc_row, sh_row = nxt
        in_specs += [pl.BlockSpec((None, 1, d), lambda i: (n_row, 0, 0)), modspec(n_layer)]
        args += [ng3, mod3]
        out_specs.append(pl.BlockSpec((tm, d), lambda i: (i, 0)))
        out_shape.append(jax.ShapeDtypeStruct((nrows, d), BF16))
        rows_next = (sc_row, sh_row)
    res = pl.pallas_call(
        functools.partial(_combine_kernel, tm=tm, gate_row=gate_row, rows_next=rows_next),
        grid=(nt,),
        in_specs=in_specs,
        out_specs=out_specs,
        out_shape=out_shape,
        scratch_shapes=[pltpu.VMEM((TOP_K, tm, d), F32), pltpu.SemaphoreType.DMA(())],
        compiler_params=_params("arbitrary"),
        name="combine_tail",
    )(*args)
    return (res[0], res[1]) if nxt is not None else (res[0], None)


def _routing_tables(idx, n_experts, tm):
    m = idx.shape[1]
    p = TOP_K * m + n_experts * tm
    e = idx.reshape(-1)
    onehot = (e[:, None] == jnp.arange(n_experts, dtype=jnp.int32)[None, :]).astype(jnp.int32)
    rank = jnp.sum((jnp.cumsum(onehot, axis=0) - 1) * onehot, axis=1)
    counts = jnp.sum(onehot, axis=0)
    padded = ((counts + tm - 1) // tm) * tm
    ends = jnp.cumsum(padded)
    starts = ends - padded
    pos = starts[e] + rank
    n_tiles = p // tm
    tile_start = jnp.arange(n_tiles, dtype=jnp.int32) * tm
    tile_expert = jnp.sum((tile_start[:, None] >= ends[None, :]).astype(jnp.int32), axis=1)
    pad_tiles = jnp.concatenate([jnp.maximum(ends // tm - 1, 0),
                                 jnp.arange(n_tiles - n_experts, n_tiles, dtype=jnp.int32)]).astype(jnp.int32)
    return pos.reshape(TOP_K, m), jnp.minimum(tile_expert, n_experts - 1), pad_tiles


def kernel(x_prompt, x_sample, cache_k, cache_v, c, c_ctx, ada_w, ada_b, norm_g, attn_w_in, attn_lambda, attn_subln, attn_w_out, gmlp_w_in, gmlp_ln_g, gmlp_ln_b, gmlp_w_s, gmlp_b_s, gmlp_w_out, hyena_w_in, hyena_b_in, hyena_conv_w, hyena_conv_b, hyena_ffn_w1, hyena_ffn_b1, hyena_ffn_w2, hyena_ffn_b2, hyena_ffn_w3, hyena_sin_freq, hyena_bias, hyena_w_out, ffn_w_gate, ffn_w_up, ffn_w_down, moe_router, moe_w_gate, moe_w_up, moe_w_down):
    batch, seq, d = x_prompt.shape
    dec_batch, dec_seq, _ = x_sample.shape
    depth = ada_w.shape[0]
    n_attn = attn_w_in.shape[0]
    past = cache_k.shape[2]
    n_experts = moe_router.shape[-1]
    rows = _Rows(batch * seq, dec_batch * dec_seq, dec_seq)
    n_p, n_s, m = rows.n_p, rows.n_s, rows.m
    heads = d // (2 * HEAD_DIM)

    cond8 = jnp.concatenate([c_ctx[None, :], c, jnp.zeros((8 - rows.nseg, d), F32)], axis=0)
    mod = _modulation(cond8, ada_w, ada_b)[:, :rows.nseg]
    mod3 = mod.reshape(depth * rows.nseg, 6, d)
    ng3 = norm_g.reshape(depth * 4, 1, d)

    x, h = _prenorm_join(x_prompt.reshape(n_p, d), x_sample.reshape(n_s, d), ng3, mod3, rows, 0)

    rope = _rope_tables(dec_seq)
    ck4 = cache_k.reshape(dec_batch, n_attn, past, d)
    cv4 = cache_v.reshape(dec_batch, n_attn, past, d)
    new_k = new_v = None
    y_prompt = y_sample = None

    for i in range(depth):
        kind, j = i % 3, i // 3
        if kind == 0:
            lam_init = 0.8 - 0.6 * math.exp(-0.3 * i)
            q_p = _proj(h, attn_w_in, j, row0=0, nrows=n_p, col0=0, ncols=d)
            new_k = _proj(h, attn_w_in, j, row0=0, nrows=n_p, col0=d, ncols=d, out_dtype=F32,
                          cache=(new_k, n_attn, j, batch, seq))
            new_v = _proj(h, attn_w_in, j, row0=0, nrows=n_p, col0=2 * d, ncols=d, out_dtype=F32,
                          cache=(new_v, n_attn, j, batch, seq))
            qk_s = _proj(h, attn_w_in, j, row0=n_p, nrows=n_s, col0=0, ncols=2 * d, epilogue="rope", rope=rope,
                         seq=dec_seq)
            v_s = _proj(h, attn_w_in, j, row0=n_p, nrows=n_s, col0=2 * d, ncols=d)
            o_p = _attention(q_p, 0, new_k, j, 0, new_v, j, None, attn_lambda, attn_subln, j, lam_init,
                             batch=batch, seq=seq, heads_per_step=heads, bq=seq, lag=4 * heads)
            o_s = _attention(qk_s, 0, qk_s.reshape(dec_batch, 1, dec_seq, 2 * d), 0, d,
                             v_s.reshape(dec_batch, 1, dec_seq, d), 0, (ck4, cv4), attn_lambda, attn_subln, j,
                             lam_init, batch=dec_batch, seq=dec_seq, heads_per_step=min(2, heads),
                             bq=_pick(dec_seq, 256, 128), lag=8)
            mix, w_out = (o_p, o_s), attn_w_out
        elif kind == 1:
            uv = _proj(h, gmlp_w_in, j, row0=0, nrows=m, col0=0, ncols=2 * d, epilogue="gelu")
            mix, w_out = _gmlp_gate(uv, gmlp_ln_g, gmlp_ln_b, gmlp_w_s, gmlp_b_s, j), gmlp_w_out
        else:
            p = _proj(h, hyena_w_in, j, row0=0, nrows=m, col0=0, ncols=3 * d, epilogue="bias", bias=hyena_b_in)
            outs = []
            for row0, nb, length in ((0, batch, seq), (n_p, dec_batch, dec_seq)):
                fmat = _dft_matrix(length)
                a_tab, b_tab = _hyena_filters(length, d, hyena_ffn_w1, hyena_ffn_b1, hyena_ffn_w2, hyena_ffn_b2,
                                              hyena_ffn_w3, hyena_sin_freq, j)
                kh, kn = _hyena_spectrum(fmat, a_tab, b_tab)
                outs.append(_hyena_conv(p, row0, nb, length, fmat, kh, kn, hyena_conv_w, hyena_conv_b, hyena_bias, j))
            mix, w_out = tuple(outs), hyena_w_out

        moe_layer = i % 2 == 1
        jj = i // 2
        router_pad = jnp.pad(moe_router[jj], ((0, 0), (0, LANES - n_experts))) if moe_layer else None
        x, h2, logits = _mm_tail(mix, w_out[j].astype(BF16), x, rows, mod3, i, G1, ng3, i * 4 + 1,
                                 (i * 4 + 2, i, SC2, SH2), router_pad=router_pad)

        last = i == depth - 1
        nxt = None if last else ((i + 1) * 4, i + 1, SC1, SH1)
        if not moe_layer:
            tm = rows.tile(1024, 512, 256, 128)
            a = _swiglu_up(h2, ffn_w_gate[:, None], ffn_w_up[:, None], jj, jnp.zeros((m // tm,), jnp.int32), tm)
            x, h, _ = _mm_tail(a, ffn_w_down[jj].astype(BF16), x, rows, mod3, i, G2, ng3, i * 4 + 3, nxt)
        else:
            tm = rows.tile(512, 256, 128)
            idx, wts = _router(logits, n_experts)
            pos, tile_expert, pad_tiles = _routing_tables(jnp.transpose(idx[:, :TOP_K]), n_experts, tm)
            xs = _dispatch_rows(h2, pos, pad_tiles, TOP_K * m + n_experts * tm, tm)
            a = _swiglu_up(xs, moe_w_gate, moe_w_up, jj, tile_expert, tm, bn=min(768, moe_w_gate.shape[-1]))
            ys = _expert_down(a, moe_w_down, jj, tile_expert, tm)
            tail_args = (ys, pos, wts, x, rows, mod3, i, G2, ng3, i * 4 + 3)
            if last:
                y_prompt, _ = _combine_tail(*tail_args, None, row0=0, nrows=n_p)
                y_sample, _ = _combine_tail(*tail_args, None, row0=n_p, nrows=n_s)
            else:
                x, h = _combine_tail(*tail_args, nxt, row0=0, nrows=m)

    if y_prompt is None:
        y_prompt, y_sample = x[:n_p], x[n_p:]
    y_prompt = y_prompt.reshape(batch, seq, d)
    y_sample = y_sample.reshape(dec_batch, dec_seq, d)
    hk = cache_k.shape[3]
    hv = cache_v.shape[3]
    return (y_prompt, y_sample, new_k.reshape(batch, n_attn, seq, hk, d // hk),
            new_v.reshape(batch, n_attn, seq, hv, d // hv))
```

```python
import functools
import math

import jax
import jax.numpy as jnp
from jax import lax
from jax.experimental import pallas as pl
from jax.experimental.pallas import tpu as pltpu

F32 = jnp.float32
BF16 = jnp.bfloat16

EPS = 1e-6
GRID_W = 64
ROPE_THETA = 10000.0
CHUNK = 128
HYENA_BANDS = 16
HYENA_FAST_DECAY = 0.3
HYENA_SLOW_DECAY = 1.5
HYENA_TARGET = 0.01
HEAD_DIM = 64
TOP_K = 2

V7X_VMEM_BYTES = 64 * 1024 * 1024
VMEM_LIMIT = V7X_VMEM_BYTES - 8 * 1024 * 1024
LANES = 128
TAIL_ROWS = 128
FREQ_ROWS = 512
DMA_UNROLL = 8

SH1, SC1, G1, SH2, SC2, G2 = range(6)


def _pick(n, *cands):
    for c in cands:
        if n % c == 0:
            return c
    return n


def _params(*sem):
    return pltpu.CompilerParams(dimension_semantics=sem, vmem_limit_bytes=VMEM_LIMIT)


def _const_spec(shape):
    nd = len(shape)
    return pl.BlockSpec(shape, lambda *_: (0,) * nd, pipeline_mode=pl.Buffered(1))


def _rms(x):
    return x * lax.rsqrt(jnp.mean(x * x, axis=-1, keepdims=True) + EPS)


class _Rows:
    def __init__(self, n_p, n_s, dec_seq):
        self.n_p, self.n_s, self.dec_seq = n_p, n_s, dec_seq
        self.m = n_p + n_s
        self.nseg = 1 + n_s // dec_seq

    def seg(self, i, bm):
        r = i * bm
        return jnp.where(r < self.n_p, 0, 1 + (r - self.n_p) // self.dec_seq)

    def tile(self, *cands):
        return _pick(math.gcd(self.n_p, self.dec_seq), *cands)


def _mod_kernel(cond_ref, w_ref, b_ref, o_ref):
    s = cond_ref[...]
    s = s * jax.nn.sigmoid(s)
    o_ref[...] = jnp.dot(s.astype(BF16), w_ref[...].astype(BF16), preferred_element_type=F32) + b_ref[...]


def _modulation(cond8, ada_w, ada_b):
    depth, d, n6 = ada_w.shape
    bn = _pick(n6, 1024, 512, 256, 128)
    return pl.pallas_call(
        _mod_kernel,
        grid=(depth, n6 // bn),
        in_specs=[
            pl.BlockSpec((8, d), lambda l, j: (0, 0)),
            pl.BlockSpec((None, d, bn), lambda l, j: (l, 0, j)),
            pl.BlockSpec((None, 1, bn), lambda l, j: (l, 0, j)),
        ],
        out_specs=pl.BlockSpec((None, 8, bn), lambda l, j: (l, 0, j)),
        out_shape=jax.ShapeDtypeStruct((depth, 8, n6), F32),
        compiler_params=_params("arbitrary", "arbitrary"),
        name="modulation",
    )(cond8, ada_w, ada_b.reshape(depth, 1, n6))


def _pre_kernel(xp_ref, xs_ref, g_ref, mod_ref, x_ref, h_ref, *, tp):
    i = pl.program_id(0)

    def emit(src_ref):
        x = src_ref[...]
        x_ref[...] = x
        h = _rms(x) * g_ref[...] * (1.0 + mod_ref[SC1:SC1 + 1, :]) + mod_ref[SH1:SH1 + 1, :]
        h_ref[...] = h.astype(h_ref.dtype)

    @pl.when(i < tp)
    def _():
        emit(xp_ref)

    @pl.when(i >= tp)
    def _():
        emit(xs_ref)


def _prenorm_join(xp, xs, ng3, mod3, rows, layer):
    d = xp.shape[-1]
    bm = rows.tile(512, 256, 128)
    tp = rows.n_p // bm
    ts = rows.n_s // bm
    return pl.pallas_call(
        functools.partial(_pre_kernel, tp=tp),
        grid=(tp + ts,),
        in_specs=[
            pl.BlockSpec((bm, d), lambda i: (jnp.minimum(i, tp - 1), 0)),
            pl.BlockSpec((bm, d), lambda i: (jnp.maximum(i - tp, 0), 0)),
            pl.BlockSpec((None, 1, d), lambda i: (layer * 4, 0, 0)),
            pl.BlockSpec((None, 6, d), lambda i: (layer * rows.nseg + rows.seg(i, bm), 0, 0)),
        ],
        out_specs=[pl.BlockSpec((bm, d), lambda i: (i, 0)), pl.BlockSpec((bm, d), lambda i: (i, 0))],
        out_shape=[jax.ShapeDtypeStruct((rows.m, d), F32), jax.ShapeDtypeStruct((rows.m, d), BF16)],
        compiler_params=_params("arbitrary"),
        name="prenorm_join",
    )(xp, xs, ng3, mod3)


def _proj_kernel(*refs, epilogue, half, cache_rows, cache_slot=None):
    if epilogue == "bias":
        x_ref, w_ref, b_ref, o_ref = refs
    elif epilogue == "rope":
        x_ref, w_ref, cos_ref, sin_ref, o_ref = refs
    else:
        x_ref, w_ref, o_ref = refs
    acc = jnp.dot(x_ref[...], w_ref[...].astype(BF16), preferred_element_type=F32)
    if epilogue == "bias":
        acc = acc + b_ref[...]
    elif epilogue == "gelu":
        acc = 0.5 * acc * (1.0 + lax.erf(acc * (2.0 ** -0.5)))
    if epilogue == "rope":
        cos_t = cos_ref[...]
        sin_t = sin_ref[...]
        lane = lax.broadcasted_iota(jnp.int32, cos_t.shape, 1)
        first = (lane % (2 * half)) < half
        for cblk in range(acc.shape[1] // LANES):
            blk = acc[:, cblk * LANES:(cblk + 1) * LANES]
            partner = jnp.where(first, pltpu.roll(blk, LANES - half, 1), pltpu.roll(blk, half, 1))
            o_ref[:, cblk * LANES:(cblk + 1) * LANES] = (blk * cos_t + partner * sin_t).astype(o_ref.dtype)
    elif cache_slot is not None:
        o_ref[...] = jnp.zeros(o_ref.shape, o_ref.dtype)
        o_ref[:, cache_slot] = acc.reshape(o_ref.shape[0], *o_ref.shape[2:]).astype(o_ref.dtype)
    elif cache_rows:
        o_ref[...] = acc.reshape(o_ref.shape).astype(o_ref.dtype)
    else:
        o_ref[...] = acc.astype(o_ref.dtype)


def _proj(h, w, layer, *, row0, nrows, col0, ncols, out_dtype=BF16, epilogue="none", bias=None,
          rope=None, seq=None, cache=None):
    k = h.shape[1]
    bm = _pick(math.gcd(math.gcd(row0, nrows), seq or 0), 1024, 512, 256, 128)
    bn = _pick(math.gcd(col0, ncols) if col0 else ncols, 1024, 512, 256, 128)
    r0, c0 = row0 // bm, col0 // bn
    in_specs = [
        pl.BlockSpec((bm, k), lambda j, i: (r0 + i, 0)),
        pl.BlockSpec((None, k, bn), lambda j, i: (layer, 0, c0 + j)),
    ]
    args = [h, w]
    half = 0
    if epilogue == "bias":
        in_specs.append(pl.BlockSpec((None, 1, bn), lambda j, i: (layer, 0, c0 + j)))
        args.append(bias.reshape(bias.shape[0], 1, bias.shape[1]))
    elif epilogue == "rope":
        cos_t, sin_t, half = rope
        per_seq = seq // bm
        in_specs += [pl.BlockSpec((bm, LANES), lambda j, i: (i % per_seq, 0))] * 2
        args += [cos_t, sin_t]
    aliases = {}
    first_slot = None
    if cache is None:
        out_spec = pl.BlockSpec((bm, bn), lambda j, i: (i, j))
        out_shape = jax.ShapeDtypeStruct((nrows, ncols), out_dtype)
    else:
        arr, n_slots, slot, batch, cseq = cache
        bb = bm // cseq
        out_shape = jax.ShapeDtypeStruct((batch, n_slots, cseq, ncols), out_dtype)
        if arr is None:
            out_spec = pl.BlockSpec((bb, n_slots, cseq, bn), lambda j, i: (i, 0, 0, j))
            first_slot = slot
        else:
            out_spec = pl.BlockSpec((bb, None, cseq, bn), lambda j, i: (i, slot, 0, j))
            in_specs.append(pl.BlockSpec(memory_space=pl.ANY))
            args.append(arr)
            aliases = {len(args) - 1: 0}

    def body(*refs):
        if aliases:
            refs = refs[:len(args) - 1] + refs[len(args):]
        _proj_kernel(*refs, epilogue=epilogue, half=half, cache_rows=cache is not None, cache_slot=first_slot)

    return pl.pallas_call(
        body,
        grid=(ncols // bn, nrows // bm),
        in_specs=in_specs,
        out_specs=out_spec,
        out_shape=out_shape,
        input_output_aliases=aliases,
        compiler_params=_params("arbitrary", "arbitrary"),
        name="proj_" + epilogue,
    )(*args)


def _rope_tables(dec_seq):
    rows = dec_seq // GRID_W
    row = jnp.repeat(jnp.arange(rows), GRID_W).astype(F32)
    col = jnp.tile(jnp.arange(GRID_W), rows).astype(F32)
    half = HEAD_DIM // 4
    inv = ROPE_THETA ** (-jnp.arange(half, dtype=F32) / half)
    ar = row[:, None] * inv[None, :]
    ac = col[:, None] * inv[None, :]
    cos64 = jnp.concatenate([jnp.cos(ar), jnp.cos(ar), jnp.cos(ac), jnp.cos(ac)], axis=-1)
    sin64 = jnp.concatenate([-jnp.sin(ar), jnp.sin(ar), -jnp.sin(ac), jnp.sin(ac)], axis=-1)
    reps = LANES // HEAD_DIM
    return jnp.tile(cos64, (1, reps)), jnp.tile(sin64, (1, reps)), half


def _attn_kernel(*refs, heads, has_cache, lam_init, lag):
    if has_cache:
        lam_ref, sub_ref, q_ref, k_ref, v_ref, kc_ref, vc_ref, o_ref = refs
    else:
        lam_ref, sub_ref, q_ref, k_ref, v_ref, o_ref = refs
    p = lam_ref[...]
    lam = (jnp.exp(jnp.sum(p[0:1] * p[1:2], axis=-1, keepdims=True))
           - jnp.exp(jnp.sum(p[2:3] * p[3:4], axis=-1, keepdims=True)) + lam_init)
    scale = HEAD_DIM ** -0.5
    assert math.log2(HEAD_DIM) % 2 == 0, "folding the score scale into bf16 queries needs a power-of-two scale"
    nt = (((1,), (1,)), ((), ()))
    head_slices = [slice(h * 2 * HEAD_DIM, (h + 1) * 2 * HEAD_DIM) for h in range(heads)]
    n_units = 2 * heads

    def score_stage(u):
        sl, ms = head_slices[u // 2], slice((u % 2) * HEAD_DIM, (u % 2 + 1) * HEAD_DIM)
        q = q_ref[:, sl][:, ms] * scale
        s = lax.dot_general(q, k_ref[:, sl][:, ms].astype(BF16), nt, preferred_element_type=F32)
        sc = None
        if has_cache:
            sc = lax.dot_general(q, kc_ref[:, sl][:, ms].astype(BF16), nt, preferred_element_type=F32)
        return s, sc

    def exp_stage(s, sc):
        mx = jnp.max(s, axis=-1, keepdims=True)
        if has_cache:
            mx = jnp.maximum(mx, jnp.max(sc, axis=-1, keepdims=True))
        return jnp.exp((s - mx).astype(BF16)), jnp.exp((sc - mx).astype(BF16)) if has_cache else None

    def value_stage(u, e, ec):
        sl = head_slices[u // 2]
        v = v_ref[:, sl].astype(BF16)
        acc = jnp.dot(e, jnp.concatenate([v, jnp.ones_like(v)], axis=1), preferred_element_type=F32)
        if has_cache:
            vc = vc_ref[:, sl].astype(BF16)
            acc = acc + jnp.dot(ec, jnp.concatenate([vc, jnp.ones_like(vc)], axis=1), preferred_element_type=F32)
        vd = acc.shape[1] // 2
        return acc[:, :vd] / acc[:, vd:]

    half = lag // 2
    scores, weights, maps = {}, {}, {}
    for t in range(n_units + lag):
        if 0 <= t - lag:
            maps[t - lag] = value_stage(t - lag, *weights.pop(t - lag))
        if 0 <= t - half < n_units:
            weights[t - half] = exp_stage(*scores.pop(t - half))
        if t < n_units:
            scores[t] = score_stage(t)
    for h, sl in enumerate(head_slices):
        o = maps[2 * h] - lam * maps[2 * h + 1]
        o = _rms(o) * sub_ref[...] * (1.0 - lam_init)
        o_ref[:, sl] = o.astype(o_ref.dtype)


def _attention(q, q_col0, k4, k_slot, k_col0, v4, v_slot, cache4, lam, subln, layer_j, lam_init, *, batch, seq,
               heads_per_step, bq, lag):
    d = v4.shape[-1]
    hw = heads_per_step * 2 * HEAD_DIM
    nk = k4.shape[2]
    per_seq = seq // bq
    qc0, kc0 = q_col0 // hw, k_col0 // hw
    in_specs = [
        pl.BlockSpec((None, 4, HEAD_DIM), lambda b, h, i: (layer_j, 0, 0)),
        pl.BlockSpec((None, 1, 2 * HEAD_DIM), lambda b, h, i: (layer_j, 0, 0)),
        pl.BlockSpec((bq, hw), lambda b, h, i: (b * per_seq + i, qc0 + h)),
        pl.BlockSpec((None, None, nk, hw), lambda b, h, i: (b, k_slot, 0, kc0 + h)),
        pl.BlockSpec((None, None, nk, hw), lambda b, h, i: (b, v_slot, 0, h)),
    ]
    args = [lam, subln.reshape(subln.shape[0], 1, subln.shape[1]), q, k4, v4]
    if cache4 is not None:
        ck, cv = cache4
        nc = ck.shape[2]
        in_specs += [pl.BlockSpec((None, None, nc, hw), lambda b, h, i: (b, layer_j, 0, h))] * 2
        args += [ck, cv]
    return pl.pallas_call(
        functools.partial(_attn_kernel, heads=heads_per_step, has_cache=cache4 is not None, lam_init=lam_init,
                          lag=lag),
        grid=(batch, d // hw, per_seq),
        in_specs=in_specs,
        out_specs=pl.BlockSpec((bq, hw), lambda b, h, i: (b * per_seq + i, h)),
        out_shape=jax.ShapeDtypeStruct((batch * seq, d), BF16),
        compiler_params=_params("arbitrary", "arbitrary", "arbitrary"),
        name="diff_attention",
    )(*args)


def _gmlp_kernel(u_ref, v_ref, g_ref, b_ref, ws_ref, bs_ref, o_ref, *, chunks, groups, gd):
    v = v_ref[...].astype(F32)
    mu = jnp.mean(v, axis=-1, keepdims=True)
    vc = v - mu
    var = jnp.mean(vc * vc, axis=-1, keepdims=True)
    vn = (vc * lax.rsqrt(var + EPS) * g_ref[...] + b_ref[...]).astype(BF16)
    for g in range(groups):
        w = ws_ref[g].astype(BF16)
        cs = slice(g * gd, (g + 1) * gd)
        for c in range(chunks):
            rs = slice(c * CHUNK, (c + 1) * CHUNK)
            vm = jnp.dot(w, vn[rs, cs], preferred_element_type=F32) + bs_ref[:, cs]
            o_ref[rs, cs] = (u_ref[rs, cs].astype(F32) * vm).astype(o_ref.dtype)


def _gmlp_gate(uv, ln_g, ln_b, w_s, b_s, layer_j):
    m, d2 = uv.shape
    d = d2 // 2
    groups = w_s.shape[1]
    gd = d // groups
    tm = _pick(m, 512, 256, 128)
    bs_full = jnp.repeat(jnp.transpose(b_s[layer_j]), gd, axis=1)
    return pl.pallas_call(
        functools.partial(_gmlp_kernel, chunks=tm // CHUNK, groups=groups, gd=gd),
        grid=(m // tm,),
        in_specs=[
            pl.BlockSpec((tm, d), lambda i: (i, 0)),
            pl.BlockSpec((tm, d), lambda i: (i, 1)),
            pl.BlockSpec((None, 1, d), lambda i: (layer_j, 0, 0)),
            pl.BlockSpec((None, 1, d), lambda i: (layer_j, 0, 0)),
            pl.BlockSpec((None, groups, CHUNK, CHUNK), lambda i: (layer_j, 0, 0, 0)),
            pl.BlockSpec((CHUNK, d), lambda i: (0, 0)),
        ],
        out_specs=pl.BlockSpec((tm, d), lambda i: (i, 0)),
        out_shape=jax.ShapeDtypeStruct((m, d), BF16),
        compiler_params=_params("arbitrary"),
        name="gmlp_gate",
    )(uv, uv, ln_g.reshape(ln_g.shape[0], 1, d), ln_b.reshape(ln_b.shape[0], 1, d), w_s, bs_full)


def _dft_matrix(length):
    n = 2 * length
    split = _pick(length, 32, 16, 8)
    f = jnp.arange(length, dtype=jnp.int32)[:, None]

    def cos_sin(t):
        ang = ((f * t[None, :]) % n).astype(F32) * (2.0 * math.pi / n)
        return jnp.cos(ang), jnp.sin(ang)

    ca, sa = cos_sin(jnp.arange(length // split, dtype=jnp.int32) * split)
    cb, sb = cos_sin(jnp.arange(split, dtype=jnp.int32))
    cos_ft = (ca[:, :, None] * cb[:, None, :] - sa[:, :, None] * sb[:, None, :]).reshape(length, length)
    sin_ft = (sa[:, :, None] * cb[:, None, :] + ca[:, :, None] * sb[:, None, :]).reshape(length, length)
    return jnp.concatenate([cos_ft, -sin_ft], axis=0).astype(BF16)


def _filter_features(length):
    pos = jnp.arange(length, dtype=F32)
    t = jnp.linspace(0.0, 1.0, length, dtype=F32)[:, None]
    bands = jnp.linspace(1e-4, HYENA_BANDS - 1, HYENA_BANDS, dtype=F32)
    ang = (2.0 * math.pi / length) * pos[:, None] * bands[None, :]
    z = jnp.concatenate([t, jnp.cos(ang), -jnp.sin(ang)], axis=-1)
    return jnp.pad(z, ((0, 0), (0, LANES - z.shape[1])))


def _filt_kernel(z_ref, w1_ref, b1_ref, w2_ref, b2_ref, fr_ref, w00, w01, w10, w11, dl_ref, a_ref, b_ref):
    hi = lax.Precision.HIGHEST
    z = z_ref[...]
    h = jnp.sin(fr_ref[0:1, :] * (jnp.dot(z, w1_ref[...], precision=hi, preferred_element_type=F32) + b1_ref[...]))
    h = jnp.sin(fr_ref[1:2, :] * (jnp.dot(h, w2_ref[...], precision=hi, preferred_element_type=F32) + b2_ref[...]))
    hb = h.astype(BF16)
    decay = jnp.exp(-z[:, 0:1] * dl_ref[...])
    row = lax.broadcasted_iota(jnp.int32, (z.shape[0], 1), 0)
    for o, (wf, wb) in enumerate(((w00, w01), (w10, w11))):
        fwd = jnp.dot(hb, wf[...].astype(BF16), preferred_element_type=F32) * decay
        bwd = jnp.dot(hb, wb[...].astype(BF16), preferred_element_type=F32) * decay
        bwd = jnp.where(row == 0, 0.0, bwd)
        norm = (jnp.sum(jnp.abs(fwd), axis=0, keepdims=True) + jnp.sum(jnp.abs(bwd), axis=0, keepdims=True) + EPS)
        inv = 1.0 / norm
        a_ref[o] = ((fwd + bwd) * inv).astype(a_ref.dtype)
        b_ref[o] = ((fwd - bwd) * inv).astype(b_ref.dtype)


def _hyena_filters(length, d, w1, b1, w2, b2, w3, freq, layer_j):
    fh = w2.shape[-1]
    emb = w1.shape[1]
    bd = _pick(d, 512, 256, 128)
    nb = d // bd
    zfeat = _filter_features(length)
    w1p = jnp.pad(w1[layer_j], ((0, LANES - emb), (0, 0)))
    max_decay = math.log(HYENA_TARGET) / HYENA_FAST_DECAY
    min_decay = math.log(HYENA_TARGET) / HYENA_SLOW_DECAY
    deltas = jnp.abs(jnp.linspace(min_decay, max_decay, d, dtype=F32))[None, :]
    w3_specs = [pl.BlockSpec((None, fh, bd), functools.partial(lambda c, g: (layer_j, 0, g * nb + c), g=g))
                for g in range(4)]
    out_spec = pl.BlockSpec((2, length, bd), lambda c: (0, 0, c))
    return pl.pallas_call(
        _filt_kernel,
        grid=(nb,),
        in_specs=[
            pl.BlockSpec((length, LANES), lambda c: (0, 0)),
            pl.BlockSpec((LANES, fh), lambda c: (0, 0)),
            pl.BlockSpec((None, 1, fh), lambda c: (layer_j, 0, 0)),
            pl.BlockSpec((None, fh, fh), lambda c: (layer_j, 0, 0)),
            pl.BlockSpec((None, 1, fh), lambda c: (layer_j, 0, 0)),
            pl.BlockSpec((None, 2, fh), lambda c: (layer_j, 0, 0)),
            *w3_specs,
            pl.BlockSpec((1, bd), lambda c: (0, c)),
        ],
        out_specs=[out_spec, out_spec],
        out_shape=[jax.ShapeDtypeStruct((2, length, d), BF16)] * 2,
        compiler_params=_params("arbitrary"),
        name="hyena_filters",
    )(zfeat, w1p, b1.reshape(b1.shape[0], 1, fh), w2, b2.reshape(b2.shape[0], 1, fh), freq, w3, w3, w3, w3, deltas)


def _spectrum_kernel(f_ref, a_ref, b_ref, kh_ref, kn_ref, *, length):
    n = 2 * length
    row = lax.broadcasted_iota(jnp.int32, (length, 1), 0)
    wn = jnp.where(row == 0, 1.0 / n, 2.0 / n)
    sgn = jnp.where(row % 2 == 0, 1.0, -1.0)
    a = a_ref[...]
    kh_ref[0:length, :] = jnp.dot(f_ref[0:length, :], a, preferred_element_type=F32) * wn
    kh_ref[length:n, :] = jnp.dot(f_ref[length:n, :], b_ref[...], preferred_element_type=F32) * wn
    nyq = jnp.sum(a.astype(F32) * sgn, axis=0, keepdims=True) * (1.0 / n)
    kn_ref[...] = jnp.broadcast_to(nyq, kn_ref.shape)


def _hyena_spectrum(fmat, a_tab, b_tab):
    _, length, d = a_tab.shape
    bd = _pick(d, 256, 128)
    tab_spec = pl.BlockSpec((None, length, bd), lambda o, c: (o, 0, c))
    return pl.pallas_call(
        functools.partial(_spectrum_kernel, length=length),
        grid=(2, d // bd),
        in_specs=[_const_spec((2 * length, length)), tab_spec, tab_spec],
        out_specs=[pl.BlockSpec((None, 2 * length, bd), lambda o, c: (o, 0, c)),
                   pl.BlockSpec((None, 8, bd), lambda o, c: (o, 0, c))],
        out_shape=[jax.ShapeDtypeStruct((2, 2 * length, d), F32), jax.ShapeDtypeStruct((2, 8, d), F32)],
        compiler_params=_params("arbitrary", "arbitrary"),
        name="hyena_spectrum",
    )(fmat, a_tab, b_tab)


def _hconv_kernel(f_ref, kh_ref, kn_ref, pv_ref, p1_ref, p2_ref, cwv, cw1, cw2, cbv, cb1, cb2, hb_ref, o_ref, y_scr,
                  *, length):
    n = 2 * length
    row = lax.broadcasted_iota(jnp.int32, (length, 1), 0)
    sgn = jnp.where(row % 2 == 0, 1.0, -1.0)

    seqs = [slice(b * length, (b + 1) * length) for b in range(o_ref.shape[0] // length)]

    def short_conv(p_ref, cw, cb, seq):
        p = p_ref[seq, :].astype(F32)
        prev = jnp.where(row == 0, 0.0, pltpu.roll(p, 1, 0))
        nxt = jnp.where(row == length - 1, 0.0, pltpu.roll(p, length - 1, 0))
        return cb[...] + (prev * cw[0:1, :] + p * cw[1:2, :] + nxt * cw[2:3, :])

    fb = min(length, FREQ_ROWS)

    def to_spectrum(z, o, b):
        zb = z.astype(BF16)

        def freq_block(i, carry):
            re = pl.ds(pl.multiple_of(i * fb, fb), fb)
            im = pl.ds(pl.multiple_of(length + i * fb, fb), fb)
            zr = jnp.dot(f_ref[re, :], zb, preferred_element_type=F32)
            zi = jnp.dot(f_ref[im, :], zb, preferred_element_type=F32)
            kr, ki = kh_ref[o, re, :], kh_ref[o, im, :]
            y_scr[pl.ds(pl.multiple_of(b * n + i * fb, fb), fb), :] = (zr * kr - zi * ki).astype(BF16)
            y_scr[pl.ds(pl.multiple_of(b * n + length + i * fb, fb), fb), :] = (zr * ki + zi * kr).astype(BF16)
            return carry

        lax.fori_loop(0, length // fb, freq_block, 0)
        return jnp.sum(zb.astype(F32) * sgn, axis=0, keepdims=True) * kn_ref[o, 0:1, :]

    def from_spectrum(z, nyq, o, b):
        y = (jnp.dot(f_ref[0:length, :], y_scr[b * n:b * n + length, :], preferred_element_type=F32)
             + jnp.dot(f_ref[length:n, :], y_scr[b * n + length:(b + 1) * n, :], preferred_element_type=F32)
             + sgn * nyq)
        return y + z * hb_ref[o:o + 1, :]

    z = [short_conv(pv_ref, cwv, cbv, seq) for seq in seqs]
    for o, (p_ref, cw, cb) in enumerate(((p1_ref, cw1, cb1), (p2_ref, cw2, cb2))):
        nyq = [to_spectrum(z[b], o, b) for b in range(len(seqs))]
        gate = [short_conv(p_ref, cw, cb, seq) for seq in seqs]
        z = [gate[b] * from_spectrum(z[b], nyq[b], o, b) for b in range(len(seqs))]
    for b, seq in enumerate(seqs):
        o_ref[seq, :] = z[b].astype(o_ref.dtype)


def _hyena_conv(p, row0, batch, length, fmat, kh, kn, conv_w, conv_b, hbias, layer_j):
    d = p.shape[1] // 3
    bd = _pick(d, 256, 128) if length > 512 else _pick(d, 512, 256, 128)
    nb = d // bd
    spb = 2 if length <= 512 and batch % 2 == 0 and (row0 // length) % 2 == 0 else 1
    r0 = row0 // (spb * length)

    def pspec(g):
        return pl.BlockSpec((spb * length, bd), lambda c, b: (r0 + b, g * nb + c))

    def wspec(g, rows_):
        return pl.BlockSpec((None, rows_, bd), lambda c, b: (layer_j, 0, g * nb + c))

    return pl.pallas_call(
        functools.partial(_hconv_kernel, length=length),
        grid=(nb, batch // spb),
        in_specs=[
            _const_spec((2 * length, length)),
            pl.BlockSpec((2, 2 * length, bd), lambda c, b: (0, 0, c), pipeline_mode=pl.Buffered(1)),
            pl.BlockSpec((2, 8, bd), lambda c, b: (0, 0, c)),
            pspec(0), pspec(1), pspec(2),
            wspec(0, 3), wspec(1, 3), wspec(2, 3),
            wspec(0, 1), wspec(1, 1), wspec(2, 1),
            pl.BlockSpec((None, 2, bd), lambda c, b: (layer_j, 0, c)),
        ],
        out_specs=pl.BlockSpec((spb * length, bd), lambda c, b: (b, c)),
        out_shape=jax.ShapeDtypeStruct((batch * length, d), BF16),
        scratch_shapes=[pltpu.VMEM((spb * 2 * length, bd), BF16)],
        compiler_params=_params("arbitrary", "arbitrary"),
        name="hyena_conv",
    )(fmat, kh, kn, p, p, p, conv_w, conv_w, conv_w, conv_b.reshape(conv_b.shape[0], 1, -1),
      conv_b.reshape(conv_b.shape[0], 1, -1), conv_b.reshape(conv_b.shape[0], 1, -1), hbias)


def _up_kernel(te_ref, x_ref, wg_ref, wu_ref, o_ref):
    del te_ref
    x = x_ref[...].astype(BF16)
    g = jnp.dot(x, wg_ref[...].astype(BF16), preferred_element_type=F32)
    u = jnp.dot(x, wu_ref[...].astype(BF16), preferred_element_type=F32)
    o_ref[...] = (g * jax.nn.sigmoid(g) * u).astype(o_ref.dtype)


def _swiglu_up(x, wg4, wu4, layer, tile_expert, tm, bn=None):
    rows_, d = x.shape
    f = wg4.shape[-1]
    bn = bn or _pick(f, 512, 256, 128)
    wspec = pl.BlockSpec((None, None, d, bn), lambda j, i, te: (layer, te[i], 0, j))
    return pl.pallas_call(
        _up_kernel,
        grid_spec=pltpu.PrefetchScalarGridSpec(
            num_scalar_prefetch=1,
            grid=(pl.cdiv(f, bn), rows_ // tm),
            in_specs=[pl.BlockSpec((tm, d), lambda j, i, te: (i, 0)), wspec, wspec],
            out_specs=pl.BlockSpec((tm, bn), lambda j, i, te: (i, j)),
        ),
        out_shape=jax.ShapeDtypeStruct((rows_, f), BF16),
        compiler_params=_params("arbitrary", "arbitrary"),
        name="swiglu_up",
    )(tile_expert, x, wg4, wu4)


def _down_kernel(te_ref, a_ref, w_ref, o_ref):
    del te_ref
    o_ref[...] = jnp.dot(a_ref[...], w_ref[...].astype(BF16), preferred_element_type=F32)


def _expert_down(a, wd4, layer, tile_expert, tm):
    rows_, f = a.shape
    d = wd4.shape[-1]
    bn = _pick(d, 1024, 512, 256, 128)
    return pl.pallas_call(
        _down_kernel,
        grid_spec=pltpu.PrefetchScalarGridSpec(
            num_scalar_prefetch=1,
            grid=(d // bn, rows_ // tm),
            in_specs=[
                pl.BlockSpec((tm, f), lambda j, i, te: (i, 0)),
                pl.BlockSpec((None, None, f, bn), lambda j, i, te: (layer, te[i], 0, j)),
            ],
            out_specs=pl.BlockSpec((tm, bn), lambda j, i, te: (i, j)),
        ),
        out_shape=jax.ShapeDtypeStruct((rows_, d), F32),
        compiler_params=_params("arbitrary", "arbitrary"),
        name="expert_down",
    )(tile_expert, a, wd4)


def _router_kernel(lg_ref, idx_ref, w_ref, *, n_experts):
    lane = lax.broadcasted_iota(jnp.int32, lg_ref.shape, 1)
    logits = jnp.where(lane < n_experts, lg_ref[...], -jnp.inf)
    m1 = jnp.max(logits, axis=-1, keepdims=True)
    i1 = jnp.min(jnp.where(logits == m1, lane, LANES), axis=-1, keepdims=True)
    rest = jnp.where(lane == i1, -jnp.inf, logits)
    m2 = jnp.max(rest, axis=-1, keepdims=True)
    i2 = jnp.min(jnp.where(rest == m2, lane, LANES), axis=-1, keepdims=True)
    e2 = jnp.exp(m2 - m1)
    den = 1.0 + e2
    idx_ref[...] = jnp.where(lane == 0, i1, jnp.where(lane == 1, i2, 0))
    w_ref[...] = jnp.where(lane == 0, 1.0 / den, jnp.where(lane == 1, e2 / den, 0.0))


def _router(logits, n_experts):
    m = logits.shape[0]
    tm = _pick(m, 1024, 512, 256, 128)
    spec = pl.BlockSpec((tm, LANES), lambda i: (i, 0))
    return pl.pallas_call(
        functools.partial(_router_kernel, n_experts=n_experts),
        grid=(m // tm,),
        in_specs=[spec],
        out_specs=[spec, spec],
        out_shape=[jax.ShapeDtypeStruct((m, LANES), jnp.int32), jax.ShapeDtypeStruct((m, LANES), F32)],
        compiler_params=_params("arbitrary"),
        name="router",
    )(logits)


def _issue_unrolled(n, copy_of):
    def body(g, carry):
        for u in range(DMA_UNROLL):
            copy_of(g * DMA_UNROLL + u).start(priority=u % 2)
        return carry

    lax.fori_loop(0, n // DMA_UNROLL, body, 0)


def _wait_unrolled(n, copy_of):
    def body(g, carry):
        for u in range(DMA_UNROLL):
            copy_of(g * DMA_UNROLL + u).wait()
        return carry

    lax.fori_loop(0, n // DMA_UNROLL, body, 0)


def _dispatch_kernel(pad_ref, pos_ref, h_ref, xs_hbm, zbuf, sem, zsem, *, tm, n_pad):
    @pl.when(pl.program_id(0) == 0)
    def _():
        zbuf[...] = jnp.zeros(zbuf.shape, zbuf.dtype)
        for z in range(n_pad):
            zero = pltpu.make_async_copy(zbuf, xs_hbm.at[pl.ds(pl.multiple_of(pad_ref[z] * tm, tm), tm), :], zsem)
            zero.start()
            zero.wait()

    def row_copy(r, slot, t):
        return pltpu.make_async_copy(h_ref.at[pl.ds(r, 1), :], xs_hbm.at[pl.ds(t, 1), :], sem)

    for slot in range(TOP_K):
        _issue_unrolled(tm, lambda r, slot=slot: row_copy(r, slot, pos_ref[0, slot, r]))
    for slot in range(TOP_K):
        _wait_unrolled(tm, lambda r, slot=slot: row_copy(r, slot, 0))


def _dispatch_rows(h, pos, pad_tiles, n_rows, tm):
    m, d = h.shape
    n_pad = pad_tiles.shape[0]
    pos3 = jnp.transpose(pos.reshape(TOP_K, m // tm, tm), (1, 0, 2))
    return pl.pallas_call(
        functools.partial(_dispatch_kernel, tm=tm, n_pad=n_pad),
        grid_spec=pltpu.PrefetchScalarGridSpec(
            num_scalar_prefetch=1,
            grid=(m // tm,),
            in_specs=[
                pl.BlockSpec((1, TOP_K, tm), lambda i, pad: (i, 0, 0), memory_space=pltpu.SMEM),
                pl.BlockSpec((tm, d), lambda i, pad: (i, 0)),
            ],
            out_specs=pl.BlockSpec(memory_space=pl.ANY),
            scratch_shapes=[pltpu.VMEM((tm, d), F32), pltpu.SemaphoreType.DMA(()), pltpu.SemaphoreType.DMA(())],
        ),
        out_shape=jax.ShapeDtypeStruct((n_rows, d), F32),
        compiler_params=_params("arbitrary"),
        name="dispatch_rows",
    )(pad_tiles, pos3, h)


class _TailParams:
    def __init__(self, nw_ref, mod_ref, gate_row, nxt):
        self.nw = nw_ref[...]
        self.gate = mod_ref[gate_row:gate_row + 1, :]
        self.has_next = nxt is not None
        if self.has_next:
            ng_ref, nmod_ref, sc_row, sh_row = nxt
            self.ng = ng_ref[...]
            self.sc1 = 1.0 + nmod_ref[sc_row:sc_row + 1, :]
            self.sh = nmod_ref[sh_row:sh_row + 1, :]

    def apply(self, out, x):
        xn = x + self.gate * (_rms(out) * self.nw)
        h = _rms(xn) * self.ng * self.sc1 + self.sh if self.has_next else None
        return xn, h


def _mm_tail_kernel(*refs, tp, two_lhs, gate_row, rows_next, moe, ch):
    refs = list(refs)
    ap_ref = refs.pop(0)
    as_ref = refs.pop(0) if two_lhs else None
    w_ref, x_ref, nw_ref, mod_ref = refs[:4]
    refs = refs[4:]
    nxt = None
    if rows_next is not None:
        nxt = (refs[0], refs[1], rows_next[0], rows_next[1])
        refs = refs[2:]
    rt_ref = refs.pop(0) if moe else None
    xo_ref = refs.pop(0)
    h_ref = refs.pop(0) if rows_next is not None else None
    lg_ref = refs.pop(0) if moe else None
    i = pl.program_id(0)
    tail = _TailParams(nw_ref, mod_ref, gate_row, nxt)
    w = w_ref[...]
    chunks = [slice(c * ch, (c + 1) * ch) for c in range(x_ref.shape[0] // ch)]
    outs = []
    for rs in chunks:
        a = ap_ref[rs, :]
        if two_lhs:
            a = jnp.where(i < tp, a, as_ref[rs, :])
        outs.append(jnp.dot(a, w, preferred_element_type=F32))
    for rs, out in zip(chunks, outs):
        xn, h = tail.apply(out, x_ref[rs, :])
        xo_ref[rs, :] = xn
        if moe:
            h_ref[rs, :] = h
            h_hi = h.astype(BF16)
            h_lo = (h - h_hi.astype(F32)).astype(BF16)
            lg_ref[rs, :] = (jnp.dot(h_hi, rt_ref[0], preferred_element_type=F32)
                             + (jnp.dot(h_lo, rt_ref[0], preferred_element_type=F32)
                                + jnp.dot(h_hi, rt_ref[1], preferred_element_type=F32)))
        elif h is not None:
            h_ref[rs, :] = h.astype(h_ref.dtype)


def _mm_tail(a, w_bf, x, rows, mod3, layer, gate_row, ng3, nw_row, nxt, router_pad=None):
    two = isinstance(a, tuple)
    k, d = w_bf.shape
    moe = router_pad is not None
    bm = rows.tile(512, 256, 128) if k <= 2048 else rows.tile(256, 128)
    ch = min(bm, 256)
    tp = rows.n_p // bm
    mt = rows.m // bm
    if two:
        lhs_specs = [pl.BlockSpec((bm, k), lambda i: (jnp.minimum(i, tp - 1), 0)),
                     pl.BlockSpec((bm, k), lambda i: (jnp.maximum(i - tp, 0), 0))]
        lhs = list(a)
    else:
        lhs_specs = [pl.BlockSpec((bm, k), lambda i: (i, 0))]
        lhs = [a]

    def modspec(lyr):
        return pl.BlockSpec((None, 6, d), lambda i: (lyr * rows.nseg + rows.seg(i, bm), 0, 0))

    in_specs = lhs_specs + [
        _const_spec((k, d)),
        pl.BlockSpec((bm, d), lambda i: (i, 0)),
        pl.BlockSpec((None, 1, d), lambda i: (nw_row, 0, 0)),
        modspec(layer),
    ]
    args = lhs + [w_bf, x, ng3, mod3]
    out_specs = [pl.BlockSpec((bm, d), lambda i: (i, 0))]
    out_shape = [jax.ShapeDtypeStruct((rows.m, d), F32)]
    rows_next = None
    if nxt is not None:
        n_row, n_layer, sc_row, sh_row = nxt
        in_specs += [pl.BlockSpec((None, 1, d), lambda i: (n_row, 0, 0)), modspec(n_layer)]
        args += [ng3, mod3]
        rows_next = (sc_row, sh_row)
        if moe:
            in_specs.append(pl.BlockSpec((2, d, LANES), lambda i: (0, 0, 0)))
            r_hi = router_pad.astype(BF16)
            args.append(jnp.stack([r_hi, (router_pad - r_hi.astype(F32)).astype(BF16)]))
            out_specs += [pl.BlockSpec((bm, d), lambda i: (i, 0)), pl.BlockSpec((bm, LANES), lambda i: (i, 0))]
            out_shape += [jax.ShapeDtypeStruct((rows.m, d), F32), jax.ShapeDtypeStruct((rows.m, LANES), F32)]
        else:
            out_specs.append(pl.BlockSpec((bm, d), lambda i: (i, 0)))
            out_shape.append(jax.ShapeDtypeStruct((rows.m, d), BF16))
    res = pl.pallas_call(
        functools.partial(_mm_tail_kernel, tp=tp, two_lhs=two, gate_row=gate_row, rows_next=rows_next, moe=moe,
                          ch=ch),
        grid=(mt,),
        in_specs=in_specs,
        out_specs=out_specs,
        out_shape=out_shape,
        compiler_params=_params("arbitrary"),
        name="matmul_tail",
    )(*args)
    return tuple(res) + (None,) * (3 - len(res))


def _combine_kernel(*refs, tm, nsteps, gate_row, rows_next):
    refs = list(refs)
    pos_ref, nxt_pos_ref, y_hbm, wt_ref, x_ref, nw_ref, mod_ref = refs[:7]
    refs = refs[7:]
    nxt = None
    if rows_next is not None:
        nxt = (refs[0], refs[1], rows_next[0], rows_next[1])
        refs = refs[2:]
    xo_ref = refs.pop(0)
    h_ref = refs.pop(0) if rows_next is not None else None
    buf, sem = refs
    i = pl.program_id(0)

    def tok_copy(b, r, slot, t):
        return pltpu.make_async_copy(y_hbm.at[pl.ds(t, 1), :], buf.at[b, slot, pl.ds(r, 1), :], sem.at[b])

    def fetch(b, idx_ref):
        for slot in range(TOP_K):
            _issue_unrolled(tm, lambda r, slot=slot: tok_copy(b, r, slot, idx_ref[0, slot, r]))

    @pl.when(i == 0)
    def _():
        fetch(0, pos_ref)

    tail = _TailParams(nw_ref, mod_ref, gate_row, nxt)
    for b in (0, 1):
        @pl.when(i % 2 == b)
        def _(b=b):
            @pl.when(i + 1 < nsteps)
            def _():
                fetch(1 - b, nxt_pos_ref)

            for slot in range(TOP_K):
                _wait_unrolled(tm, lambda r, slot=slot: tok_copy(b, r, slot, 0))

            def chunk(r, carry):
                rs = pl.ds(pl.multiple_of(r * TAIL_ROWS, TAIL_ROWS), TAIL_ROWS)
                out = wt_ref[rs, 0:1] * buf[b, 0, rs, :]
                for slot in range(1, TOP_K):
                    out = out + wt_ref[rs, slot:slot + 1] * buf[b, slot, rs, :]
                xn, h = tail.apply(out, x_ref[rs, :])
                xo_ref[rs, :] = xn
                if h is not None:
                    h_ref[rs, :] = h.astype(h_ref.dtype)
                return carry

            lax.fori_loop(0, tm // TAIL_ROWS, chunk, 0)


def _combine_tail(ys3, pos, wts, x, rows, mod3, layer, gate_row, ng3, nw_row, nxt, *, row0, nrows):
    d = ys3.shape[1]
    tm = _pick(math.gcd(row0, nrows) if row0 else nrows, 256, 128)
    t0 = row0 // tm
    nt = nrows // tm

    def modspec(lyr):
        return pl.BlockSpec((None, 6, d), lambda i: (lyr * rows.nseg + rows.seg(t0 + i, tm), 0, 0))

    in_specs = [
        pl.BlockSpec((1, TOP_K, tm), lambda i: (t0 + i, 0, 0), memory_space=pltpu.SMEM),
        pl.BlockSpec((1, TOP_K, tm), lambda i: (t0 + jnp.minimum(i + 1, nt - 1), 0, 0), memory_space=pltpu.SMEM),
        pl.BlockSpec(memory_space=pl.ANY),
        pl.BlockSpec((tm, LANES), lambda i: (t0 + i, 0)),
        pl.BlockSpec((tm, d), lambda i: (t0 + i, 0)),
        pl.BlockSpec((None, 1, d), lambda i: (nw_row, 0, 0)),
        modspec(layer),
    ]
    pos3 = jnp.transpose(pos.reshape(TOP_K, rows.m // tm, tm), (1, 0, 2))
    args = [pos3, pos3, ys3, wts, x, ng3, mod3]
    rows_next = None
    out_specs = [pl.BlockSpec((tm, d), lambda i: (i, 0))]
    out_shape = [jax.ShapeDtypeStruct((nrows, d), F32)]
    if nxt is not None:
        n_row, n_layer, sc_row, sh_row = nxt
        in_specs += [pl.BlockSpec((None, 1, d), lambda i: (n_row, 0, 0)), modspec(n_layer)]
        args += [ng3, mod3]
        out_specs.append(pl.BlockSpec((tm, d), lambda i: (i, 0)))
        out_shape.append(jax.ShapeDtypeStruct((nrows, d), BF16))
        rows_next = (sc_row, sh_row)
    res = pl.pallas_call(
        functools.partial(_combine_kernel, tm=tm, nsteps=nt, gate_row=gate_row, rows_next=rows_next),
        grid=(nt,),
        in_specs=in_specs,
        out_specs=out_specs,
        out_shape=out_shape,
        scratch_shapes=[pltpu.VMEM((2, TOP_K, tm, d), F32), pltpu.SemaphoreType.DMA((2,))],
        compiler_params=_params("arbitrary"),
        name="combine_tail",
    )(*args)
    return (res[0], res[1]) if nxt is not None else (res[0], None)


def _routing_tables(idx, n_experts, tm):
    m = idx.shape[1]
    p = TOP_K * m + n_experts * tm
    e = idx.reshape(-1)
    onehot = (e[:, None] == jnp.arange(n_experts, dtype=jnp.int32)[None, :]).astype(jnp.int32)
    rank = jnp.sum((jnp.cumsum(onehot, axis=0) - 1) * onehot, axis=1)
    counts = jnp.sum(onehot, axis=0)
    padded = ((counts + tm - 1) // tm) * tm
    ends = jnp.cumsum(padded)
    starts = ends - padded
    pos = starts[e] + rank
    n_tiles = p // tm
    tile_start = jnp.arange(n_tiles, dtype=jnp.int32) * tm
    tile_expert = jnp.sum((tile_start[:, None] >= ends[None, :]).astype(jnp.int32), axis=1)
    pad_tiles = jnp.concatenate([jnp.maximum(ends // tm - 1, 0),
                                 jnp.arange(n_tiles - n_experts, n_tiles, dtype=jnp.int32)]).astype(jnp.int32)
    return pos.reshape(TOP_K, m), jnp.minimum(tile_expert, n_experts - 1), pad_tiles


def kernel(x_prompt, x_sample, cache_k, cache_v, c, c_ctx, ada_w, ada_b, norm_g, attn_w_in, attn_lambda, attn_subln, attn_w_out, gmlp_w_in, gmlp_ln_g, gmlp_ln_b, gmlp_w_s, gmlp_b_s, gmlp_w_out, hyena_w_in, hyena_b_in, hyena_conv_w, hyena_conv_b, hyena_ffn_w1, hyena_ffn_b1, hyena_ffn_w2, hyena_ffn_b2, hyena_ffn_w3, hyena_sin_freq, hyena_bias, hyena_w_out, ffn_w_gate, ffn_w_up, ffn_w_down, moe_router, moe_w_gate, moe_w_up, moe_w_down):
    batch, seq, d = x_prompt.shape
    dec_batch, dec_seq, _ = x_sample.shape
    depth = ada_w.shape[0]
    n_attn = attn_w_in.shape[0]
    past = cache_k.shape[2]
    n_experts = moe_router.shape[-1]
    rows = _Rows(batch * seq, dec_batch * dec_seq, dec_seq)
    n_p, n_s, m = rows.n_p, rows.n_s, rows.m
    heads = d // (2 * HEAD_DIM)

    cond8 = jnp.concatenate([c_ctx[None, :], c, jnp.zeros((8 - rows.nseg, d), F32)], axis=0)
    mod = _modulation(cond8, ada_w, ada_b)[:, :rows.nseg]
    mod3 = mod.reshape(depth * rows.nseg, 6, d)
    ng3 = norm_g.reshape(depth * 4, 1, d)

    x, h = _prenorm_join(x_prompt.reshape(n_p, d), x_sample.reshape(n_s, d), ng3, mod3, rows, 0)

    rope = _rope_tables(dec_seq)
    ck4 = cache_k.reshape(dec_batch, n_attn, past, d)
    cv4 = cache_v.reshape(dec_batch, n_attn, past, d)
    new_k = new_v = None
    y_prompt = y_sample = None

    for i in range(depth):
        kind, j = i % 3, i // 3
        if kind == 0:
            lam_init = 0.8 - 0.6 * math.exp(-0.3 * i)
            q_p = _proj(h, attn_w_in, j, row0=0, nrows=n_p, col0=0, ncols=d)
            new_k = _proj(h, attn_w_in, j, row0=0, nrows=n_p, col0=d, ncols=d, out_dtype=F32,
                          cache=(new_k, n_attn, j, batch, seq))
            new_v = _proj(h, attn_w_in, j, row0=0, nrows=n_p, col0=2 * d, ncols=d, out_dtype=F32,
                          cache=(new_v, n_attn, j, batch, seq))
            qk_s = _proj(h, attn_w_in, j, row0=n_p, nrows=n_s, col0=0, ncols=2 * d, epilogue="rope", rope=rope,
                         seq=dec_seq)
            v_s = _proj(h, attn_w_in, j, row0=n_p, nrows=n_s, col0=2 * d, ncols=d)
            o_p = _attention(q_p, 0, new_k, j, 0, new_v, j, None, attn_lambda, attn_subln, j, lam_init,
                             batch=batch, seq=seq, heads_per_step=heads, bq=seq, lag=4 * heads)
            o_s = _attention(qk_s, 0, qk_s.reshape(dec_batch, 1, dec_seq, 2 * d), 0, d,
                             v_s.reshape(dec_batch, 1, dec_seq, d), 0, (ck4, cv4), attn_lambda, attn_subln, j,
                             lam_init, batch=dec_batch, seq=dec_seq, heads_per_step=min(2, heads),
                             bq=_pick(dec_seq, 256, 128), lag=8)
            mix, w_out = (o_p, o_s), attn_w_out
        elif kind == 1:
            uv = _proj(h, gmlp_w_in, j, row0=0, nrows=m, col0=0, ncols=2 * d, epilogue="gelu")
            mix, w_out = _gmlp_gate(uv, gmlp_ln_g, gmlp_ln_b, gmlp_w_s, gmlp_b_s, j), gmlp_w_out
        else:
            p = _proj(h, hyena_w_in, j, row0=0, nrows=m, col0=0, ncols=3 * d, epilogue="bias", bias=hyena_b_in)
            outs = []
            for row0, nb, length in ((0, batch, seq), (n_p, dec_batch, dec_seq)):
                fmat = _dft_matrix(length)
                a_tab, b_tab = _hyena_filters(length, d, hyena_ffn_w1, hyena_ffn_b1, hyena_ffn_w2, hyena_ffn_b2,
                                              hyena_ffn_w3, hyena_sin_freq, j)
                kh, kn = _hyena_spectrum(fmat, a_tab, b_tab)
                outs.append(_hyena_conv(p, row0, nb, length, fmat, kh, kn, hyena_conv_w, hyena_conv_b, hyena_bias, j))
            mix, w_out = tuple(outs), hyena_w_out

        moe_layer = i % 2 == 1
        jj = i // 2
        router_pad = jnp.pad(moe_router[jj], ((0, 0), (0, LANES - n_experts))) if moe_layer else None
        x, h2, logits = _mm_tail(mix, w_out[j].astype(BF16), x, rows, mod3, i, G1, ng3, i * 4 + 1,
                                 (i * 4 + 2, i, SC2, SH2), router_pad=router_pad)

        last = i == depth - 1
        nxt = None if last else ((i + 1) * 4, i + 1, SC1, SH1)
        if not moe_layer:
            tm = rows.tile(1024, 512, 256, 128)
            a = _swiglu_up(h2, ffn_w_gate[:, None], ffn_w_up[:, None], jj, jnp.zeros((m // tm,), jnp.int32), tm)
            x, h, _ = _mm_tail(a, ffn_w_down[jj].astype(BF16), x, rows, mod3, i, G2, ng3, i * 4 + 3, nxt)
        else:
            tm = rows.tile(512, 256, 128)
            idx, wts = _router(logits, n_experts)
            pos, tile_expert, pad_tiles = _routing_tables(jnp.transpose(idx[:, :TOP_K]), n_experts, tm)
            xs = _dispatch_rows(h2, pos, pad_tiles, TOP_K * m + n_experts * tm, tm)
            a = _swiglu_up(xs, moe_w_gate, moe_w_up, jj, tile_expert, tm, bn=min(768, moe_w_gate.shape[-1]))
            ys = _expert_down(a, moe_w_down, jj, tile_expert, tm)
            tail_args = (ys, pos, wts, x, rows, mod3, i, G2, ng3, i * 4 + 3)
            if last:
                y_prompt, _ = _combine_tail(*tail_args, None, row0=0, nrows=n_p)
                y_sample, _ = _combine_tail(*tail_args, None, row0=n_p, nrows=n_s)
            else:
                x, h = _combine_tail(*tail_args, nxt, row0=0, nrows=m)

    if y_prompt is None:
        y_prompt, y_sample = x[:n_p], x[n_p:]
    y_prompt = y_prompt.reshape(batch, seq, d)
    y_sample = y_sample.reshape(dec_batch, dec_seq, d)
    hk = cache_k.shape[3]
    hv = cache_v.shape[3]
    return (y_prompt, y_sample, new_k.reshape(batch, n_attn, seq, hk, d // hk),
            new_v.reshape(batch, n_attn, seq, hv, d // hv))
```

```python
import functools
import math

import jax
import jax.numpy as jnp
from jax import lax
from jax.experimental import pallas as pl
from jax.experimental.pallas import tpu as pltpu

F32 = jnp.float32
BF16 = jnp.bfloat16

EPS = 1e-6
GRID_W = 64
ROPE_THETA = 10000.0
CHUNK = 128
HYENA_BANDS = 16
HYENA_FAST_DECAY = 0.3
HYENA_SLOW_DECAY = 1.5
HYENA_TARGET = 0.01
HEAD_DIM = 64
TOP_K = 2

V7X_VMEM_BYTES = 64 * 1024 * 1024
VMEM_LIMIT = V7X_VMEM_BYTES - 8 * 1024 * 1024
LANES = 128
TAIL_ROWS = 128
FREQ_ROWS = 512
DMA_UNROLL = 8

SH1, SC1, G1, SH2, SC2, G2 = range(6)


def _pick(n, *cands):
    for c in cands:
        if n % c == 0:
            return c
    return n


def _params(*sem):
    return pltpu.CompilerParams(dimension_semantics=sem, vmem_limit_bytes=VMEM_LIMIT)


def _const_spec(shape):
    nd = len(shape)
    return pl.BlockSpec(shape, lambda *_: (0,) * nd, pipeline_mode=pl.Buffered(1))


def _rms(x):
    return x * lax.rsqrt(jnp.mean(x * x, axis=-1, keepdims=True) + EPS)


class _Rows:
    def __init__(self, n_p, n_s, dec_seq):
        self.n_p, self.n_s, self.dec_seq = n_p, n_s, dec_seq
        self.m = n_p + n_s
        self.nseg = 1 + n_s // dec_seq

    def seg(self, i, bm):
        r = i * bm
        return jnp.where(r < self.n_p, 0, 1 + (r - self.n_p) // self.dec_seq)

    def tile(self, *cands):
        return _pick(math.gcd(self.n_p, self.dec_seq), *cands)


def _mod_kernel(cond_ref, w_ref, b_ref, o_ref):
    s = cond_ref[...]
    s = s * jax.nn.sigmoid(s)
    o_ref[...] = jnp.dot(s.astype(BF16), w_ref[...].astype(BF16), preferred_element_type=F32) + b_ref[...]


def _modulation(cond8, ada_w, ada_b):
    depth, d, n6 = ada_w.shape
    bn = _pick(n6, 1024, 512, 256, 128)
    return pl.pallas_call(
        _mod_kernel,
        grid=(depth, n6 // bn),
        in_specs=[
            pl.BlockSpec((8, d), lambda l, j: (0, 0)),
            pl.BlockSpec((None, d, bn), lambda l, j: (l, 0, j)),
            pl.BlockSpec((None, 1, bn), lambda l, j: (l, 0, j)),
        ],
        out_specs=pl.BlockSpec((None, 8, bn), lambda l, j: (l, 0, j)),
        out_shape=jax.ShapeDtypeStruct((depth, 8, n6), F32),
        compiler_params=_params("arbitrary", "arbitrary"),
        name="modulation",
    )(cond8, ada_w, ada_b.reshape(depth, 1, n6))


def _pre_kernel(xp_ref, xs_ref, g_ref, mod_ref, x_ref, h_ref, *, tp):
    i = pl.program_id(0)

    def emit(src_ref):
        x = src_ref[...]
        x_ref[...] = x
        h = _rms(x) * g_ref[...] * (1.0 + mod_ref[SC1:SC1 + 1, :]) + mod_ref[SH1:SH1 + 1, :]
        h_ref[...] = h.astype(h_ref.dtype)

    @pl.when(i < tp)
    def _():
        emit(xp_ref)

    @pl.when(i >= tp)
    def _():
        emit(xs_ref)


def _prenorm_join(xp, xs, ng3, mod3, rows, layer):
    d = xp.shape[-1]
    bm = rows.tile(512, 256, 128)
    tp = rows.n_p // bm
    ts = rows.n_s // bm
    return pl.pallas_call(
        functools.partial(_pre_kernel, tp=tp),
        grid=(tp + ts,),
        in_specs=[
            pl.BlockSpec((bm, d), lambda i: (jnp.minimum(i, tp - 1), 0)),
            pl.BlockSpec((bm, d), lambda i: (jnp.maximum(i - tp, 0), 0)),
            pl.BlockSpec((None, 1, d), lambda i: (layer * 4, 0, 0)),
            pl.BlockSpec((None, 6, d), lambda i: (layer * rows.nseg + rows.seg(i, bm), 0, 0)),
        ],
        out_specs=[pl.BlockSpec((bm, d), lambda i: (i, 0)), pl.BlockSpec((bm, d), lambda i: (i, 0))],
        out_shape=[jax.ShapeDtypeStruct((rows.m, d), F32), jax.ShapeDtypeStruct((rows.m, d), BF16)],
        compiler_params=_params("arbitrary"),
        name="prenorm_join",
    )(xp, xs, ng3, mod3)


def _proj_kernel(*refs, epilogue, half, cache_rows, cache_slot=None):
    if epilogue == "bias":
        x_ref, w_ref, b_ref, o_ref = refs
    elif epilogue == "rope":
        x_ref, w_ref, cos_ref, sin_ref, o_ref = refs
    else:
        x_ref, w_ref, o_ref = refs
    acc = jnp.dot(x_ref[...], w_ref[...].astype(BF16), preferred_element_type=F32)
    if epilogue == "bias":
        acc = acc + b_ref[...]
    elif epilogue == "gelu":
        acc = 0.5 * acc * (1.0 + lax.erf(acc * (2.0 ** -0.5)))
    if epilogue == "rope":
        cos_t = cos_ref[...]
        sin_t = sin_ref[...]
        lane = lax.broadcasted_iota(jnp.int32, cos_t.shape, 1)
        first = (lane % (2 * half)) < half
        for cblk in range(acc.shape[1] // LANES):
            blk = acc[:, cblk * LANES:(cblk + 1) * LANES]
            partner = jnp.where(first, pltpu.roll(blk, LANES - half, 1), pltpu.roll(blk, half, 1))
            o_ref[:, cblk * LANES:(cblk + 1) * LANES] = (blk * cos_t + partner * sin_t).astype(o_ref.dtype)
    elif cache_slot is not None:
        o_ref[...] = jnp.zeros(o_ref.shape, o_ref.dtype)
        o_ref[:, cache_slot] = acc.reshape(o_ref.shape[0], *o_ref.shape[2:]).astype(o_ref.dtype)
    elif cache_rows:
        o_ref[...] = acc.reshape(o_ref.shape).astype(o_ref.dtype)
    else:
        o_ref[...] = acc.astype(o_ref.dtype)


def _proj(h, w, layer, *, row0, nrows, col0, ncols, out_dtype=BF16, epilogue="none", bias=None,
          rope=None, seq=None, cache=None):
    k = h.shape[1]
    bm = _pick(math.gcd(math.gcd(row0, nrows), seq or 0), 1024, 512, 256, 128)
    bn = _pick(math.gcd(col0, ncols) if col0 else ncols, 1024, 512, 256, 128)
    r0, c0 = row0 // bm, col0 // bn
    in_specs = [
        pl.BlockSpec((bm, k), lambda j, i: (r0 + i, 0)),
        pl.BlockSpec((None, k, bn), lambda j, i: (layer, 0, c0 + j)),
    ]
    args = [h, w]
    half = 0
    if epilogue == "bias":
        in_specs.append(pl.BlockSpec((None, 1, bn), lambda j, i: (layer, 0, c0 + j)))
        args.append(bias.reshape(bias.shape[0], 1, bias.shape[1]))
    elif epilogue == "rope":
        cos_t, sin_t, half = rope
        per_seq = seq // bm
        in_specs += [pl.BlockSpec((bm, LANES), lambda j, i: (i % per_seq, 0))] * 2
        args += [cos_t, sin_t]
    aliases = {}
    first_slot = None
    if cache is None:
        out_spec = pl.BlockSpec((bm, bn), lambda j, i: (i, j))
        out_shape = jax.ShapeDtypeStruct((nrows, ncols), out_dtype)
    else:
        arr, n_slots, slot, batch, cseq = cache
        bb = bm // cseq
        out_shape = jax.ShapeDtypeStruct((batch, n_slots, cseq, ncols), out_dtype)
        if arr is None:
            out_spec = pl.BlockSpec((bb, n_slots, cseq, bn), lambda j, i: (i, 0, 0, j))
            first_slot = slot
        else:
            out_spec = pl.BlockSpec((bb, None, cseq, bn), lambda j, i: (i, slot, 0, j))
            in_specs.append(pl.BlockSpec(memory_space=pl.ANY))
            args.append(arr)
            aliases = {len(args) - 1: 0}

    def body(*refs):
        if aliases:
            refs = refs[:len(args) - 1] + refs[len(args):]
        _proj_kernel(*refs, epilogue=epilogue, half=half, cache_rows=cache is not None, cache_slot=first_slot)

    return pl.pallas_call(
        body,
        grid=(ncols // bn, nrows // bm),
        in_specs=in_specs,
        out_specs=out_spec,
        out_shape=out_shape,
        input_output_aliases=aliases,
        compiler_params=_params("arbitrary", "arbitrary"),
        name="proj_" + epilogue,
    )(*args)


def _rope_tables(dec_seq):
    rows = dec_seq // GRID_W
    row = jnp.repeat(jnp.arange(rows), GRID_W).astype(F32)
    col = jnp.tile(jnp.arange(GRID_W), rows).astype(F32)
    half = HEAD_DIM // 4
    inv = ROPE_THETA ** (-jnp.arange(half, dtype=F32) / half)
    ar = row[:, None] * inv[None, :]
    ac = col[:, None] * inv[None, :]
    cos64 = jnp.concatenate([jnp.cos(ar), jnp.cos(ar), jnp.cos(ac), jnp.cos(ac)], axis=-1)
    sin64 = jnp.concatenate([-jnp.sin(ar), jnp.sin(ar), -jnp.sin(ac), jnp.sin(ac)], axis=-1)
    reps = LANES // HEAD_DIM
    return jnp.tile(cos64, (1, reps)), jnp.tile(sin64, (1, reps)), half


def _attn_kernel(*refs, heads, has_cache, lam_init, lag):
    if has_cache:
        lam_ref, sub_ref, q_ref, k_ref, v_ref, kc_ref, vc_ref, o_ref = refs
    else:
        lam_ref, sub_ref, q_ref, k_ref, v_ref, o_ref = refs
    p = lam_ref[...]
    lam = (jnp.exp(jnp.sum(p[0:1] * p[1:2], axis=-1, keepdims=True))
           - jnp.exp(jnp.sum(p[2:3] * p[3:4], axis=-1, keepdims=True)) + lam_init)
    scale = HEAD_DIM ** -0.5
    assert math.log2(HEAD_DIM) % 2 == 0, "folding the score scale into bf16 queries needs a power-of-two scale"
    nt = (((1,), (1,)), ((), ()))
    head_slices = [slice(h * 2 * HEAD_DIM, (h + 1) * 2 * HEAD_DIM) for h in range(heads)]
    n_units = 2 * heads

    def score_stage(u):
        sl, ms = head_slices[u // 2], slice((u % 2) * HEAD_DIM, (u % 2 + 1) * HEAD_DIM)
        q = q_ref[:, sl][:, ms] * scale
        s = lax.dot_general(q, k_ref[:, sl][:, ms].astype(BF16), nt, preferred_element_type=F32)
        sc = None
        if has_cache:
            sc = lax.dot_general(q, kc_ref[:, sl][:, ms].astype(BF16), nt, preferred_element_type=F32)
        return s, sc

    def exp_stage(s, sc):
        mx = jnp.max(s, axis=-1, keepdims=True)
        if has_cache:
            mx = jnp.maximum(mx, jnp.max(sc, axis=-1, keepdims=True))
        return jnp.exp((s - mx).astype(BF16)), jnp.exp((sc - mx).astype(BF16)) if has_cache else None

    def value_stage(u, e, ec):
        sl = head_slices[u // 2]
        v = v_ref[:, sl].astype(BF16)
        acc = jnp.dot(e, jnp.concatenate([v, jnp.ones_like(v)], axis=1), preferred_element_type=F32)
        if has_cache:
            vc = vc_ref[:, sl].astype(BF16)
            acc = acc + jnp.dot(ec, jnp.concatenate([vc, jnp.ones_like(vc)], axis=1), preferred_element_type=F32)
        vd = acc.shape[1] // 2
        return acc[:, :vd] / acc[:, vd:]

    half = lag // 2
    scores, weights, maps = {}, {}, {}
    for t in range(n_units + lag):
        if 0 <= t - lag:
            maps[t - lag] = value_stage(t - lag, *weights.pop(t - lag))
        if 0 <= t - half < n_units:
            weights[t - half] = exp_stage(*scores.pop(t - half))
        if t < n_units:
            scores[t] = score_stage(t)
    for h, sl in enumerate(head_slices):
        o = maps[2 * h] - lam * maps[2 * h + 1]
        o = _rms(o) * sub_ref[...] * (1.0 - lam_init)
        o_ref[:, sl] = o.astype(o_ref.dtype)


def _attention(q, q_col0, k4, k_slot, k_col0, v4, v_slot, cache4, lam, subln, layer_j, lam_init, *, batch, seq,
               heads_per_step, bq, lag):
    d = v4.shape[-1]
    hw = heads_per_step * 2 * HEAD_DIM
    nk = k4.shape[2]
    per_seq = seq // bq
    qc0, kc0 = q_col0 // hw, k_col0 // hw
    in_specs = [
        pl.BlockSpec((None, 4, HEAD_DIM), lambda b, h, i: (layer_j, 0, 0)),
        pl.BlockSpec((None, 1, 2 * HEAD_DIM), lambda b, h, i: (layer_j, 0, 0)),
        pl.BlockSpec((bq, hw), lambda b, h, i: (b * per_seq + i, qc0 + h)),
        pl.BlockSpec((None, None, nk, hw), lambda b, h, i: (b, k_slot, 0, kc0 + h)),
        pl.BlockSpec((None, None, nk, hw), lambda b, h, i: (b, v_slot, 0, h)),
    ]
    args = [lam, subln.reshape(subln.shape[0], 1, subln.shape[1]), q, k4, v4]
    if cache4 is not None:
        ck, cv = cache4
        nc = ck.shape[2]
        in_specs += [pl.BlockSpec((None, None, nc, hw), lambda b, h, i: (b, layer_j, 0, h))] * 2
        args += [ck, cv]
    return pl.pallas_call(
        functools.partial(_attn_kernel, heads=heads_per_step, has_cache=cache4 is not None, lam_init=lam_init,
                          lag=lag),
        grid=(batch, d // hw, per_seq),
        in_specs=in_specs,
        out_specs=pl.BlockSpec((bq, hw), lambda b, h, i: (b * per_seq + i, h)),
        out_shape=jax.ShapeDtypeStruct((batch * seq, d), BF16),
        compiler_params=_params("arbitrary", "arbitrary", "arbitrary"),
        name="diff_attention",
    )(*args)


def _gmlp_kernel(u_ref, v_ref, g_ref, b_ref, ws_ref, bs_ref, o_ref, *, chunks, groups, gd):
    v = v_ref[...].astype(F32)
    mu = jnp.mean(v, axis=-1, keepdims=True)
    vc = v - mu
    var = jnp.mean(vc * vc, axis=-1, keepdims=True)
    vn = (vc * lax.rsqrt(var + EPS) * g_ref[...] + b_ref[...]).astype(BF16)
    for g in range(groups):
        w = ws_ref[g].astype(BF16)
        cs = slice(g * gd, (g + 1) * gd)
        for c in range(chunks):
            rs = slice(c * CHUNK, (c + 1) * CHUNK)
            vm = jnp.dot(w, vn[rs, cs], preferred_element_type=F32) + bs_ref[:, cs]
            o_ref[rs, cs] = (u_ref[rs, cs].astype(F32) * vm).astype(o_ref.dtype)


def _gmlp_gate(uv, ln_g, ln_b, w_s, b_s, layer_j):
    m, d2 = uv.shape
    d = d2 // 2
    groups = w_s.shape[1]
    gd = d // groups
    tm = _pick(m, 512, 256, 128)
    bs_full = jnp.repeat(jnp.transpose(b_s[layer_j]), gd, axis=1)
    return pl.pallas_call(
        functools.partial(_gmlp_kernel, chunks=tm // CHUNK, groups=groups, gd=gd),
        grid=(m // tm,),
        in_specs=[
            pl.BlockSpec((tm, d), lambda i: (i, 0)),
            pl.BlockSpec((tm, d), lambda i: (i, 1)),
            pl.BlockSpec((None, 1, d), lambda i: (layer_j, 0, 0)),
            pl.BlockSpec((None, 1, d), lambda i: (layer_j, 0, 0)),
            pl.BlockSpec((None, groups, CHUNK, CHUNK), lambda i: (layer_j, 0, 0, 0)),
            pl.BlockSpec((CHUNK, d), lambda i: (0, 0)),
        ],
        out_specs=pl.BlockSpec((tm, d), lambda i: (i, 0)),
        out_shape=jax.ShapeDtypeStruct((m, d), BF16),
        compiler_params=_params("arbitrary"),
        name="gmlp_gate",
    )(uv, uv, ln_g.reshape(ln_g.shape[0], 1, d), ln_b.reshape(ln_b.shape[0], 1, d), w_s, bs_full)


def _dft_matrix(length):
    n = 2 * length
    split = _pick(length, 32, 16, 8)
    f = jnp.arange(length, dtype=jnp.int32)[:, None]

    def cos_sin(t):
        ang = ((f * t[None, :]) % n).astype(F32) * (2.0 * math.pi / n)
        return jnp.cos(ang), jnp.sin(ang)

    ca, sa = cos_sin(jnp.arange(length // split, dtype=jnp.int32) * split)
    cb, sb = cos_sin(jnp.arange(split, dtype=jnp.int32))
    cos_ft = (ca[:, :, None] * cb[:, None, :] - sa[:, :, None] * sb[:, None, :]).reshape(length, length)
    sin_ft = (sa[:, :, None] * cb[:, None, :] + ca[:, :, None] * sb[:, None, :]).reshape(length, length)
    return jnp.concatenate([cos_ft, -sin_ft], axis=0).astype(BF16)


def _filter_features(length):
    pos = jnp.arange(length, dtype=F32)
    t = jnp.linspace(0.0, 1.0, length, dtype=F32)[:, None]
    bands = jnp.linspace(1e-4, HYENA_BANDS - 1, HYENA_BANDS, dtype=F32)
    ang = (2.0 * math.pi / length) * pos[:, None] * bands[None, :]
    z = jnp.concatenate([t, jnp.cos(ang), -jnp.sin(ang)], axis=-1)
    return jnp.pad(z, ((0, 0), (0, LANES - z.shape[1])))


def _filt_kernel(z_ref, w1_ref, b1_ref, w2_ref, b2_ref, fr_ref, w00, w01, w10, w11, dl_ref, a_ref, b_ref):
    hi = lax.Precision.HIGHEST
    z = z_ref[...]
    h = jnp.sin(fr_ref[0:1, :] * (jnp.dot(z, w1_ref[...], precision=hi, preferred_element_type=F32) + b1_ref[...]))
    h = jnp.sin(fr_ref[1:2, :] * (jnp.dot(h, w2_ref[...], precision=hi, preferred_element_type=F32) + b2_ref[...]))
    hb = h.astype(BF16)
    decay = jnp.exp(-z[:, 0:1] * dl_ref[...])
    row = lax.broadcasted_iota(jnp.int32, (z.shape[0], 1), 0)
    for o, (wf, wb) in enumerate(((w00, w01), (w10, w11))):
        fwd = jnp.dot(hb, wf[...].astype(BF16), preferred_element_type=F32) * decay
        bwd = jnp.dot(hb, wb[...].astype(BF16), preferred_element_type=F32) * decay
        bwd = jnp.where(row == 0, 0.0, bwd)
        norm = (jnp.sum(jnp.abs(fwd), axis=0, keepdims=True) + jnp.sum(jnp.abs(bwd), axis=0, keepdims=True) + EPS)
        inv = 1.0 / norm
        a_ref[o] = ((fwd + bwd) * inv).astype(a_ref.dtype)
        b_ref[o] = ((fwd - bwd) * inv).astype(b_ref.dtype)


def _hyena_filters(length, d, w1, b1, w2, b2, w3, freq, layer_j):
    fh = w2.shape[-1]
    emb = w1.shape[1]
    bd = _pick(d, 512, 256, 128)
    nb = d // bd
    zfeat = _filter_features(length)
    w1p = jnp.pad(w1[layer_j], ((0, LANES - emb), (0, 0)))
    max_decay = math.log(HYENA_TARGET) / HYENA_FAST_DECAY
    min_decay = math.log(HYENA_TARGET) / HYENA_SLOW_DECAY
    deltas = jnp.abs(jnp.linspace(min_decay, max_decay, d, dtype=F32))[None, :]
    w3_specs = [pl.BlockSpec((None, fh, bd), functools.partial(lambda c, g: (layer_j, 0, g * nb + c), g=g))
                for g in range(4)]
    out_spec = pl.BlockSpec((2, length, bd), lambda c: (0, 0, c))
    return pl.pallas_call(
        _filt_kernel,
        grid=(nb,),
        in_specs=[
            pl.BlockSpec((length, LANES), lambda c: (0, 0)),
            pl.BlockSpec((LANES, fh), lambda c: (0, 0)),
            pl.BlockSpec((None, 1, fh), lambda c: (layer_j, 0, 0)),
            pl.BlockSpec((None, fh, fh), lambda c: (layer_j, 0, 0)),
            pl.BlockSpec((None, 1, fh), lambda c: (layer_j, 0, 0)),
            pl.BlockSpec((None, 2, fh), lambda c: (layer_j, 0, 0)),
            *w3_specs,
            pl.BlockSpec((1, bd), lambda c: (0, c)),
        ],
        out_specs=[out_spec, out_spec],
        out_shape=[jax.ShapeDtypeStruct((2, length, d), BF16)] * 2,
        compiler_params=_params("arbitrary"),
        name="hyena_filters",
    )(zfeat, w1p, b1.reshape(b1.shape[0], 1, fh), w2, b2.reshape(b2.shape[0], 1, fh), freq, w3, w3, w3, w3, deltas)


def _spectrum_kernel(f_ref, a_ref, b_ref, kh_ref, kn_ref, *, length):
    n = 2 * length
    row = lax.broadcasted_iota(jnp.int32, (length, 1), 0)
    wn = jnp.where(row == 0, 1.0 / n, 2.0 / n)
    sgn = jnp.where(row % 2 == 0, 1.0, -1.0)
    a = a_ref[...]
    kh_ref[0:length, :] = jnp.dot(f_ref[0:length, :], a, preferred_element_type=F32) * wn
    kh_ref[length:n, :] = jnp.dot(f_ref[length:n, :], b_ref[...], preferred_element_type=F32) * wn
    nyq = jnp.sum(a.astype(F32) * sgn, axis=0, keepdims=True) * (1.0 / n)
    kn_ref[...] = jnp.broadcast_to(nyq, kn_ref.shape)


def _hyena_spectrum(fmat, a_tab, b_tab):
    _, length, d = a_tab.shape
    bd = _pick(d, 256, 128)
    tab_spec = pl.BlockSpec((None, length, bd), lambda o, c: (o, 0, c))
    return pl.pallas_call(
        functools.partial(_spectrum_kernel, length=length),
        grid=(2, d // bd),
        in_specs=[_const_spec((2 * length, length)), tab_spec, tab_spec],
        out_specs=[pl.BlockSpec((None, 2 * length, bd), lambda o, c: (o, 0, c)),
                   pl.BlockSpec((None, 8, bd), lambda o, c: (o, 0, c))],
        out_shape=[jax.ShapeDtypeStruct((2, 2 * length, d), F32), jax.ShapeDtypeStruct((2, 8, d), F32)],
        compiler_params=_params("arbitrary", "arbitrary"),
        name="hyena_spectrum",
    )(fmat, a_tab, b_tab)


def _hconv_kernel(f_ref, kh_ref, kn_ref, pv_ref, p1_ref, p2_ref, cwv, cw1, cw2, cbv, cb1, cb2, hb_ref, o_ref, y_scr,
                  *, length):
    n = 2 * length
    row = lax.broadcasted_iota(jnp.int32, (length, 1), 0)
    sgn = jnp.where(row % 2 == 0, 1.0, -1.0)

    seqs = [slice(b * length, (b + 1) * length) for b in range(o_ref.shape[0] // length)]

    def short_conv(p_ref, cw, cb, seq):
        p = p_ref[seq, :].astype(F32)
        prev = jnp.where(row == 0, 0.0, pltpu.roll(p, 1, 0))
        nxt = jnp.where(row == length - 1, 0.0, pltpu.roll(p, length - 1, 0))
        return cb[...] + (prev * cw[0:1, :] + p * cw[1:2, :] + nxt * cw[2:3, :])

    fb = min(length, FREQ_ROWS)

    def to_spectrum(z, o, b):
        zb = z.astype(BF16)

        def freq_block(i, carry):
            re = pl.ds(pl.multiple_of(i * fb, fb), fb)
            im = pl.ds(pl.multiple_of(length + i * fb, fb), fb)
            zr = jnp.dot(f_ref[re, :], zb, preferred_element_type=F32)
            zi = jnp.dot(f_ref[im, :], zb, preferred_element_type=F32)
            kr, ki = kh_ref[o, re, :], kh_ref[o, im, :]
            y_scr[pl.ds(pl.multiple_of(b * n + i * fb, fb), fb), :] = (zr * kr - zi * ki).astype(BF16)
            y_scr[pl.ds(pl.multiple_of(b * n + length + i * fb, fb), fb), :] = (zr * ki + zi * kr).astype(BF16)
            return carry

        lax.fori_loop(0, length // fb, freq_block, 0)
        return jnp.sum(zb.astype(F32) * sgn, axis=0, keepdims=True) * kn_ref[o, 0:1, :]

    def from_spectrum(z, nyq, o, b):
        y = (jnp.dot(f_ref[0:length, :], y_scr[b * n:b * n + length, :], preferred_element_type=F32)
             + jnp.dot(f_ref[length:n, :], y_scr[b * n + length:(b + 1) * n, :], preferred_element_type=F32)
             + sgn * nyq)
        return y + z * hb_ref[o:o + 1, :]

    z = [short_conv(pv_ref, cwv, cbv, seq) for seq in seqs]
    for o, (p_ref, cw, cb) in enumerate(((p1_ref, cw1, cb1), (p2_ref, cw2, cb2))):
        nyq = [to_spectrum(z[b], o, b) for b in range(len(seqs))]
        gate = [short_conv(p_ref, cw, cb, seq) for seq in seqs]
        z = [gate[b] * from_spectrum(z[b], nyq[b], o, b) for b in range(len(seqs))]
    for b, seq in enumerate(seqs):
        o_ref[seq, :] = z[b].astype(o_ref.dtype)


def _hyena_conv(p, row0, batch, length, fmat, kh, kn, conv_w, conv_b, hbias, layer_j):
    d = p.shape[1] // 3
    bd = _pick(d, 256, 128) if length > 512 else _pick(d, 512, 256, 128)
    nb = d // bd
    spb = 2 if length <= 512 and batch % 2 == 0 and (row0 // length) % 2 == 0 else 1
    r0 = row0 // (spb * length)

    def pspec(g):
        return pl.BlockSpec((spb * length, bd), lambda c, b: (r0 + b, g * nb + c))

    def wspec(g, rows_):
        return pl.BlockSpec((None, rows_, bd), lambda c, b: (layer_j, 0, g * nb + c))

    return pl.pallas_call(
        functools.partial(_hconv_kernel, length=length),
        grid=(nb, batch // spb),
        in_specs=[
            _const_spec((2 * length, length)),
            pl.BlockSpec((2, 2 * length, bd), lambda c, b: (0, 0, c), pipeline_mode=pl.Buffered(1)),
            pl.BlockSpec((2, 8, bd), lambda c, b: (0, 0, c)),
            pspec(0), pspec(1), pspec(2),
            wspec(0, 3), wspec(1, 3), wspec(2, 3),
            wspec(0, 1), wspec(1, 1), wspec(2, 1),
            pl.BlockSpec((None, 2, bd), lambda c, b: (layer_j, 0, c)),
        ],
        out_specs=pl.BlockSpec((spb * length, bd), lambda c, b: (b, c)),
        out_shape=jax.ShapeDtypeStruct((batch * length, d), BF16),
        scratch_shapes=[pltpu.VMEM((spb * 2 * length, bd), BF16)],
        compiler_params=_params("arbitrary", "arbitrary"),
        name="hyena_conv",
    )(fmat, kh, kn, p, p, p, conv_w, conv_w, conv_w, conv_b.reshape(conv_b.shape[0], 1, -1),
      conv_b.reshape(conv_b.shape[0], 1, -1), conv_b.reshape(conv_b.shape[0], 1, -1), hbias)


def _tile_is_active(te_ref):
    return pl.program_id(1) < te_ref[pl.num_programs(1)]


def _up_kernel(te_ref, x_ref, wg_ref, wu_ref, o_ref):
    @pl.when(_tile_is_active(te_ref))
    def _():
        x = x_ref[...].astype(BF16)
        g = jnp.dot(x, wg_ref[...].astype(BF16), preferred_element_type=F32)
        u = jnp.dot(x, wu_ref[...].astype(BF16), preferred_element_type=F32)
        o_ref[...] = (g * jax.nn.sigmoid(g) * u).astype(o_ref.dtype)

    @pl.when(jnp.logical_not(_tile_is_active(te_ref)))
    def _():
        o_ref[...] = jnp.zeros(o_ref.shape, o_ref.dtype)


def _swiglu_up(x, wg4, wu4, layer, tile_expert, tm, bn=None):
    rows_, d = x.shape
    f = wg4.shape[-1]
    bn = bn or _pick(f, 512, 256, 128)
    wspec = pl.BlockSpec((None, None, d, bn), lambda j, i, te: (layer, te[i], 0, j))
    return pl.pallas_call(
        _up_kernel,
        grid_spec=pltpu.PrefetchScalarGridSpec(
            num_scalar_prefetch=1,
            grid=(pl.cdiv(f, bn), rows_ // tm),
            in_specs=[pl.BlockSpec((tm, d), lambda j, i, te: (i, 0)), wspec, wspec],
            out_specs=pl.BlockSpec((tm, bn), lambda j, i, te: (i, j)),
        ),
        out_shape=jax.ShapeDtypeStruct((rows_, f), BF16),
        compiler_params=_params("arbitrary", "arbitrary"),
        name="swiglu_up",
    )(tile_expert, x, wg4, wu4)


def _down_kernel(te_ref, a_ref, w_ref, o_ref):
    @pl.when(_tile_is_active(te_ref))
    def _():
        o_ref[...] = jnp.dot(a_ref[...], w_ref[...].astype(BF16), preferred_element_type=F32)

    @pl.when(jnp.logical_not(_tile_is_active(te_ref)))
    def _():
        o_ref[...] = jnp.zeros(o_ref.shape, o_ref.dtype)


def _expert_down(a, wd4, layer, tile_expert, tm):
    rows_, f = a.shape
    d = wd4.shape[-1]
    bn = _pick(d, 1024, 512, 256, 128)
    return pl.pallas_call(
        _down_kernel,
        grid_spec=pltpu.PrefetchScalarGridSpec(
            num_scalar_prefetch=1,
            grid=(d // bn, rows_ // tm),
            in_specs=[
                pl.BlockSpec((tm, f), lambda j, i, te: (i, 0)),
                pl.BlockSpec((None, None, f, bn), lambda j, i, te: (layer, te[i], 0, j)),
            ],
            out_specs=pl.BlockSpec((tm, bn), lambda j, i, te: (i, j)),
        ),
        out_shape=jax.ShapeDtypeStruct((rows_, d), F32),
        compiler_params=_params("arbitrary", "arbitrary"),
        name="expert_down",
    )(tile_expert, a, wd4)


def _router_kernel(lg_ref, idx_ref, w_ref, *, n_experts):
    lane = lax.broadcasted_iota(jnp.int32, lg_ref.shape, 1)
    logits = jnp.where(lane < n_experts, lg_ref[...], -jnp.inf)
    m1 = jnp.max(logits, axis=-1, keepdims=True)
    i1 = jnp.min(jnp.where(logits == m1, lane, LANES), axis=-1, keepdims=True)
    rest = jnp.where(lane == i1, -jnp.inf, logits)
    m2 = jnp.max(rest, axis=-1, keepdims=True)
    i2 = jnp.min(jnp.where(rest == m2, lane, LANES), axis=-1, keepdims=True)
    e2 = jnp.exp(m2 - m1)
    den = 1.0 + e2
    idx_ref[...] = jnp.where(lane == 0, i1, jnp.where(lane == 1, i2, 0))
    w_ref[...] = jnp.where(lane == 0, 1.0 / den, jnp.where(lane == 1, e2 / den, 0.0))


def _router(logits, n_experts):
    m = logits.shape[0]
    tm = _pick(m, 1024, 512, 256, 128)
    spec = pl.BlockSpec((tm, LANES), lambda i: (i, 0))
    return pl.pallas_call(
        functools.partial(_router_kernel, n_experts=n_experts),
        grid=(m // tm,),
        in_specs=[spec],
        out_specs=[spec, spec],
        out_shape=[jax.ShapeDtypeStruct((m, LANES), jnp.int32), jax.ShapeDtypeStruct((m, LANES), F32)],
        compiler_params=_params("arbitrary"),
        name="router",
    )(logits)


def _issue_unrolled(n, copy_of):
    def body(g, carry):
        for u in range(DMA_UNROLL):
            copy_of(g * DMA_UNROLL + u).start(priority=u % 2)
        return carry

    lax.fori_loop(0, n // DMA_UNROLL, body, 0)


def _wait_unrolled(n, copy_of):
    def body(g, carry):
        for u in range(DMA_UNROLL):
            copy_of(g * DMA_UNROLL + u).wait()
        return carry

    lax.fori_loop(0, n // DMA_UNROLL, body, 0)


def _dispatch_kernel(pad_ref, pos_ref, h_ref, xs_hbm, zbuf, sem, zsem, *, tm, n_pad):
    @pl.when(pl.program_id(0) == 0)
    def _():
        zbuf[...] = jnp.zeros(zbuf.shape, zbuf.dtype)
        for z in range(n_pad):
            zero = pltpu.make_async_copy(zbuf, xs_hbm.at[pl.ds(pl.multiple_of(pad_ref[z] * tm, tm), tm), :], zsem)
            zero.start()
            zero.wait()

    def row_copy(r, slot, t):
        return pltpu.make_async_copy(h_ref.at[pl.ds(r, 1), :], xs_hbm.at[pl.ds(t, 1), :], sem)

    for slot in range(TOP_K):
        _issue_unrolled(tm, lambda r, slot=slot: row_copy(r, slot, pos_ref[0, slot, r]))
    for slot in range(TOP_K):
        _wait_unrolled(tm, lambda r, slot=slot: row_copy(r, slot, 0))


def _dispatch_rows(h, pos, pad_tiles, n_rows, tm):
    m, d = h.shape
    n_pad = pad_tiles.shape[0]
    pos3 = jnp.transpose(pos.reshape(TOP_K, m // tm, tm), (1, 0, 2))
    return pl.pallas_call(
        functools.partial(_dispatch_kernel, tm=tm, n_pad=n_pad),
        grid_spec=pltpu.PrefetchScalarGridSpec(
            num_scalar_prefetch=1,
            grid=(m // tm,),
            in_specs=[
                pl.BlockSpec((1, TOP_K, tm), lambda i, pad: (i, 0, 0), memory_space=pltpu.SMEM),
                pl.BlockSpec((tm, d), lambda i, pad: (i, 0)),
            ],
            out_specs=pl.BlockSpec(memory_space=pl.ANY),
            scratch_shapes=[pltpu.VMEM((tm, d), F32), pltpu.SemaphoreType.DMA(()), pltpu.SemaphoreType.DMA(())],
        ),
        out_shape=jax.ShapeDtypeStruct((n_rows, d), F32),
        compiler_params=_params("arbitrary"),
        name="dispatch_rows",
    )(pad_tiles, pos3, h)


class _TailParams:
    def __init__(self, nw_ref, mod_ref, gate_row, nxt):
        self.nw = nw_ref[...]
        self.gate = mod_ref[gate_row:gate_row + 1, :]
        self.has_next = nxt is not None
        if self.has_next:
            ng_ref, nmod_ref, sc_row, sh_row = nxt
            self.ng = ng_ref[...]
            self.sc1 = 1.0 + nmod_ref[sc_row:sc_row + 1, :]
            self.sh = nmod_ref[sh_row:sh_row + 1, :]

    def apply(self, out, x):
        xn = x + self.gate * (_rms(out) * self.nw)
        h = _rms(xn) * self.ng * self.sc1 + self.sh if self.has_next else None
        return xn, h


def _mm_tail_kernel(*refs, tp, two_lhs, gate_row, rows_next, moe, ch):
    refs = list(refs)
    ap_ref = refs.pop(0)
    as_ref = refs.pop(0) if two_lhs else None
    w_ref, x_ref, nw_ref, mod_ref = refs[:4]
    refs = refs[4:]
    nxt = None
    if rows_next is not None:
        nxt = (refs[0], refs[1], rows_next[0], rows_next[1])
        refs = refs[2:]
    rt_ref = refs.pop(0) if moe else None
    xo_ref = refs.pop(0)
    h_ref = refs.pop(0) if rows_next is not None else None
    lg_ref = refs.pop(0) if moe else None
    i = pl.program_id(0)
    tail = _TailParams(nw_ref, mod_ref, gate_row, nxt)
    w = w_ref[...]
    chunks = [slice(c * ch, (c + 1) * ch) for c in range(x_ref.shape[0] // ch)]
    outs = []
    for rs in chunks:
        a = ap_ref[rs, :]
        if two_lhs:
            a = jnp.where(i < tp, a, as_ref[rs, :])
        outs.append(jnp.dot(a, w, preferred_element_type=F32))
    for rs, out in zip(chunks, outs):
        xn, h = tail.apply(out, x_ref[rs, :])
        xo_ref[rs, :] = xn
        if moe:
            h_ref[rs, :] = h
            h_hi = h.astype(BF16)
            h_lo = (h - h_hi.astype(F32)).astype(BF16)
            lg_ref[rs, :] = (jnp.dot(h_hi, rt_ref[0], preferred_element_type=F32)
                             + (jnp.dot(h_lo, rt_ref[0], preferred_element_type=F32)
                                + jnp.dot(h_hi, rt_ref[1], preferred_element_type=F32)))
        elif h is not None:
            h_ref[rs, :] = h.astype(h_ref.dtype)


def _mm_tail(a, w_bf, x, rows, mod3, layer, gate_row, ng3, nw_row, nxt, router_pad=None):
    two = isinstance(a, tuple)
    k, d = w_bf.shape
    moe = router_pad is not None
    bm = rows.tile(512, 256, 128) if k <= 2048 else rows.tile(256, 128)
    ch = min(bm, 256)
    tp = rows.n_p // bm
    mt = rows.m // bm
    if two:
        lhs_specs = [pl.BlockSpec((bm, k), lambda i: (jnp.minimum(i, tp - 1), 0)),
                     pl.BlockSpec((bm, k), lambda i: (jnp.maximum(i - tp, 0), 0))]
        lhs = list(a)
    else:
        lhs_specs = [pl.BlockSpec((bm, k), lambda i: (i, 0))]
        lhs = [a]

    def modspec(lyr):
        return pl.BlockSpec((None, 6, d), lambda i: (lyr * rows.nseg + rows.seg(i, bm), 0, 0))

    in_specs = lhs_specs + [
        _const_spec((k, d)),
        pl.BlockSpec((bm, d), lambda i: (i, 0)),
        pl.BlockSpec((None, 1, d), lambda i: (nw_row, 0, 0)),
        modspec(layer),
    ]
    args = lhs + [w_bf, x, ng3, mod3]
    out_specs = [pl.BlockSpec((bm, d), lambda i: (i, 0))]
    out_shape = [jax.ShapeDtypeStruct((rows.m, d), F32)]
    rows_next = None
    if nxt is not None:
        n_row, n_layer, sc_row, sh_row = nxt
        in_specs += [pl.BlockSpec((None, 1, d), lambda i: (n_row, 0, 0)), modspec(n_layer)]
        args += [ng3, mod3]
        rows_next = (sc_row, sh_row)
        if moe:
            in_specs.append(pl.BlockSpec((2, d, LANES), lambda i: (0, 0, 0)))
            r_hi = router_pad.astype(BF16)
            args.append(jnp.stack([r_hi, (router_pad - r_hi.astype(F32)).astype(BF16)]))
            out_specs += [pl.BlockSpec((bm, d), lambda i: (i, 0)), pl.BlockSpec((bm, LANES), lambda i: (i, 0))]
            out_shape += [jax.ShapeDtypeStruct((rows.m, d), F32), jax.ShapeDtypeStruct((rows.m, LANES), F32)]
        else:
            out_specs.append(pl.BlockSpec((bm, d), lambda i: (i, 0)))
            out_shape.append(jax.ShapeDtypeStruct((rows.m, d), BF16))
    res = pl.pallas_call(
        functools.partial(_mm_tail_kernel, tp=tp, two_lhs=two, gate_row=gate_row, rows_next=rows_next, moe=moe,
                          ch=ch),
        grid=(mt,),
        in_specs=in_specs,
        out_specs=out_specs,
        out_shape=out_shape,
        compiler_params=_params("arbitrary"),
        name="matmul_tail",
    )(*args)
    return tuple(res) + (None,) * (3 - len(res))


def _combine_kernel(*refs, tm, nsteps, gate_row, rows_next):
    refs = list(refs)
    pos_ref, nxt_pos_ref, y_hbm, wt_ref, x_ref, nw_ref, mod_ref = refs[:7]
    refs = refs[7:]
    nxt = None
    if rows_next is not None:
        nxt = (refs[0], refs[1], rows_next[0], rows_next[1])
        refs = refs[2:]
    xo_ref = refs.pop(0)
    h_ref = refs.pop(0) if rows_next is not None else None
    buf, sem = refs
    i = pl.program_id(0)

    def tok_copy(b, r, slot, t):
        return pltpu.make_async_copy(y_hbm.at[pl.ds(t, 1), :], buf.at[b, slot, pl.ds(r, 1), :], sem.at[b])

    def fetch(b, idx_ref):
        for slot in range(TOP_K):
            _issue_unrolled(tm, lambda r, slot=slot: tok_copy(b, r, slot, idx_ref[0, slot, r]))

    @pl.when(i == 0)
    def _():
        fetch(0, pos_ref)

    tail = _TailParams(nw_ref, mod_ref, gate_row, nxt)
    for b in (0, 1):
        @pl.when(i % 2 == b)
        def _(b=b):
            @pl.when(i + 1 < nsteps)
            def _():
                fetch(1 - b, nxt_pos_ref)

            for slot in range(TOP_K):
                _wait_unrolled(tm, lambda r, slot=slot: tok_copy(b, r, slot, 0))

            def chunk(r, carry):
                rs = pl.ds(pl.multiple_of(r * TAIL_ROWS, TAIL_ROWS), TAIL_ROWS)
                out = wt_ref[rs, 0:1] * buf[b, 0, rs, :]
                for slot in range(1, TOP_K):
                    out = out + wt_ref[rs, slot:slot + 1] * buf[b, slot, rs, :]
                xn, h = tail.apply(out, x_ref[rs, :])
                xo_ref[rs, :] = xn
                if h is not None:
                    h_ref[rs, :] = h.astype(h_ref.dtype)
                return carry

            lax.fori_loop(0, tm // TAIL_ROWS, chunk, 0)


def _combine_tail(ys3, pos, wts, x, rows, mod3, layer, gate_row, ng3, nw_row, nxt, *, row0, nrows):
    d = ys3.shape[1]
    tm = _pick(math.gcd(row0, nrows) if row0 else nrows, 256, 128)
    t0 = row0 // tm
    nt = nrows // tm

    def modspec(lyr):
        return pl.BlockSpec((None, 6, d), lambda i: (lyr * rows.nseg + rows.seg(t0 + i, tm), 0, 0))

    in_specs = [
        pl.BlockSpec((1, TOP_K, tm), lambda i: (t0 + i, 0, 0), memory_space=pltpu.SMEM),
        pl.BlockSpec((1, TOP_K, tm), lambda i: (t0 + jnp.minimum(i + 1, nt - 1), 0, 0), memory_space=pltpu.SMEM),
        pl.BlockSpec(memory_space=pl.ANY),
        pl.BlockSpec((tm, LANES), lambda i: (t0 + i, 0)),
        pl.BlockSpec((tm, d), lambda i: (t0 + i, 0)),
        pl.BlockSpec((None, 1, d), lambda i: (nw_row, 0, 0)),
        modspec(layer),
    ]
    pos3 = jnp.transpose(pos.reshape(TOP_K, rows.m // tm, tm), (1, 0, 2))
    args = [pos3, pos3, ys3, wts, x, ng3, mod3]
    rows_next = None
    out_specs = [pl.BlockSpec((tm, d), lambda i: (i, 0))]
    out_shape = [jax.ShapeDtypeStruct((nrows, d), F32)]
    if nxt is not None:
        n_row, n_layer, sc_row, sh_row = nxt
        in_specs += [pl.BlockSpec((None, 1, d), lambda i: (n_row, 0, 0)), modspec(n_layer)]
        args += [ng3, mod3]
        out_specs.append(pl.BlockSpec((tm, d), lambda i: (i, 0)))
        out_shape.append(jax.ShapeDtypeStruct((nrows, d), BF16))
        rows_next = (sc_row, sh_row)
    res = pl.pallas_call(
        functools.partial(_combine_kernel, tm=tm, nsteps=nt, gate_row=gate_row, rows_next=rows_next),
        grid=(nt,),
        in_specs=in_specs,
        out_specs=out_specs,
        out_shape=out_shape,
        scratch_shapes=[pltpu.VMEM((2, TOP_K, tm, d), F32), pltpu.SemaphoreType.DMA((2,))],
        compiler_params=_params("arbitrary"),
        name="combine_tail",
    )(*args)
    return (res[0], res[1]) if nxt is not None else (res[0], None)


def _routing_tables(idx, n_experts, tm):
    m = idx.shape[1]
    p = TOP_K * m + n_experts * tm
    e = idx.reshape(-1)
    onehot = (e[:, None] == jnp.arange(n_experts, dtype=jnp.int32)[None, :]).astype(jnp.int32)
    rank = jnp.sum((jnp.cumsum(onehot, axis=0) - 1) * onehot, axis=1)
    counts = jnp.sum(onehot, axis=0)
    padded = ((counts + tm - 1) // tm) * tm
    ends = jnp.cumsum(padded)
    starts = ends - padded
    pos = starts[e] + rank
    n_tiles = p // tm
    tile_start = jnp.arange(n_tiles, dtype=jnp.int32) * tm
    tile_expert = jnp.sum((tile_start[:, None] >= ends[None, :]).astype(jnp.int32), axis=1)
    pad_tiles = jnp.concatenate([jnp.maximum(ends // tm - 1, 0),
                                 jnp.arange(n_tiles - n_experts, n_tiles, dtype=jnp.int32)]).astype(jnp.int32)
    tile_table = jnp.concatenate([jnp.minimum(tile_expert, n_experts - 1), ends[-1:] // tm]).astype(jnp.int32)
    return pos.reshape(TOP_K, m), tile_table, pad_tiles


def kernel(x_prompt, x_sample, cache_k, cache_v, c, c_ctx, ada_w, ada_b, norm_g, attn_w_in, attn_lambda, attn_subln, attn_w_out, gmlp_w_in, gmlp_ln_g, gmlp_ln_b, gmlp_w_s, gmlp_b_s, gmlp_w_out, hyena_w_in, hyena_b_in, hyena_conv_w, hyena_conv_b, hyena_ffn_w1, hyena_ffn_b1, hyena_ffn_w2, hyena_ffn_b2, hyena_ffn_w3, hyena_sin_freq, hyena_bias, hyena_w_out, ffn_w_gate, ffn_w_up, ffn_w_down, moe_router, moe_w_gate, moe_w_up, moe_w_down):
    batch, seq, d = x_prompt.shape
    dec_batch, dec_seq, _ = x_sample.shape
    depth = ada_w.shape[0]
    n_attn = attn_w_in.shape[0]
    past = cache_k.shape[2]
    n_experts = moe_router.shape[-1]
    rows = _Rows(batch * seq, dec_batch * dec_seq, dec_seq)
    n_p, n_s, m = rows.n_p, rows.n_s, rows.m
    heads = d // (2 * HEAD_DIM)

    cond8 = jnp.concatenate([c_ctx[None, :], c, jnp.zeros((8 - rows.nseg, d), F32)], axis=0)
    mod = _modulation(cond8, ada_w, ada_b)[:, :rows.nseg]
    mod3 = mod.reshape(depth * rows.nseg, 6, d)
    ng3 = norm_g.reshape(depth * 4, 1, d)

    x, h = _prenorm_join(x_prompt.reshape(n_p, d), x_sample.reshape(n_s, d), ng3, mod3, rows, 0)

    rope = _rope_tables(dec_seq)
    ck4 = cache_k.reshape(dec_batch, n_attn, past, d)
    cv4 = cache_v.reshape(dec_batch, n_attn, past, d)
    new_k = new_v = None
    y_prompt = y_sample = None

    for i in range(depth):
        kind, j = i % 3, i // 3
        if kind == 0:
            lam_init = 0.8 - 0.6 * math.exp(-0.3 * i)
            q_p = _proj(h, attn_w_in, j, row0=0, nrows=n_p, col0=0, ncols=d)
            new_k = _proj(h, attn_w_in, j, row0=0, nrows=n_p, col0=d, ncols=d, out_dtype=F32,
                          cache=(new_k, n_attn, j, batch, seq))
            new_v = _proj(h, attn_w_in, j, row0=0, nrows=n_p, col0=2 * d, ncols=d, out_dtype=F32,
                          cache=(new_v, n_attn, j, batch, seq))
            qk_s = _proj(h, attn_w_in, j, row0=n_p, nrows=n_s, col0=0, ncols=2 * d, epilogue="rope", rope=rope,
                         seq=dec_seq)
            v_s = _proj(h, attn_w_in, j, row0=n_p, nrows=n_s, col0=2 * d, ncols=d)
            o_p = _attention(q_p, 0, new_k, j, 0, new_v, j, None, attn_lambda, attn_subln, j, lam_init,
                             batch=batch, seq=seq, heads_per_step=heads, bq=seq, lag=4 * heads)
            o_s = _attention(qk_s, 0, qk_s.reshape(dec_batch, 1, dec_seq, 2 * d), 0, d,
                             v_s.reshape(dec_batch, 1, dec_seq, d), 0, (ck4, cv4), attn_lambda, attn_subln, j,
                             lam_init, batch=dec_batch, seq=dec_seq, heads_per_step=min(2, heads),
                             bq=_pick(dec_seq, 256, 128), lag=8)
            mix, w_out = (o_p, o_s), attn_w_out
        elif kind == 1:
            uv = _proj(h, gmlp_w_in, j, row0=0, nrows=m, col0=0, ncols=2 * d, epilogue="gelu")
            mix, w_out = _gmlp_gate(uv, gmlp_ln_g, gmlp_ln_b, gmlp_w_s, gmlp_b_s, j), gmlp_w_out
        else:
            p = _proj(h, hyena_w_in, j, row0=0, nrows=m, col0=0, ncols=3 * d, epilogue="bias", bias=hyena_b_in)
            outs = []
            for row0, nb, length in ((0, batch, seq), (n_p, dec_batch, dec_seq)):
                fmat = _dft_matrix(length)
                a_tab, b_tab = _hyena_filters(length, d, hyena_ffn_w1, hyena_ffn_b1, hyena_ffn_w2, hyena_ffn_b2,
                                              hyena_ffn_w3, hyena_sin_freq, j)
                kh, kn = _hyena_spectrum(fmat, a_tab, b_tab)
                outs.append(_hyena_conv(p, row0, nb, length, fmat, kh, kn, hyena_conv_w, hyena_conv_b, hyena_bias, j))
            mix, w_out = tuple(outs), hyena_w_out

        moe_layer = i % 2 == 1
        jj = i // 2
        router_pad = jnp.pad(moe_router[jj], ((0, 0), (0, LANES - n_experts))) if moe_layer else None
        x, h2, logits = _mm_tail(mix, w_out[j].astype(BF16), x, rows, mod3, i, G1, ng3, i * 4 + 1,
                                 (i * 4 + 2, i, SC2, SH2), router_pad=router_pad)

        last = i == depth - 1
        nxt = None if last else ((i + 1) * 4, i + 1, SC1, SH1)
        if not moe_layer:
            tm = rows.tile(1024, 512, 256, 128)
            one_group = jnp.array([0] * (m // tm) + [m // tm], jnp.int32)
            a = _swiglu_up(h2, ffn_w_gate[:, None], ffn_w_up[:, None], jj, one_group, tm)
            x, h, _ = _mm_tail(a, ffn_w_down[jj].astype(BF16), x, rows, mod3, i, G2, ng3, i * 4 + 3, nxt)
        else:
            tm = rows.tile(512, 256, 128)
            idx, wts = _router(logits, n_experts)
            pos, tile_expert, pad_tiles = _routing_tables(jnp.transpose(idx[:, :TOP_K]), n_experts, tm)
            xs = _dispatch_rows(h2, pos, pad_tiles, TOP_K * m + n_experts * tm, tm)
            a = _swiglu_up(xs, moe_w_gate, moe_w_up, jj, tile_expert, tm, bn=min(768, moe_w_gate.shape[-1]))
            ys = _expert_down(a, moe_w_down, jj, tile_expert, tm)
            tail_args = (ys, pos, wts, x, rows, mod3, i, G2, ng3, i * 4 + 3)
            if last:
                y_prompt, _ = _combine_tail(*tail_args, None, row0=0, nrows=n_p)
                y_sample, _ = _combine_tail(*tail_args, None, row0=n_p, nrows=n_s)
            else:
                x, h = _combine_tail(*tail_args, nxt, row0=0, nrows=m)

    if y_prompt is None:
        y_prompt, y_sample = x[:n_p], x[n_p:]
    y_prompt = y_prompt.reshape(batch, seq, d)
    y_sample = y_sample.reshape(dec_batch, dec_seq, d)
    hk = cache_k.shape[3]
    hv = cache_v.shape[3]
    return (y_prompt, y_sample, new_k.reshape(batch, n_attn, seq, hk, d // hk),
            new_v.reshape(batch, n_attn, seq, hv, d // hv))
```

```python
import functools
import math

import jax
import jax.numpy as jnp
from jax import lax
from jax.experimental import pallas as pl
from jax.experimental.pallas import tpu as pltpu

F32 = jnp.float32
BF16 = jnp.bfloat16

EPS = 1e-6
GRID_W = 64
ROPE_THETA = 10000.0
CHUNK = 128
HYENA_BANDS = 16
HYENA_FAST_DECAY = 0.3
HYENA_SLOW_DECAY = 1.5
HYENA_TARGET = 0.01
HEAD_DIM = 64
TOP_K = 2

V7X_VMEM_BYTES = 64 * 1024 * 1024
VMEM_LIMIT = V7X_VMEM_BYTES - 8 * 1024 * 1024
LANES = 128
TAIL_ROWS = 128
FREQ_ROWS = 512
DMA_UNROLL = 8

SH1, SC1, G1, SH2, SC2, G2 = range(6)


def _pick(n, *cands):
    for c in cands:
        if n % c == 0:
            return c
    return n


def _params(*sem):
    return pltpu.CompilerParams(dimension_semantics=sem, vmem_limit_bytes=VMEM_LIMIT)


def _const_spec(shape):
    nd = len(shape)
    return pl.BlockSpec(shape, lambda *_: (0,) * nd, pipeline_mode=pl.Buffered(1))


def _rms(x):
    return x * lax.rsqrt(jnp.mean(x * x, axis=-1, keepdims=True) + EPS)


class _Rows:
    def __init__(self, n_p, n_s, dec_seq):
        self.n_p, self.n_s, self.dec_seq = n_p, n_s, dec_seq
        self.m = n_p + n_s
        self.nseg = 1 + n_s // dec_seq

    def seg(self, i, bm):
        r = i * bm
        return jnp.where(r < self.n_p, 0, 1 + (r - self.n_p) // self.dec_seq)

    def tile(self, *cands):
        return _pick(math.gcd(self.n_p, self.dec_seq), *cands)


def _mod_kernel(cond_ref, w_ref, b_ref, o_ref):
    s = cond_ref[...]
    s = s * jax.nn.sigmoid(s)
    o_ref[...] = jnp.dot(s.astype(BF16), w_ref[...].astype(BF16), preferred_element_type=F32) + b_ref[...]


def _modulation(cond8, ada_w, ada_b):
    depth, d, n6 = ada_w.shape
    bn = _pick(n6, 1024, 512, 256, 128)
    return pl.pallas_call(
        _mod_kernel,
        grid=(depth, n6 // bn),
        in_specs=[
            pl.BlockSpec((8, d), lambda l, j: (0, 0)),
            pl.BlockSpec((None, d, bn), lambda l, j: (l, 0, j)),
            pl.BlockSpec((None, 1, bn), lambda l, j: (l, 0, j)),
        ],
        out_specs=pl.BlockSpec((None, 8, bn), lambda l, j: (l, 0, j)),
        out_shape=jax.ShapeDtypeStruct((depth, 8, n6), F32),
        compiler_params=_params("arbitrary", "arbitrary"),
        name="modulation",
    )(cond8, ada_w, ada_b.reshape(depth, 1, n6))


def _pre_kernel(xp_ref, xs_ref, g_ref, mod_ref, x_ref, h_ref, *, tp):
    i = pl.program_id(0)

    def emit(src_ref):
        x = src_ref[...]
        x_ref[...] = x
        h = _rms(x) * g_ref[...] * (1.0 + mod_ref[SC1:SC1 + 1, :]) + mod_ref[SH1:SH1 + 1, :]
        h_ref[...] = h.astype(h_ref.dtype)

    @pl.when(i < tp)
    def _():
        emit(xp_ref)

    @pl.when(i >= tp)
    def _():
        emit(xs_ref)


def _prenorm_join(xp, xs, ng3, mod3, rows, layer):
    d = xp.shape[-1]
    bm = rows.tile(512, 256, 128)
    tp = rows.n_p // bm
    ts = rows.n_s // bm
    return pl.pallas_call(
        functools.partial(_pre_kernel, tp=tp),
        grid=(tp + ts,),
        in_specs=[
            pl.BlockSpec((bm, d), lambda i: (jnp.minimum(i, tp - 1), 0)),
            pl.BlockSpec((bm, d), lambda i: (jnp.maximum(i - tp, 0), 0)),
            pl.BlockSpec((None, 1, d), lambda i: (layer * 4, 0, 0)),
            pl.BlockSpec((None, 6, d), lambda i: (layer * rows.nseg + rows.seg(i, bm), 0, 0)),
        ],
        out_specs=[pl.BlockSpec((bm, d), lambda i: (i, 0)), pl.BlockSpec((bm, d), lambda i: (i, 0))],
        out_shape=[jax.ShapeDtypeStruct((rows.m, d), F32), jax.ShapeDtypeStruct((rows.m, d), BF16)],
        compiler_params=_params("arbitrary"),
        name="prenorm_join",
    )(xp, xs, ng3, mod3)


def _proj_kernel(*refs, epilogue, half, cache_rows, cache_slot=None):
    if epilogue == "bias":
        x_ref, w_ref, b_ref, o_ref = refs
    elif epilogue == "rope":
        x_ref, w_ref, cos_ref, sin_ref, o_ref = refs
    else:
        x_ref, w_ref, o_ref = refs
    acc = jnp.dot(x_ref[...], w_ref[...].astype(BF16), preferred_element_type=F32)
    if epilogue == "bias":
        acc = acc + b_ref[...]
    elif epilogue == "gelu":
        acc = 0.5 * acc * (1.0 + lax.erf(acc * (2.0 ** -0.5)))
    if epilogue == "rope":
        cos_t = cos_ref[...]
        sin_t = sin_ref[...]
        lane = lax.broadcasted_iota(jnp.int32, cos_t.shape, 1)
        first = (lane % (2 * half)) < half
        for cblk in range(acc.shape[1] // LANES):
            blk = acc[:, cblk * LANES:(cblk + 1) * LANES]
            partner = jnp.where(first, pltpu.roll(blk, LANES - half, 1), pltpu.roll(blk, half, 1))
            o_ref[:, cblk * LANES:(cblk + 1) * LANES] = (blk * cos_t + partner * sin_t).astype(o_ref.dtype)
    elif cache_slot is not None:
        o_ref[...] = jnp.zeros(o_ref.shape, o_ref.dtype)
        o_ref[:, cache_slot] = acc.reshape(o_ref.shape[0], *o_ref.shape[2:]).astype(o_ref.dtype)
    elif cache_rows:
        o_ref[...] = acc.reshape(o_ref.shape).astype(o_ref.dtype)
    else:
        o_ref[...] = acc.astype(o_ref.dtype)


def _proj(h, w, layer, *, row0, nrows, col0, ncols, out_dtype=BF16, epilogue="none", bias=None,
          rope=None, seq=None, cache=None):
    k = h.shape[1]
    bm = _pick(math.gcd(math.gcd(row0, nrows), seq or 0), 1024, 512, 256, 128)
    bn = _pick(math.gcd(col0, ncols) if col0 else ncols, 1024, 512, 256, 128)
    r0, c0 = row0 // bm, col0 // bn
    in_specs = [
        pl.BlockSpec((bm, k), lambda j, i: (r0 + i, 0)),
        pl.BlockSpec((None, k, bn), lambda j, i: (layer, 0, c0 + j)),
    ]
    args = [h, w]
    half = 0
    if epilogue == "bias":
        in_specs.append(pl.BlockSpec((None, 1, bn), lambda j, i: (layer, 0, c0 + j)))
        args.append(bias.reshape(bias.shape[0], 1, bias.shape[1]))
    elif epilogue == "rope":
        cos_t, sin_t, half = rope
        per_seq = seq // bm
        in_specs += [pl.BlockSpec((bm, LANES), lambda j, i: (i % per_seq, 0))] * 2
        args += [cos_t, sin_t]
    aliases = {}
    first_slot = None
    if cache is None:
        out_spec = pl.BlockSpec((bm, bn), lambda j, i: (i, j))
        out_shape = jax.ShapeDtypeStruct((nrows, ncols), out_dtype)
    else:
        arr, n_slots, slot, batch, cseq = cache
        bb = bm // cseq
        out_shape = jax.ShapeDtypeStruct((batch, n_slots, cseq, ncols), out_dtype)
        if arr is None:
            out_spec = pl.BlockSpec((bb, n_slots, cseq, bn), lambda j, i: (i, 0, 0, j))
            first_slot = slot
        else:
            out_spec = pl.BlockSpec((bb, None, cseq, bn), lambda j, i: (i, slot, 0, j))
            in_specs.append(pl.BlockSpec(memory_space=pl.ANY))
            args.append(arr)
            aliases = {len(args) - 1: 0}

    def body(*refs):
        if aliases:
            refs = refs[:len(args) - 1] + refs[len(args):]
        _proj_kernel(*refs, epilogue=epilogue, half=half, cache_rows=cache is not None, cache_slot=first_slot)

    return pl.pallas_call(
        body,
        grid=(ncols // bn, nrows // bm),
        in_specs=in_specs,
        out_specs=out_spec,
        out_shape=out_shape,
        input_output_aliases=aliases,
        compiler_params=_params("arbitrary", "arbitrary"),
        name="proj_" + epilogue,
    )(*args)


def _rope_tables(dec_seq):
    rows = dec_seq // GRID_W
    row = jnp.repeat(jnp.arange(rows), GRID_W).astype(F32)
    col = jnp.tile(jnp.arange(GRID_W), rows).astype(F32)
    half = HEAD_DIM // 4
    inv = ROPE_THETA ** (-jnp.arange(half, dtype=F32) / half)
    ar = row[:, None] * inv[None, :]
    ac = col[:, None] * inv[None, :]
    cos64 = jnp.concatenate([jnp.cos(ar), jnp.cos(ar), jnp.cos(ac), jnp.cos(ac)], axis=-1)
    sin64 = jnp.concatenate([-jnp.sin(ar), jnp.sin(ar), -jnp.sin(ac), jnp.sin(ac)], axis=-1)
    reps = LANES // HEAD_DIM
    return jnp.tile(cos64, (1, reps)), jnp.tile(sin64, (1, reps)), half


def _attn_kernel(*refs, heads, has_cache, lam_init, lag):
    if has_cache:
        lam_ref, sub_ref, q_ref, k_ref, v_ref, kc_ref, vc_ref, o_ref = refs
    else:
        lam_ref, sub_ref, q_ref, k_ref, v_ref, o_ref = refs
    p = lam_ref[...]
    lam = (jnp.exp(jnp.sum(p[0:1] * p[1:2], axis=-1, keepdims=True))
           - jnp.exp(jnp.sum(p[2:3] * p[3:4], axis=-1, keepdims=True)) + lam_init)
    scale = HEAD_DIM ** -0.5
    assert math.log2(HEAD_DIM) % 2 == 0, "folding the score scale into bf16 queries needs a power-of-two scale"
    nt = (((1,), (1,)), ((), ()))
    head_slices = [slice(h * 2 * HEAD_DIM, (h + 1) * 2 * HEAD_DIM) for h in range(heads)]
    n_units = 2 * heads

    def score_stage(u):
        sl, ms = head_slices[u // 2], slice((u % 2) * HEAD_DIM, (u % 2 + 1) * HEAD_DIM)
        q = q_ref[:, sl][:, ms] * scale
        s = lax.dot_general(q, k_ref[:, sl][:, ms].astype(BF16), nt, preferred_element_type=F32)
        sc = None
        if has_cache:
            sc = lax.dot_general(q, kc_ref[:, sl][:, ms].astype(BF16), nt, preferred_element_type=F32)
        return s, sc

    def exp_stage(s, sc):
        mx = jnp.max(s, axis=-1, keepdims=True)
        if has_cache:
            mx = jnp.maximum(mx, jnp.max(sc, axis=-1, keepdims=True))
        return jnp.exp((s - mx).astype(BF16)), jnp.exp((sc - mx).astype(BF16)) if has_cache else None

    def value_stage(u, e, ec):
        sl = head_slices[u // 2]
        v = v_ref[:, sl].astype(BF16)
        acc = jnp.dot(e, jnp.concatenate([v, jnp.ones_like(v)], axis=1), preferred_element_type=F32)
        if has_cache:
            vc = vc_ref[:, sl].astype(BF16)
            acc = acc + jnp.dot(ec, jnp.concatenate([vc, jnp.ones_like(vc)], axis=1), preferred_element_type=F32)
        vd = acc.shape[1] // 2
        return acc[:, :vd] / acc[:, vd:]

    half = lag // 2
    scores, weights, maps = {}, {}, {}
    for t in range(n_units + lag):
        if 0 <= t - lag:
            maps[t - lag] = value_stage(t - lag, *weights.pop(t - lag))
        if 0 <= t - half < n_units:
            weights[t - half] = exp_stage(*scores.pop(t - half))
        if t < n_units:
            scores[t] = score_stage(t)
    for h, sl in enumerate(head_slices):
        o = maps[2 * h] - lam * maps[2 * h + 1]
        o = _rms(o) * sub_ref[...] * (1.0 - lam_init)
        o_ref[:, sl] = o.astype(o_ref.dtype)


def _attention(q, q_col0, k4, k_slot, k_col0, v4, v_slot, cache4, lam, subln, layer_j, lam_init, *, batch, seq,
               heads_per_step, bq, lag):
    d = v4.shape[-1]
    hw = heads_per_step * 2 * HEAD_DIM
    nk = k4.shape[2]
    per_seq = seq // bq
    qc0, kc0 = q_col0 // hw, k_col0 // hw
    in_specs = [
        pl.BlockSpec((None, 4, HEAD_DIM), lambda b, h, i: (layer_j, 0, 0)),
        pl.BlockSpec((None, 1, 2 * HEAD_DIM), lambda b, h, i: (layer_j, 0, 0)),
        pl.BlockSpec((bq, hw), lambda b, h, i: (b * per_seq + i, qc0 + h)),
        pl.BlockSpec((None, None, nk, hw), lambda b, h, i: (b, k_slot, 0, kc0 + h)),
        pl.BlockSpec((None, None, nk, hw), lambda b, h, i: (b, v_slot, 0, h)),
    ]
    args = [lam, subln.reshape(subln.shape[0], 1, subln.shape[1]), q, k4, v4]
    if cache4 is not None:
        ck, cv = cache4
        nc = ck.shape[2]
        in_specs += [pl.BlockSpec((None, None, nc, hw), lambda b, h, i: (b, layer_j, 0, h))] * 2
        args += [ck, cv]
    return pl.pallas_call(
        functools.partial(_attn_kernel, heads=heads_per_step, has_cache=cache4 is not None, lam_init=lam_init,
                          lag=lag),
        grid=(batch, d // hw, per_seq),
        in_specs=in_specs,
        out_specs=pl.BlockSpec((bq, hw), lambda b, h, i: (b * per_seq + i, h)),
        out_shape=jax.ShapeDtypeStruct((batch * seq, d), BF16),
        compiler_params=_params("arbitrary", "arbitrary", "arbitrary"),
        name="diff_attention",
    )(*args)


def _gmlp_kernel(u_ref, v_ref, g_ref, b_ref, ws_ref, bs_ref, o_ref, *, chunks, groups, gd):
    v = v_ref[...].astype(F32)
    mu = jnp.mean(v, axis=-1, keepdims=True)
    vc = v - mu
    var = jnp.mean(vc * vc, axis=-1, keepdims=True)
    vn = (vc * lax.rsqrt(var + EPS) * g_ref[...] + b_ref[...]).astype(BF16)
    for g in range(groups):
        w = ws_ref[g].astype(BF16)
        cs = slice(g * gd, (g + 1) * gd)
        for c in range(chunks):
            rs = slice(c * CHUNK, (c + 1) * CHUNK)
            vm = jnp.dot(w, vn[rs, cs], preferred_element_type=F32) + bs_ref[:, cs]
            o_ref[rs, cs] = (u_ref[rs, cs].astype(F32) * vm).astype(o_ref.dtype)


def _gmlp_gate(uv, ln_g, ln_b, w_s, b_s, layer_j):
    m, d2 = uv.shape
    d = d2 // 2
    groups = w_s.shape[1]
    gd = d // groups
    tm = _pick(m, 512, 256, 128)
    bs_full = jnp.repeat(jnp.transpose(b_s[layer_j]), gd, axis=1)
    return pl.pallas_call(
        functools.partial(_gmlp_kernel, chunks=tm // CHUNK, groups=groups, gd=gd),
        grid=(m // tm,),
        in_specs=[
            pl.BlockSpec((tm, d), lambda i: (i, 0)),
            pl.BlockSpec((tm, d), lambda i: (i, 1)),
            pl.BlockSpec((None, 1, d), lambda i: (layer_j, 0, 0)),
            pl.BlockSpec((None, 1, d), lambda i: (layer_j, 0, 0)),
            pl.BlockSpec((None, groups, CHUNK, CHUNK), lambda i: (layer_j, 0, 0, 0)),
            pl.BlockSpec((CHUNK, d), lambda i: (0, 0)),
        ],
        out_specs=pl.BlockSpec((tm, d), lambda i: (i, 0)),
        out_shape=jax.ShapeDtypeStruct((m, d), BF16),
        compiler_params=_params("arbitrary"),
        name="gmlp_gate",
    )(uv, uv, ln_g.reshape(ln_g.shape[0], 1, d), ln_b.reshape(ln_b.shape[0], 1, d), w_s, bs_full)


def _dft_matrix(length):
    n = 2 * length
    split = _pick(length, 32, 16, 8)
    f = jnp.arange(length, dtype=jnp.int32)[:, None]

    def cos_sin(t):
        ang = ((f * t[None, :]) % n).astype(F32) * (2.0 * math.pi / n)
        return jnp.cos(ang), jnp.sin(ang)

    ca, sa = cos_sin(jnp.arange(length // split, dtype=jnp.int32) * split)
    cb, sb = cos_sin(jnp.arange(split, dtype=jnp.int32))
    cos_ft = (ca[:, :, None] * cb[:, None, :] - sa[:, :, None] * sb[:, None, :]).reshape(length, length)
    sin_ft = (sa[:, :, None] * cb[:, None, :] + ca[:, :, None] * sb[:, None, :]).reshape(length, length)
    return jnp.concatenate([cos_ft, -sin_ft], axis=0).astype(BF16)


def _filter_features(length):
    pos = jnp.arange(length, dtype=F32)
    t = jnp.linspace(0.0, 1.0, length, dtype=F32)[:, None]
    bands = jnp.linspace(1e-4, HYENA_BANDS - 1, HYENA_BANDS, dtype=F32)
    ang = (2.0 * math.pi / length) * pos[:, None] * bands[None, :]
    z = jnp.concatenate([t, jnp.cos(ang), -jnp.sin(ang)], axis=-1)
    return jnp.pad(z, ((0, 0), (0, LANES - z.shape[1])))


def _filt_kernel(z_ref, w1_ref, b1_ref, w2_ref, b2_ref, fr_ref, w00, w01, w10, w11, dl_ref, a_ref, b_ref):
    hi = lax.Precision.HIGHEST
    z = z_ref[...]
    h = jnp.sin(fr_ref[0:1, :] * (jnp.dot(z, w1_ref[...], precision=hi, preferred_element_type=F32) + b1_ref[...]))
    h = jnp.sin(fr_ref[1:2, :] * (jnp.dot(h, w2_ref[...], precision=hi, preferred_element_type=F32) + b2_ref[...]))
    hb = h.astype(BF16)
    decay = jnp.exp(-z[:, 0:1] * dl_ref[...])
    row = lax.broadcasted_iota(jnp.int32, (z.shape[0], 1), 0)
    for o, (wf, wb) in enumerate(((w00, w01), (w10, w11))):
        fwd = jnp.dot(hb, wf[...].astype(BF16), preferred_element_type=F32) * decay
        bwd = jnp.dot(hb, wb[...].astype(BF16), preferred_element_type=F32) * decay
        bwd = jnp.where(row == 0, 0.0, bwd)
        norm = (jnp.sum(jnp.abs(fwd), axis=0, keepdims=True) + jnp.sum(jnp.abs(bwd), axis=0, keepdims=True) + EPS)
        inv = 1.0 / norm
        a_ref[o] = ((fwd + bwd) * inv).astype(a_ref.dtype)
        b_ref[o] = ((fwd - bwd) * inv).astype(b_ref.dtype)


def _hyena_filters(length, d, w1, b1, w2, b2, w3, freq, layer_j):
    fh = w2.shape[-1]
    emb = w1.shape[1]
    bd = _pick(d, 512, 256, 128)
    nb = d // bd
    zfeat = _filter_features(length)
    w1p = jnp.pad(w1[layer_j], ((0, LANES - emb), (0, 0)))
    max_decay = math.log(HYENA_TARGET) / HYENA_FAST_DECAY
    min_decay = math.log(HYENA_TARGET) / HYENA_SLOW_DECAY
    deltas = jnp.abs(jnp.linspace(min_decay, max_decay, d, dtype=F32))[None, :]
    w3_specs = [pl.BlockSpec((None, fh, bd), functools.partial(lambda c, g: (layer_j, 0, g * nb + c), g=g))
                for g in range(4)]
    out_spec = pl.BlockSpec((2, length, bd), lambda c: (0, 0, c))
    return pl.pallas_call(
        _filt_kernel,
        grid=(nb,),
        in_specs=[
            pl.BlockSpec((length, LANES), lambda c: (0, 0)),
            pl.BlockSpec((LANES, fh), lambda c: (0, 0)),
            pl.BlockSpec((None, 1, fh), lambda c: (layer_j, 0, 0)),
            pl.BlockSpec((None, fh, fh), lambda c: (layer_j, 0, 0)),
            pl.BlockSpec((None, 1, fh), lambda c: (layer_j, 0, 0)),
            pl.BlockSpec((None, 2, fh), lambda c: (layer_j, 0, 0)),
            *w3_specs,
            pl.BlockSpec((1, bd), lambda c: (0, c)),
        ],
        out_specs=[out_spec, out_spec],
        out_shape=[jax.ShapeDtypeStruct((2, length, d), BF16)] * 2,
        compiler_params=_params("arbitrary"),
        name="hyena_filters",
    )(zfeat, w1p, b1.reshape(b1.shape[0], 1, fh), w2, b2.reshape(b2.shape[0], 1, fh), freq, w3, w3, w3, w3, deltas)


def _spectrum_kernel(f_ref, a_ref, b_ref, kh_ref, kn_ref, *, length):
    n = 2 * length
    row = lax.broadcasted_iota(jnp.int32, (length, 1), 0)
    wn = jnp.where(row == 0, 1.0 / n, 2.0 / n)
    sgn = jnp.where(row % 2 == 0, 1.0, -1.0)
    a = a_ref[...]
    kh_ref[0:length, :] = jnp.dot(f_ref[0:length, :], a, preferred_element_type=F32) * wn
    kh_ref[length:n, :] = jnp.dot(f_ref[length:n, :], b_ref[...], preferred_element_type=F32) * wn
    nyq = jnp.sum(a.astype(F32) * sgn, axis=0, keepdims=True) * (1.0 / n)
    kn_ref[...] = jnp.broadcast_to(nyq, kn_ref.shape)


def _hyena_spectrum(fmat, a_tab, b_tab):
    _, length, d = a_tab.shape
    bd = _pick(d, 256, 128)
    tab_spec = pl.BlockSpec((None, length, bd), lambda o, c: (o, 0, c))
    return pl.pallas_call(
        functools.partial(_spectrum_kernel, length=length),
        grid=(2, d // bd),
        in_specs=[_const_spec((2 * length, length)), tab_spec, tab_spec],
        out_specs=[pl.BlockSpec((None, 2 * length, bd), lambda o, c: (o, 0, c)),
                   pl.BlockSpec((None, 8, bd), lambda o, c: (o, 0, c))],
        out_shape=[jax.ShapeDtypeStruct((2, 2 * length, d), F32), jax.ShapeDtypeStruct((2, 8, d), F32)],
        compiler_params=_params("arbitrary", "arbitrary"),
        name="hyena_spectrum",
    )(fmat, a_tab, b_tab)


def _hconv_kernel(f_ref, kh_ref, kn_ref, pv_ref, p1_ref, p2_ref, cwv, cw1, cw2, cbv, cb1, cb2, hb_ref, o_ref, y_scr,
                  *, length):
    n = 2 * length
    row = lax.broadcasted_iota(jnp.int32, (length, 1), 0)
    sgn = jnp.where(row % 2 == 0, 1.0, -1.0)

    seqs = [slice(b * length, (b + 1) * length) for b in range(o_ref.shape[0] // length)]

    def short_conv(p_ref, cw, cb, seq):
        p = p_ref[seq, :].astype(F32)
        prev = jnp.where(row == 0, 0.0, pltpu.roll(p, 1, 0))
        nxt = jnp.where(row == length - 1, 0.0, pltpu.roll(p, length - 1, 0))
        return cb[...] + (prev * cw[0:1, :] + p * cw[1:2, :] + nxt * cw[2:3, :])

    fb = min(length, FREQ_ROWS)

    def to_spectrum(z, o, b):
        zb = z.astype(BF16)

        def freq_block(i, carry):
            re = pl.ds(pl.multiple_of(i * fb, fb), fb)
            im = pl.ds(pl.multiple_of(length + i * fb, fb), fb)
            zr = jnp.dot(f_ref[re, :], zb, preferred_element_type=F32)
            zi = jnp.dot(f_ref[im, :], zb, preferred_element_type=F32)
            kr, ki = kh_ref[o, re, :], kh_ref[o, im, :]
            y_scr[pl.ds(pl.multiple_of(b * n + i * fb, fb), fb), :] = (zr * kr - zi * ki).astype(BF16)
            y_scr[pl.ds(pl.multiple_of(b * n + length + i * fb, fb), fb), :] = (zr * ki + zi * kr).astype(BF16)
            return carry

        lax.fori_loop(0, length // fb, freq_block, 0)
        return jnp.sum(zb.astype(F32) * sgn, axis=0, keepdims=True) * kn_ref[o, 0:1, :]

    def from_spectrum(z, nyq, o, b):
        y = (jnp.dot(f_ref[0:length, :], y_scr[b * n:b * n + length, :], preferred_element_type=F32)
             + jnp.dot(f_ref[length:n, :], y_scr[b * n + length:(b + 1) * n, :], preferred_element_type=F32)
             + sgn * nyq)
        return y + z * hb_ref[o:o + 1, :]

    z = [short_conv(pv_ref, cwv, cbv, seq) for seq in seqs]
    for o, (p_ref, cw, cb) in enumerate(((p1_ref, cw1, cb1), (p2_ref, cw2, cb2))):
        nyq = [to_spectrum(z[b], o, b) for b in range(len(seqs))]
        gate = [short_conv(p_ref, cw, cb, seq) for seq in seqs]
        z = [gate[b] * from_spectrum(z[b], nyq[b], o, b) for b in range(len(seqs))]
    for b, seq in enumerate(seqs):
        o_ref[seq, :] = z[b].astype(o_ref.dtype)


def _hyena_conv(p, row0, batch, length, fmat, kh, kn, conv_w, conv_b, hbias, layer_j):
    d = p.shape[1] // 3
    bd = _pick(d, 256, 128) if length > 512 else _pick(d, 512, 256, 128)
    nb = d // bd
    spb = 2 if length <= 512 and batch % 2 == 0 and (row0 // length) % 2 == 0 else 1
    r0 = row0 // (spb * length)

    def pspec(g):
        return pl.BlockSpec((spb * length, bd), lambda c, b: (r0 + b, g * nb + c))

    def wspec(g, rows_):
        return pl.BlockSpec((None, rows_, bd), lambda c, b: (layer_j, 0, g * nb + c))

    return pl.pallas_call(
        functools.partial(_hconv_kernel, length=length),
        grid=(nb, batch // spb),
        in_specs=[
            _const_spec((2 * length, length)),
            pl.BlockSpec((2, 2 * length, bd), lambda c, b: (0, 0, c), pipeline_mode=pl.Buffered(1)),
            pl.BlockSpec((2, 8, bd), lambda c, b: (0, 0, c)),
            pspec(0), pspec(1), pspec(2),
            wspec(0, 3), wspec(1, 3), wspec(2, 3),
            wspec(0, 1), wspec(1, 1), wspec(2, 1),
            pl.BlockSpec((None, 2, bd), lambda c, b: (layer_j, 0, c)),
        ],
        out_specs=pl.BlockSpec((spb * length, bd), lambda c, b: (b, c)),
        out_shape=jax.ShapeDtypeStruct((batch * length, d), BF16),
        scratch_shapes=[pltpu.VMEM((spb * 2 * length, bd), BF16)],
        compiler_params=_params("arbitrary", "arbitrary"),
        name="hyena_conv",
    )(fmat, kh, kn, p, p, p, conv_w, conv_w, conv_w, conv_b.reshape(conv_b.shape[0], 1, -1),
      conv_b.reshape(conv_b.shape[0], 1, -1), conv_b.reshape(conv_b.shape[0], 1, -1), hbias)


def _tile_is_active(te_ref):
    return pl.program_id(1) < te_ref[pl.num_programs(1)]


def _up_kernel(te_ref, x_ref, wg_ref, wu_ref, o_ref):
    @pl.when(_tile_is_active(te_ref))
    def _():
        x = x_ref[...].astype(BF16)
        cw = _pick(o_ref.shape[1], 512, 256, 128)
        for c0 in range(0, o_ref.shape[1], cw):
            cs = slice(c0, c0 + cw)
            g = jnp.dot(x, wg_ref[:, cs].astype(BF16), preferred_element_type=F32)
            u = jnp.dot(x, wu_ref[:, cs].astype(BF16), preferred_element_type=F32)
            o_ref[:, cs] = (g * jax.nn.sigmoid(g) * u).astype(o_ref.dtype)

    @pl.when(jnp.logical_not(_tile_is_active(te_ref)))
    def _():
        o_ref[...] = jnp.zeros(o_ref.shape, o_ref.dtype)


def _swiglu_up(x, wg4, wu4, layer, tile_expert, tm, bn=None):
    rows_, d = x.shape
    f = wg4.shape[-1]
    bn = bn or _pick(f, 512, 256, 128)
    wspec = pl.BlockSpec((None, None, d, bn), lambda j, i, te: (layer, te[i], 0, j))
    return pl.pallas_call(
        _up_kernel,
        grid_spec=pltpu.PrefetchScalarGridSpec(
            num_scalar_prefetch=1,
            grid=(pl.cdiv(f, bn), rows_ // tm),
            in_specs=[pl.BlockSpec((tm, d), lambda j, i, te: (i, 0)), wspec, wspec],
            out_specs=pl.BlockSpec((tm, bn), lambda j, i, te: (i, j)),
        ),
        out_shape=jax.ShapeDtypeStruct((rows_, f), BF16),
        compiler_params=_params("arbitrary", "arbitrary"),
        name="swiglu_up",
    )(tile_expert, x, wg4, wu4)


def _down_kernel(te_ref, a_ref, w_ref, o_ref):
    @pl.when(_tile_is_active(te_ref))
    def _():
        o_ref[...] = jnp.dot(a_ref[...], w_ref[...].astype(BF16), preferred_element_type=F32)

    @pl.when(jnp.logical_not(_tile_is_active(te_ref)))
    def _():
        o_ref[...] = jnp.zeros(o_ref.shape, o_ref.dtype)


def _expert_down(a, wd4, layer, tile_expert, tm):
    rows_, f = a.shape
    d = wd4.shape[-1]
    bn = _pick(d, 1024, 512, 256, 128)
    return pl.pallas_call(
        _down_kernel,
        grid_spec=pltpu.PrefetchScalarGridSpec(
            num_scalar_prefetch=1,
            grid=(d // bn, rows_ // tm),
            in_specs=[
                pl.BlockSpec((tm, f), lambda j, i, te: (i, 0)),
                pl.BlockSpec((None, None, f, bn), lambda j, i, te: (layer, te[i], 0, j)),
            ],
            out_specs=pl.BlockSpec((tm, bn), lambda j, i, te: (i, j)),
        ),
        out_shape=jax.ShapeDtypeStruct((rows_, d), F32),
        compiler_params=_params("arbitrary", "arbitrary"),
        name="expert_down",
    )(tile_expert, a, wd4)


def _router_kernel(lg_ref, idx_ref, w_ref, *, n_experts):
    lane = lax.broadcasted_iota(jnp.int32, lg_ref.shape, 1)
    logits = jnp.where(lane < n_experts, lg_ref[...], -jnp.inf)
    m1 = jnp.max(logits, axis=-1, keepdims=True)
    i1 = jnp.min(jnp.where(logits == m1, lane, LANES), axis=-1, keepdims=True)
    rest = jnp.where(lane == i1, -jnp.inf, logits)
    m2 = jnp.max(rest, axis=-1, keepdims=True)
    i2 = jnp.min(jnp.where(rest == m2, lane, LANES), axis=-1, keepdims=True)
    e2 = jnp.exp(m2 - m1)
    den = 1.0 + e2
    idx_ref[...] = jnp.where(lane == 0, i1, jnp.where(lane == 1, i2, 0))
    w_ref[...] = jnp.where(lane == 0, 1.0 / den, jnp.where(lane == 1, e2 / den, 0.0))


def _router(logits, n_experts):
    m = logits.shape[0]
    tm = _pick(m, 1024, 512, 256, 128)
    spec = pl.BlockSpec((tm, LANES), lambda i: (i, 0))
    return pl.pallas_call(
        functools.partial(_router_kernel, n_experts=n_experts),
        grid=(m // tm,),
        in_specs=[spec],
        out_specs=[spec, spec],
        out_shape=[jax.ShapeDtypeStruct((m, LANES), jnp.int32), jax.ShapeDtypeStruct((m, LANES), F32)],
        compiler_params=_params("arbitrary"),
        name="router",
    )(logits)


def _issue_unrolled(n, copy_of):
    def body(g, carry):
        for u in range(DMA_UNROLL):
            copy_of(g * DMA_UNROLL + u).start(priority=u % 2)
        return carry

    lax.fori_loop(0, n // DMA_UNROLL, body, 0)


def _wait_unrolled(n, copy_of):
    def body(g, carry):
        for u in range(DMA_UNROLL):
            copy_of(g * DMA_UNROLL + u).wait()
        return carry

    lax.fori_loop(0, n // DMA_UNROLL, body, 0)


def _dispatch_kernel(pad_ref, pos_ref, h_ref, xs_hbm, zbuf, sem, zsem, *, tm, n_pad):
    @pl.when(pl.program_id(0) == 0)
    def _():
        zbuf[...] = jnp.zeros(zbuf.shape, zbuf.dtype)
        for z in range(n_pad):
            zero = pltpu.make_async_copy(zbuf, xs_hbm.at[pl.ds(pl.multiple_of(pad_ref[z] * tm, tm), tm), :], zsem)
            zero.start()
            zero.wait()

    def row_copy(r, slot, t):
        return pltpu.make_async_copy(h_ref.at[pl.ds(r, 1), :], xs_hbm.at[pl.ds(t, 1), :], sem)

    for slot in range(TOP_K):
        _issue_unrolled(tm, lambda r, slot=slot: row_copy(r, slot, pos_ref[0, slot, r]))
    for slot in range(TOP_K):
        _wait_unrolled(tm, lambda r, slot=slot: row_copy(r, slot, 0))


def _dispatch_rows(h, pos, pad_tiles, n_rows, tm):
    m, d = h.shape
    n_pad = pad_tiles.shape[0]
    pos3 = jnp.transpose(pos.reshape(TOP_K, m // tm, tm), (1, 0, 2))
    return pl.pallas_call(
        functools.partial(_dispatch_kernel, tm=tm, n_pad=n_pad),
        grid_spec=pltpu.PrefetchScalarGridSpec(
            num_scalar_prefetch=1,
            grid=(m // tm,),
            in_specs=[
                pl.BlockSpec((1, TOP_K, tm), lambda i, pad: (i, 0, 0), memory_space=pltpu.SMEM),
                pl.BlockSpec((tm, d), lambda i, pad: (i, 0)),
            ],
            out_specs=pl.BlockSpec(memory_space=pl.ANY),
            scratch_shapes=[pltpu.VMEM((tm, d), F32), pltpu.SemaphoreType.DMA(()), pltpu.SemaphoreType.DMA(())],
        ),
        out_shape=jax.ShapeDtypeStruct((n_rows, d), F32),
        compiler_params=_params("arbitrary"),
        name="dispatch_rows",
    )(pad_tiles, pos3, h)


class _TailParams:
    def __init__(self, nw_ref, mod_ref, gate_row, nxt):
        self.nw = nw_ref[...]
        self.gate = mod_ref[gate_row:gate_row + 1, :]
        self.has_next = nxt is not None
        if self.has_next:
            ng_ref, nmod_ref, sc_row, sh_row = nxt
            self.ng = ng_ref[...]
            self.sc1 = 1.0 + nmod_ref[sc_row:sc_row + 1, :]
            self.sh = nmod_ref[sh_row:sh_row + 1, :]

    def apply(self, out, x):
        xn = x + self.gate * (_rms(out) * self.nw)
        h = _rms(xn) * self.ng * self.sc1 + self.sh if self.has_next else None
        return xn, h


def _mm_tail_kernel(*refs, tp, two_lhs, gate_row, rows_next, moe, ch):
    refs = list(refs)
    ap_ref = refs.pop(0)
    as_ref = refs.pop(0) if two_lhs else None
    w_ref, x_ref, nw_ref, mod_ref = refs[:4]
    refs = refs[4:]
    nxt = None
    if rows_next is not None:
        nxt = (refs[0], refs[1], rows_next[0], rows_next[1])
        refs = refs[2:]
    rt_ref = refs.pop(0) if moe else None
    xo_ref = refs.pop(0)
    h_ref = refs.pop(0) if rows_next is not None else None
    lg_ref = refs.pop(0) if moe else None
    i = pl.program_id(0)
    tail = _TailParams(nw_ref, mod_ref, gate_row, nxt)
    w = w_ref[...]
    chunks = [slice(c * ch, (c + 1) * ch) for c in range(x_ref.shape[0] // ch)]
    outs = []
    for rs in chunks:
        a = ap_ref[rs, :]
        if two_lhs:
            a = jnp.where(i < tp, a, as_ref[rs, :])
        outs.append(jnp.dot(a, w, preferred_element_type=F32))
    for rs, out in zip(chunks, outs):
        xn, h = tail.apply(out, x_ref[rs, :])
        xo_ref[rs, :] = xn
        if moe:
            h_ref[rs, :] = h
            h_hi = h.astype(BF16)
            h_lo = (h - h_hi.astype(F32)).astype(BF16)
            lg_ref[rs, :] = (jnp.dot(h_hi, rt_ref[0], preferred_element_type=F32)
                             + (jnp.dot(h_lo, rt_ref[0], preferred_element_type=F32)
                                + jnp.dot(h_hi, rt_ref[1], preferred_element_type=F32)))
        elif h is not None:
            h_ref[rs, :] = h.astype(h_ref.dtype)


def _mm_tail(a, w_bf, x, rows, mod3, layer, gate_row, ng3, nw_row, nxt, router_pad=None):
    two = isinstance(a, tuple)
    k, d = w_bf.shape
    moe = router_pad is not None
    bm = rows.tile(512, 256, 128) if k <= 2048 else rows.tile(256, 128)
    ch = min(bm, 256)
    tp = rows.n_p // bm
    mt = rows.m // bm
    if two:
        lhs_specs = [pl.BlockSpec((bm, k), lambda i: (jnp.minimum(i, tp - 1), 0)),
                     pl.BlockSpec((bm, k), lambda i: (jnp.maximum(i - tp, 0), 0))]
        lhs = list(a)
    else:
        lhs_specs = [pl.BlockSpec((bm, k), lambda i: (i, 0))]
        lhs = [a]

    def modspec(lyr):
        return pl.BlockSpec((None, 6, d), lambda i: (lyr * rows.nseg + rows.seg(i, bm), 0, 0))

    in_specs = lhs_specs + [
        _const_spec((k, d)),
        pl.BlockSpec((bm, d), lambda i: (i, 0)),
        pl.BlockSpec((None, 1, d), lambda i: (nw_row, 0, 0)),
        modspec(layer),
    ]
    args = lhs + [w_bf, x, ng3, mod3]
    out_specs = [pl.BlockSpec((bm, d), lambda i: (i, 0))]
    out_shape = [jax.ShapeDtypeStruct((rows.m, d), F32)]
    rows_next = None
    if nxt is not None:
        n_row, n_layer, sc_row, sh_row = nxt
        in_specs += [pl.BlockSpec((None, 1, d), lambda i: (n_row, 0, 0)), modspec(n_layer)]
        args += [ng3, mod3]
        rows_next = (sc_row, sh_row)
        if moe:
            in_specs.append(pl.BlockSpec((2, d, LANES), lambda i: (0, 0, 0)))
            r_hi = router_pad.astype(BF16)
            args.append(jnp.stack([r_hi, (router_pad - r_hi.astype(F32)).astype(BF16)]))
            out_specs += [pl.BlockSpec((bm, d), lambda i: (i, 0)), pl.BlockSpec((bm, LANES), lambda i: (i, 0))]
            out_shape += [jax.ShapeDtypeStruct((rows.m, d), F32), jax.ShapeDtypeStruct((rows.m, LANES), F32)]
        else:
            out_specs.append(pl.BlockSpec((bm, d), lambda i: (i, 0)))
            out_shape.append(jax.ShapeDtypeStruct((rows.m, d), BF16))
    res = pl.pallas_call(
        functools.partial(_mm_tail_kernel, tp=tp, two_lhs=two, gate_row=gate_row, rows_next=rows_next, moe=moe,
                          ch=ch),
        grid=(mt,),
        in_specs=in_specs,
        out_specs=out_specs,
        out_shape=out_shape,
        compiler_params=_params("arbitrary"),
        name="matmul_tail",
    )(*args)
    return tuple(res) + (None,) * (3 - len(res))


def _combine_kernel(*refs, tm, nsteps, gate_row, rows_next):
    refs = list(refs)
    pos_ref, nxt_pos_ref, y_hbm, wt_ref, x_ref, nw_ref, mod_ref = refs[:7]
    refs = refs[7:]
    nxt = None
    if rows_next is not None:
        nxt = (refs[0], refs[1], rows_next[0], rows_next[1])
        refs = refs[2:]
    xo_ref = refs.pop(0)
    h_ref = refs.pop(0) if rows_next is not None else None
    buf, sem = refs
    i = pl.program_id(0)

    def tok_copy(b, r, slot, t):
        return pltpu.make_async_copy(y_hbm.at[pl.ds(t, 1), :], buf.at[b, slot, pl.ds(r, 1), :], sem.at[b])

    def fetch(b, idx_ref):
        for slot in range(TOP_K):
            _issue_unrolled(tm, lambda r, slot=slot: tok_copy(b, r, slot, idx_ref[0, slot, r]))

    @pl.when(i == 0)
    def _():
        fetch(0, pos_ref)

    tail = _TailParams(nw_ref, mod_ref, gate_row, nxt)
    for b in (0, 1):
        @pl.when(i % 2 == b)
        def _(b=b):
            @pl.when(i + 1 < nsteps)
            def _():
                fetch(1 - b, nxt_pos_ref)

            for slot in range(TOP_K):
                _wait_unrolled(tm, lambda r, slot=slot: tok_copy(b, r, slot, 0))

            def chunk(r, carry):
                rs = pl.ds(pl.multiple_of(r * TAIL_ROWS, TAIL_ROWS), TAIL_ROWS)
                out = wt_ref[rs, 0:1] * buf[b, 0, rs, :]
                for slot in range(1, TOP_K):
                    out = out + wt_ref[rs, slot:slot + 1] * buf[b, slot, rs, :]
                xn, h = tail.apply(out, x_ref[rs, :])
                xo_ref[rs, :] = xn
                if h is not None:
                    h_ref[rs, :] = h.astype(h_ref.dtype)
                return carry

            lax.fori_loop(0, tm // TAIL_ROWS, chunk, 0)


def _combine_tail(ys3, pos, wts, x, rows, mod3, layer, gate_row, ng3, nw_row, nxt, *, row0, nrows):
    d = ys3.shape[1]
    tm = _pick(math.gcd(row0, nrows) if row0 else nrows, 256, 128)
    t0 = row0 // tm
    nt = nrows // tm

    def modspec(lyr):
        return pl.BlockSpec((None, 6, d), lambda i: (lyr * rows.nseg + rows.seg(t0 + i, tm), 0, 0))

    in_specs = [
        pl.BlockSpec((1, TOP_K, tm), lambda i: (t0 + i, 0, 0), memory_space=pltpu.SMEM),
        pl.BlockSpec((1, TOP_K, tm), lambda i: (t0 + jnp.minimum(i + 1, nt - 1), 0, 0), memory_space=pltpu.SMEM),
        pl.BlockSpec(memory_space=pl.ANY),
        pl.BlockSpec((tm, LANES), lambda i: (t0 + i, 0)),
        pl.BlockSpec((tm, d), lambda i: (t0 + i, 0)),
        pl.BlockSpec((None, 1, d), lambda i: (nw_row, 0, 0)),
        modspec(layer),
    ]
    pos3 = jnp.transpose(pos.reshape(TOP_K, rows.m // tm, tm), (1, 0, 2))
    args = [pos3, pos3, ys3, wts, x, ng3, mod3]
    rows_next = None
    out_specs = [pl.BlockSpec((tm, d), lambda i: (i, 0))]
    out_shape = [jax.ShapeDtypeStruct((nrows, d), F32)]
    if nxt is not None:
        n_row, n_layer, sc_row, sh_row = nxt
        in_specs += [pl.BlockSpec((None, 1, d), lambda i: (n_row, 0, 0)), modspec(n_layer)]
        args += [ng3, mod3]
        out_specs.append(pl.BlockSpec((tm, d), lambda i: (i, 0)))
        out_shape.append(jax.ShapeDtypeStruct((nrows, d), BF16))
        rows_next = (sc_row, sh_row)
    res = pl.pallas_call(
        functools.partial(_combine_kernel, tm=tm, nsteps=nt, gate_row=gate_row, rows_next=rows_next),
        grid=(nt,),
        in_specs=in_specs,
        out_specs=out_specs,
        out_shape=out_shape,
        scratch_shapes=[pltpu.VMEM((2, TOP_K, tm, d), F32), pltpu.SemaphoreType.DMA((2,))],
        compiler_params=_params("arbitrary"),
        name="combine_tail",
    )(*args)
    return (res[0], res[1]) if nxt is not None else (res[0], None)


def _routing_tables(idx, n_experts, tm):
    m = idx.shape[1]
    p = TOP_K * m + n_experts * tm
    e = idx.reshape(-1)
    onehot = (e[:, None] == jnp.arange(n_experts, dtype=jnp.int32)[None, :]).astype(jnp.int32)
    rank = jnp.sum((jnp.cumsum(onehot, axis=0) - 1) * onehot, axis=1)
    counts = jnp.sum(onehot, axis=0)
    padded = ((counts + tm - 1) // tm) * tm
    ends = jnp.cumsum(padded)
    starts = ends - padded
    pos = starts[e] + rank
    n_tiles = p // tm
    tile_start = jnp.arange(n_tiles, dtype=jnp.int32) * tm
    tile_expert = jnp.sum((tile_start[:, None] >= ends[None, :]).astype(jnp.int32), axis=1)
    pad_tiles = jnp.concatenate([jnp.maximum(ends // tm - 1, 0),
                                 jnp.arange(n_tiles - n_experts, n_tiles, dtype=jnp.int32)]).astype(jnp.int32)
    tile_table = jnp.concatenate([jnp.minimum(tile_expert, n_experts - 1), ends[-1:] // tm]).astype(jnp.int32)
    return pos.reshape(TOP_K, m), tile_table, pad_tiles


def kernel(x_prompt, x_sample, cache_k, cache_v, c, c_ctx, ada_w, ada_b, norm_g, attn_w_in, attn_lambda, attn_subln, attn_w_out, gmlp_w_in, gmlp_ln_g, gmlp_ln_b, gmlp_w_s, gmlp_b_s, gmlp_w_out, hyena_w_in, hyena_b_in, hyena_conv_w, hyena_conv_b, hyena_ffn_w1, hyena_ffn_b1, hyena_ffn_w2, hyena_ffn_b2, hyena_ffn_w3, hyena_sin_freq, hyena_bias, hyena_w_out, ffn_w_gate, ffn_w_up, ffn_w_down, moe_router, moe_w_gate, moe_w_up, moe_w_down):
    batch, seq, d = x_prompt.shape
    dec_batch, dec_seq, _ = x_sample.shape
    depth = ada_w.shape[0]
    n_attn = attn_w_in.shape[0]
    past = cache_k.shape[2]
    n_experts = moe_router.shape[-1]
    rows = _Rows(batch * seq, dec_batch * dec_seq, dec_seq)
    n_p, n_s, m = rows.n_p, rows.n_s, rows.m
    heads = d // (2 * HEAD_DIM)

    cond8 = jnp.concatenate([c_ctx[None, :], c, jnp.zeros((8 - rows.nseg, d), F32)], axis=0)
    mod = _modulation(cond8, ada_w, ada_b)[:, :rows.nseg]
    mod3 = mod.reshape(depth * rows.nseg, 6, d)
    ng3 = norm_g.reshape(depth * 4, 1, d)

    x, h = _prenorm_join(x_prompt.reshape(n_p, d), x_sample.reshape(n_s, d), ng3, mod3, rows, 0)

    rope = _rope_tables(dec_seq)
    ck4 = cache_k.reshape(dec_batch, n_attn, past, d)
    cv4 = cache_v.reshape(dec_batch, n_attn, past, d)
    new_k = new_v = None
    y_prompt = y_sample = None

    for i in range(depth):
        kind, j = i % 3, i // 3
        if kind == 0:
            lam_init = 0.8 - 0.6 * math.exp(-0.3 * i)
            q_p = _proj(h, attn_w_in, j, row0=0, nrows=n_p, col0=0, ncols=d)
            new_k = _proj(h, attn_w_in, j, row0=0, nrows=n_p, col0=d, ncols=d, out_dtype=F32,
                          cache=(new_k, n_attn, j, batch, seq))
            new_v = _proj(h, attn_w_in, j, row0=0, nrows=n_p, col0=2 * d, ncols=d, out_dtype=F32,
                          cache=(new_v, n_attn, j, batch, seq))
            qk_s = _proj(h, attn_w_in, j, row0=n_p, nrows=n_s, col0=0, ncols=2 * d, epilogue="rope", rope=rope,
                         seq=dec_seq)
            v_s = _proj(h, attn_w_in, j, row0=n_p, nrows=n_s, col0=2 * d, ncols=d)
            o_p = _attention(q_p, 0, new_k, j, 0, new_v, j, None, attn_lambda, attn_subln, j, lam_init,
                             batch=batch, seq=seq, heads_per_step=heads, bq=seq, lag=4 * heads)
            o_s = _attention(qk_s, 0, qk_s.reshape(dec_batch, 1, dec_seq, 2 * d), 0, d,
                             v_s.reshape(dec_batch, 1, dec_seq, d), 0, (ck4, cv4), attn_lambda, attn_subln, j,
                             lam_init, batch=dec_batch, seq=dec_seq, heads_per_step=min(4, heads),
                             bq=_pick(dec_seq, 256, 128), lag=4 * min(4, heads))
            mix, w_out = (o_p, o_s), attn_w_out
        elif kind == 1:
            uv = _proj(h, gmlp_w_in, j, row0=0, nrows=m, col0=0, ncols=2 * d, epilogue="gelu")
            mix, w_out = _gmlp_gate(uv, gmlp_ln_g, gmlp_ln_b, gmlp_w_s, gmlp_b_s, j), gmlp_w_out
        else:
            p = _proj(h, hyena_w_in, j, row0=0, nrows=m, col0=0, ncols=3 * d, epilogue="bias", bias=hyena_b_in)
            outs = []
            for row0, nb, length in ((0, batch, seq), (n_p, dec_batch, dec_seq)):
                fmat = _dft_matrix(length)
                a_tab, b_tab = _hyena_filters(length, d, hyena_ffn_w1, hyena_ffn_b1, hyena_ffn_w2, hyena_ffn_b2,
                                              hyena_ffn_w3, hyena_sin_freq, j)
                kh, kn = _hyena_spectrum(fmat, a_tab, b_tab)
                outs.append(_hyena_conv(p, row0, nb, length, fmat, kh, kn, hyena_conv_w, hyena_conv_b, hyena_bias, j))
            mix, w_out = tuple(outs), hyena_w_out

        moe_layer = i % 2 == 1
        jj = i // 2
        router_pad = jnp.pad(moe_router[jj], ((0, 0), (0, LANES - n_experts))) if moe_layer else None
        x, h2, logits = _mm_tail(mix, w_out[j].astype(BF16), x, rows, mod3, i, G1, ng3, i * 4 + 1,
                                 (i * 4 + 2, i, SC2, SH2), router_pad=router_pad)

        last = i == depth - 1
        nxt = None if last else ((i + 1) * 4, i + 1, SC1, SH1)
        if not moe_layer:
            tm = rows.tile(1024, 512, 256, 128)
            one_group = jnp.array([0] * (m // tm) + [m // tm], jnp.int32)
            a = _swiglu_up(h2, ffn_w_gate[:, None], ffn_w_up[:, None], jj, one_group, tm)
            x, h, _ = _mm_tail(a, ffn_w_down[jj].astype(BF16), x, rows, mod3, i, G2, ng3, i * 4 + 3, nxt)
        else:
            tm = rows.tile(512, 256, 128)
            idx, wts = _router(logits, n_experts)
            pos, tile_expert, pad_tiles = _routing_tables(jnp.transpose(idx[:, :TOP_K]), n_experts, tm)
            xs = _dispatch_rows(h2, pos, pad_tiles, TOP_K * m + n_experts * tm, tm)
            a = _swiglu_up(xs, moe_w_gate, moe_w_up, jj, tile_expert, tm, bn=min(1024, moe_w_gate.shape[-1]))
            ys = _expert_down(a, moe_w_down, jj, tile_expert, tm)
            tail_args = (ys, pos, wts, x, rows, mod3, i, G2, ng3, i * 4 + 3)
            if last:
                y_prompt, _ = _combine_tail(*tail_args, None, row0=0, nrows=n_p)
                y_sample, _ = _combine_tail(*tail_args, None, row0=n_p, nrows=n_s)
            else:
                x, h = _combine_tail(*tail_args, nxt, row0=0, nrows=m)

    if y_prompt is None:
        y_prompt, y_sample = x[:n_p], x[n_p:]
    y_prompt = y_prompt.reshape(batch, seq, d)
    y_sample = y_sample.reshape(dec_batch, dec_seq, d)
    hk = cache_k.shape[3]
    hv = cache_v.shape[3]
    return (y_prompt, y_sample, new_k.reshape(batch, n_attn, seq, hk, d // hk),
            new_v.reshape(batch, n_attn, seq, hv, d // hv))
```
